```python
import math
import jax, jax.numpy as jnp
from jax import lax
import numpy as np

D_MODEL = 1024
BATCH = 8
SEQ = 8192
DEPTH = 1

D_HYENA = 768
D_LRU = 768
D_MIX = D_HYENA + D_LRU
N_HYENA_GROUPS = 12
N_LRU_HEADS = 12
LRU_HEAD_DIM = D_LRU // N_LRU_HEADS
D_IN = 4 * D_HYENA + 2 * D_LRU
HYENA_SHORT_CONV = 3
LRU_CONV = 4
FILTER_EMB_DIM = 33
FILTER_BANDS = (FILTER_EMB_DIM - 1) // 2
FILTER_HIDDEN = 64
FILTER_TARGET = 1e-2
FAST_DECAY_PCT = 0.3
SLOW_DECAY_PCT = 1.5
MIN_DECAY = math.log(FILTER_TARGET) / FAST_DECAY_PCT
MAX_DECAY = math.log(FILTER_TARGET) / SLOW_DECAY_PCT
LRU_C = 8.0
EPS = 1e-6

kernel_name = "hymba_hyena_rglru_bidir_block"


def _rmsnorm(x, g):
    xf = x.astype(jnp.float32)
    y = xf * lax.rsqrt(jnp.mean(xf * xf, axis=-1, keepdims=True) + EPS)
    return (y * g.astype(jnp.float32)).astype(x.dtype)


def _dwconv(u, w, b, pad_lo, pad_hi):
    L = u.shape[1]
    up = jnp.pad(u, ((0, 0), (pad_lo, pad_hi), (0, 0)))
    y = b
    for k in range(w.shape[0]):
        y = y + up[:, k:k + L, :] * w[k]
    return y


def _hyena_filter(L, w1, b1, f1, w2, b2, f2, w3, b3, f3, w4):
    f32 = jnp.float32
    t = jnp.linspace(0.0, 1.0, L, dtype=f32)[:, None]
    w = (2.0 * math.pi / L) * jnp.arange(L, dtype=f32)[:, None]
    f = jnp.linspace(1e-4, FILTER_BANDS - 1, FILTER_BANDS, dtype=f32)[None, :]
    z = jnp.concatenate([t, jnp.cos(w * f), -jnp.sin(w * f)], axis=-1)
    h = jnp.sin(f1.astype(f32) * (z @ w1.astype(f32) + b1.astype(f32)))
    h = jnp.sin(f2.astype(f32) * (h @ w2.astype(f32) + b2.astype(f32)))
    h = jnp.sin(f3.astype(f32) * (h @ w3.astype(f32) + b3.astype(f32)))
    h = h @ w4.astype(f32)
    deltas = jnp.linspace(MIN_DECAY, MAX_DECAY, D_HYENA, dtype=f32)
    window = jnp.exp(-t * jnp.abs(deltas)[None, :])
    h_fwd = h[:, :D_HYENA] * window
    h_bwd = h[:, D_HYENA:] * window
    k = jnp.concatenate([h_fwd, jnp.zeros((1, D_HYENA), f32), h_bwd[:0:-1]], axis=0)
    return k / (jnp.sum(jnp.abs(k), axis=0, keepdims=True) + EPS)


def _fft_conv(u, k):
    L = u.shape[1]
    U = jnp.fft.rfft(u, n=2 * L, axis=1)
    K = jnp.fft.rfft(k, n=2 * L, axis=0)
    return jnp.fft.irfft(U * K[None], n=2 * L, axis=1)[:, :L]


def _lin_comb(c1, c2):
    a1, b1 = c1
    a2, b2 = c2
    return a1 * a2, a2 * b1 + b2


def _rglru_dir(xb, wa, ba, wx, bx, lam):
    B, L, _ = xb.shape
    xh = xb.reshape(B, L, N_LRU_HEADS, LRU_HEAD_DIM)
    r = jax.nn.sigmoid(jnp.einsum('blhi,hij->blhj', xh, wa).reshape(B, L, D_LRU) + ba)
    i = jax.nn.sigmoid(jnp.einsum('blhi,hij->blhj', xh, wx).reshape(B, L, D_LRU) + bx)
    log_a = -LRU_C * r * jax.nn.softplus(-lam)
    a = jnp.exp(log_a)
    mult = jnp.sqrt(-jnp.expm1(2.0 * log_a))
    mult = mult.at[:, 0].set(1.0)
    _, h = lax.associative_scan(_lin_comb, (a, mult * (i * xb)), axis=1)
    return h


def setup_inputs(seed: int = 0) -> dict:
    key = jax.random.key(seed)
    ks = iter(jax.random.split(key, 40))
    f32 = jnp.float32

    def nrm(shape, scale):
        return jax.random.normal(next(ks), shape, f32) * scale

    x = jax.random.normal(next(ks), (BATCH, SEQ, D_MODEL), f32)
    a_c = jax.random.uniform(next(ks), (DEPTH, 2, D_LRU), f32, 0.9, 0.999)
    a_base = a_c ** (1.0 / LRU_C)
    lru_lam = jnp.log(a_base) - jnp.log1p(-a_base)
    return {
        "x": x,
        "norm_g": 1.0 + nrm((DEPTH, D_MODEL), 0.02),
        "w_in": nrm((DEPTH, D_MODEL, D_IN), D_MODEL ** -0.5),
        "hy_conv_w": nrm((DEPTH, HYENA_SHORT_CONV, 3 * D_HYENA), HYENA_SHORT_CONV ** -0.5),
        "hy_conv_b": nrm((DEPTH, 3 * D_HYENA), 0.02),
        "flt_w1": nrm((DEPTH, FILTER_EMB_DIM, FILTER_HIDDEN), FILTER_EMB_DIM ** -0.5),
        "flt_b1": nrm((DEPTH, FILTER_HIDDEN), 0.02),
        "flt_f1": 1.0 + nrm((DEPTH, FILTER_HIDDEN), 0.02),
        "flt_w2": nrm((DEPTH, FILTER_HIDDEN, FILTER_HIDDEN), FILTER_HIDDEN ** -0.5),
        "flt_b2": nrm((DEPTH, FILTER_HIDDEN), 0.02),
        "flt_f2": 1.0 + nrm((DEPTH, FILTER_HIDDEN), 0.02),
        "flt_w3": nrm((DEPTH, FILTER_HIDDEN, FILTER_HIDDEN), FILTER_HIDDEN ** -0.5),
        "flt_b3": nrm((DEPTH, FILTER_HIDDEN), 0.02),
        "flt_f3": 1.0 + nrm((DEPTH, FILTER_HIDDEN), 0.02),
        "flt_w4": nrm((DEPTH, FILTER_HIDDEN, 2 * D_HYENA), FILTER_HIDDEN ** -0.5),
        "hy_skip": nrm((DEPTH, D_HYENA), 0.5),
        "lru_conv_w": nrm((DEPTH, LRU_CONV, D_LRU), LRU_CONV ** -0.5),
        "lru_conv_b": nrm((DEPTH, D_LRU), 0.02),
        "lru_wa": nrm((DEPTH, 2, N_LRU_HEADS, LRU_HEAD_DIM, LRU_HEAD_DIM), LRU_HEAD_DIM ** -0.5),
        "lru_ba": nrm((DEPTH, 2, D_LRU), 0.02),
        "lru_wx": nrm((DEPTH, 2, N_LRU_HEADS, LRU_HEAD_DIM, LRU_HEAD_DIM), LRU_HEAD_DIM ** -0.5),
        "lru_bx": nrm((DEPTH, 2, D_LRU), 0.02),
        "lru_lam": lru_lam,
        "hy_out_g": 1.0 + nrm((DEPTH, D_HYENA), 0.02),
        "lru_out_g": 1.0 + nrm((DEPTH, D_LRU), 0.02),
        "w_out": nrm((DEPTH, D_MIX, D_MODEL), D_MIX ** -0.5),
        "final_g": 1.0 + nrm((D_MODEL,), 0.02),
    }


def reference(x, norm_g, w_in, hy_conv_w, hy_conv_b, flt_w1, flt_b1, flt_f1,
              flt_w2, flt_b2, flt_f2, flt_w3, flt_b3, flt_f3, flt_w4, hy_skip,
              lru_conv_w, lru_conv_b, lru_wa, lru_ba, lru_wx, lru_bx, lru_lam,
              hy_out_g, lru_out_g, w_out, final_g):
    f32 = jnp.float32
    L = x.shape[1]
    for l in range(DEPTH):
        xn = _rmsnorm(x, norm_g[l])
        proj = jnp.einsum('bld,de->ble', xn, w_in[l]).astype(f32)
        o = 0
        hy_in = proj[..., o:o + 3 * D_HYENA]; o += 3 * D_HYENA
        hy_gate = proj[..., o:o + D_HYENA]; o += D_HYENA
        lru_in = proj[..., o:o + D_LRU]; o += D_LRU
        lru_gate = proj[..., o:o + D_LRU]

        hy = _dwconv(hy_in, hy_conv_w[l].astype(f32), hy_conv_b[l].astype(f32), 1, 1)
        v = hy[..., :D_HYENA]
        x0 = hy[..., D_HYENA:2 * D_HYENA]
        x1 = hy[..., 2 * D_HYENA:]
        kfilt = _hyena_filter(L, flt_w1[l], flt_b1[l], flt_f1[l], flt_w2[l], flt_b2[l],
                              flt_f2[l], flt_w3[l], flt_b3[l], flt_f3[l], flt_w4[l])
        u = v * x1
        y_hy = x0 * (_fft_conv(u, kfilt) + hy_skip[l].astype(f32) * u)

        xb = _dwconv(lru_in, lru_conv_w[l].astype(f32), lru_conv_b[l].astype(f32), 1, 2)
        h_f = _rglru_dir(xb, lru_wa[l, 0].astype(f32), lru_ba[l, 0].astype(f32),
                         lru_wx[l, 0].astype(f32), lru_bx[l, 0].astype(f32),
                         lru_lam[l, 0].astype(f32))
        h_b = _rglru_dir(xb[:, ::-1], lru_wa[l, 1].astype(f32), lru_ba[l, 1].astype(f32),
                         lru_wx[l, 1].astype(f32), lru_bx[l, 1].astype(f32),
                         lru_lam[l, 1].astype(f32))[:, ::-1]
        y_lru = h_f + h_b

        y_cat = jnp.concatenate([
            _rmsnorm(y_hy, hy_out_g[l]) * jax.nn.silu(hy_gate),
            _rmsnorm(y_lru, lru_out_g[l]) * jax.nn.silu(lru_gate)], axis=-1)
        y = jnp.einsum('ble,ed->bld', y_cat.astype(x.dtype), w_out[l])
        x = x + y.astype(x.dtype)
    return _rmsnorm(x, final_g)
```

```python
import functools
import math

import jax
import jax.numpy as jnp
from jax import lax
from jax.experimental import pallas as pl
from jax.experimental.pallas import tpu as pltpu

f32 = jnp.float32
bf16 = jnp.bfloat16

D_HY = 768
D_LRU = 768
HEAD = 64
LANES = 128
SUBLANES = 8
NB = 128
TILE = SUBLANES * NB
HALO = SUBLANES
LRU_CHUNK = 256
FILTER_BANDS = 16
FILTER_EMB = 2 * FILTER_BANDS + 1
MASK_COL = FILTER_EMB
FILTER_TARGET = 1e-2
MIN_DECAY = math.log(FILTER_TARGET) / 0.3
MAX_DECAY = math.log(FILTER_TARGET) / 1.5
LRU_C = 8.0
EPS = 1e-6
VMEM_LIMIT = 60 * 1024 * 1024


def _params(*sem):
    return pltpu.CompilerParams(dimension_semantics=sem, vmem_limit_bytes=VMEM_LIMIT)


def _rms(y, g):
    return y * lax.rsqrt(jnp.mean(y * y, axis=-1, keepdims=True) + EPS) * g


def _sigmoid(x):
    return 0.5 * jnp.tanh(0.5 * x) + 0.5


def _inproj_kernel(x_ref, xp_ref, xn_ref, g_ref, w_ref, hcw_ref, hcb_ref, lcw_ref, lcb_ref,
                   u_ref, x0_ref, hg_ref, xb_ref, lg_ref, *, n_tiles):
    i = pl.program_id(1)
    rows = TILE + 2 * HALO
    xt = jnp.concatenate([xp_ref[...], x_ref[...], xn_ref[...]], axis=0)
    xn = _rms(xt, g_ref[...])
    row = lax.broadcasted_iota(jnp.int32, (rows, 1), 0)
    inside = ((row >= HALO) | (i > 0)) & ((row < TILE + HALO) | (i < n_tiles - 1))
    xn = jnp.where(inside, xn, 0.0)
    xm = xn[HALO:HALO + TILE].astype(bf16)
    xn = xn.astype(bf16)

    def proj(c0):
        return jnp.dot(xn, w_ref[:, c0:c0 + D_HY], preferred_element_type=f32)

    def conv(p, cw_ref, cb_ref, c0, offsets):
        y = cb_ref[:, c0:c0 + D_HY]
        for k, o in enumerate(offsets):
            y = y + p[HALO + o:HALO + o + TILE] * cw_ref[k:k + 1, c0:c0 + D_HY]
        return y

    def store_ba(ref, val):
        for a in range(SUBLANES):
            ref[:, a, :] = val[a * NB:(a + 1) * NB]

    v = conv(proj(0), hcw_ref, hcb_ref, 0, (-1, 0, 1))
    x1 = conv(proj(2 * D_HY), hcw_ref, hcb_ref, 2 * D_HY, (-1, 0, 1))
    store_ba(u_ref, v * x1)
    store_ba(x0_ref, conv(proj(D_HY), hcw_ref, hcb_ref, D_HY, (-1, 0, 1)))
    hg = jnp.dot(xm, w_ref[:, 3 * D_HY:4 * D_HY], preferred_element_type=f32)
    hg_ref[...] = hg * _sigmoid(hg)
    xb_ref[...] = conv(proj(4 * D_HY), lcw_ref, lcb_ref, 0, (-1, 0, 1, 2))
    lg = jnp.dot(xm, w_ref[:, 4 * D_HY + D_LRU:], preferred_element_type=f32)
    lg_ref[...] = lg * _sigmoid(lg)


def _inproj(x, norm_g, w_in, hcw, hcb, lcw, lcb):
    B, L, D = x.shape
    n_tiles = L // TILE
    ha = L // NB
    hb = TILE // HALO
    n_hb = L // HALO
    const = lambda b, i: (0, 0)
    ba_shape = jax.ShapeDtypeStruct((B // 2, NB, 2, ha, D_HY), f32)
    ba_spec = pl.BlockSpec((None, NB, None, SUBLANES, D_HY), lambda b, i: (b // 2, 0, b % 2, i, 0))
    nat_shape = jax.ShapeDtypeStruct((B, L, D_HY), f32)
    nat_spec = pl.BlockSpec((None, TILE, D_HY), lambda b, i: (b, i, 0))
    return pl.pallas_call(
        functools.partial(_inproj_kernel, n_tiles=n_tiles),
        grid=(B, n_tiles),
        in_specs=[
            pl.BlockSpec((None, TILE, D), lambda b, i: (b, i, 0)),
            pl.BlockSpec((None, HALO, D), lambda b, i: (b, jnp.maximum(i * hb - 1, 0), 0)),
            pl.BlockSpec((None, HALO, D), lambda b, i: (b, jnp.minimum((i + 1) * hb, n_hb - 1), 0)),
            pl.BlockSpec((1, D), const),
            pl.BlockSpec(w_in.shape, const, pipeline_mode=pl.Buffered(1)),
            pl.BlockSpec(hcw.shape, const),
            pl.BlockSpec(hcb.shape, const),
            pl.BlockSpec(lcw.shape, const),
            pl.BlockSpec(lcb.shape, const),
        ],
        out_specs=[ba_spec, ba_spec, nat_spec, nat_spec, nat_spec],
        out_shape=[ba_shape, ba_shape, nat_shape, nat_shape, nat_shape],
        compiler_params=_params("parallel", "arbitrary"),
        name="inproj",
    )(x, x, x, norm_g, w_in, hcw, hcb, lcw, lcb)


def _filt_kernel(zf_ref, zb_ref, dl_ref, w1_ref, b1_ref, f1_ref, w2_ref, b2_ref, f2_ref,
                 w3_ref, b3_ref, f3_ref, w4f_ref, w4b_ref, o_ref, *, ha):
    hi = lax.Precision.HIGHEST
    cb = o_ref.shape[-1]
    rc = SUBLANES * ha
    dl = dl_ref[...]

    def half(z, w4_ref):
        h = jnp.sin(f1_ref[...] * (jnp.dot(z, w1_ref[...], precision=hi, preferred_element_type=f32) + b1_ref[...]))
        h = jnp.sin(f2_ref[...] * (jnp.dot(h, w2_ref[...], precision=hi, preferred_element_type=f32) + b2_ref[...]))
        h = jnp.sin(f3_ref[...] * (jnp.dot(h, w3_ref[...], precision=hi, preferred_element_type=f32) + b3_ref[...]))
        h = jnp.dot(h, w4_ref[...], precision=hi, preferred_element_type=f32)
        return h * jnp.exp(-z[:, 0:1] * dl)

    def fill(c, s):
        r0 = pl.multiple_of(c * rc, rc)
        b0 = pl.multiple_of(c * SUBLANES, SUBLANES)
        hf = half(zf_ref[pl.ds(r0, rc), :], w4f_ref)
        zb = zb_ref[pl.ds(r0, rc), :]
        hb = half(zb, w4b_ref) * zb[:, MASK_COL:MASK_COL + 1]
        o_ref[pl.ds(b0, SUBLANES), :ha, :] = hf.reshape(SUBLANES, ha, cb)
        o_ref[pl.ds(b0, SUBLANES), ha:, :] = hb.reshape(SUBLANES, ha, cb)
        return s + jnp.sum(jnp.abs(hf), axis=0, keepdims=True) + jnp.sum(jnp.abs(hb), axis=0, keepdims=True)

    s = lax.fori_loop(0, NB // SUBLANES, fill, jnp.zeros((1, cb), f32))
    inv = (1.0 / (s + EPS))[None]

    def scale(c, carry):
        b0 = pl.multiple_of(c * SUBLANES, SUBLANES)
        o_ref[pl.ds(b0, SUBLANES)] = o_ref[pl.ds(b0, SUBLANES)] * inv
        return carry

    lax.fori_loop(0, NB // SUBLANES, scale, 0)


def _filter_tables(L):
    ha = L // NB
    t = jnp.linspace(0.0, 1.0, L, dtype=f32)[:, None]
    w = (2.0 * math.pi / L) * jnp.arange(L, dtype=f32)[:, None]
    f = jnp.linspace(1e-4, FILTER_BANDS - 1, FILTER_BANDS, dtype=f32)[None, :]
    z = jnp.concatenate([t, jnp.cos(w * f), -jnp.sin(w * f)], axis=-1)
    z = jnp.pad(z, ((0, 0), (0, LANES - z.shape[1])))
    b = jnp.arange(NB, dtype=jnp.int32)[:, None]
    a = jnp.arange(ha, dtype=jnp.int32)[None, :]
    lag_f = (NB * a + b).reshape(-1)
    lag_b = (L - NB * a - b).reshape(-1)
    zb = z[jnp.where(lag_b < L, lag_b, 0)].at[:, MASK_COL].set((lag_b < L).astype(f32))
    return z[lag_f], zb


def _filter(L, w1, b1, f1, w2, b2, f2, w3, b3, f3, w4):
    ha = L // NB
    cb = LANES
    zf, zb = _filter_tables(L)
    dl = jnp.abs(jnp.linspace(MIN_DECAY, MAX_DECAY, D_HY, dtype=f32))[None, :]
    w1p = jnp.pad(w1, ((0, LANES - w1.shape[0]), (0, 0)))
    row = lambda v: v.reshape(1, -1)
    const = lambda j: (0, 0)
    full = lambda arr: pl.BlockSpec(arr.shape, const)
    args = [zf, zb, dl, w1p, row(b1), row(f1), w2, row(b2), row(f2), w3, row(b3), row(f3), w4, w4]
    specs = [full(a) for a in args]
    specs[2] = pl.BlockSpec((1, cb), lambda j: (0, j))
    specs[12] = pl.BlockSpec((w4.shape[0], cb), lambda j: (0, j))
    specs[13] = pl.BlockSpec((w4.shape[0], cb), lambda j: (0, j + D_HY // cb))
    return pl.pallas_call(
        functools.partial(_filt_kernel, ha=ha),
        grid=(D_HY // cb,),
        in_specs=specs,
        out_specs=pl.BlockSpec((None, NB, 2 * ha, cb), lambda j: (0, 0, 0, j)),
        out_shape=jax.ShapeDtypeStruct((1, NB, 2 * ha, D_HY), f32),
        compiler_params=_params("parallel"),
        name="filt",
    )(*args)


def _cos_sin(idx, n):
    ang = (2.0 * math.pi / n) * (idx % n).astype(f32)
    return jnp.cos(ang), jnp.sin(ang)


def _dft_matrices(L):
    n = 2 * L
    na = n // NB
    ha = na // 2
    b = jnp.arange(NB, dtype=jnp.int32)[:, None, None]
    p = jnp.arange(na, dtype=jnp.int32)[None, :, None]
    a = jnp.arange(na, dtype=jnp.int32)[None, None, :]
    c, s = _cos_sin(p * (NB * a + b), n)
    ch, sh = c[:, :, :ha], s[:, :, :ha]
    m_data = jnp.concatenate([jnp.concatenate([ch, sh], axis=2),
                              jnp.concatenate([-sh, ch], axis=2)], axis=1)
    m_real = jnp.concatenate([c, -s], axis=1)
    cht, sht = jnp.swapaxes(ch, 1, 2), jnp.swapaxes(sh, 1, 2)
    m_inv = jnp.concatenate([jnp.concatenate([cht, -sht], axis=2),
                             jnp.concatenate([sht, cht], axis=2)], axis=1)
    q = jnp.arange(NB, dtype=jnp.int32)
    gc, gs = _cos_sin(q[:, None] * q[None, :], NB)
    g_fwd = jnp.concatenate([jnp.concatenate([gc, gs], axis=1),
                             jnp.concatenate([-gs, gc], axis=1)], axis=0)
    g_inv = jnp.concatenate([jnp.concatenate([gc, -gs], axis=1),
                             jnp.concatenate([gs, gc], axis=1)], axis=0)
    return (m_data.astype(bf16), m_real.astype(bf16), m_inv.astype(bf16),
            g_fwd.astype(bf16), g_inv.astype(bf16))


def _ffta_kernel(x_ref, m_ref, o_ref, *, bt, na):
    for j in range(bt):
        res = jnp.dot(m_ref[j], x_ref[j].astype(bf16), preferred_element_type=f32)
        o_ref[0, j] = res[:na]
        o_ref[1, j] = res[na:]


def _fft_a(x, m, bt=SUBLANES):
    P, _, na, C = x.shape
    return pl.pallas_call(
        functools.partial(_ffta_kernel, bt=bt, na=na),
        grid=(P, NB // bt),
        in_specs=[pl.BlockSpec((None, bt, na, C), lambda r, j: (r, j, 0, 0)),
                  pl.BlockSpec((bt, 2 * na, na), lambda r, j: (j, 0, 0))],
        out_specs=pl.BlockSpec((None, 2, bt, na, C), lambda r, j: (r, 0, j, 0, 0)),
        out_shape=jax.ShapeDtypeStruct((P, 2, NB, na, C), f32),
        compiler_params=_params("parallel", "arbitrary"),
        name="fft_a",
    )(x, m)


def _fftb_kernel(y_ref, k_ref, gf_ref, gi_ref, o_ref, *, pt, scale):
    for j in range(pt):
        rhs = jnp.concatenate([y_ref[0, :, j, :], y_ref[1, :, j, :]], axis=0).astype(bf16)
        z = jnp.dot(gf_ref[...], rhs, preferred_element_type=f32)
        zr, zi = z[:NB], z[NB:]
        if k_ref is None:
            o_ref[0, j] = zr * scale
            o_ref[1, j] = zi * scale
        else:
            kr, ki = k_ref[0, j], k_ref[1, j]
            fr = zr * kr - zi * ki
            fi = zr * ki + zi * kr
            v = jnp.dot(gi_ref[...], jnp.concatenate([fr, fi], axis=0).astype(bf16),
                        preferred_element_type=f32)
            o_ref[0, j] = v[:NB]
            o_ref[1, j] = v[NB:]


def _fft_b(y, kf, g_fwd, g_inv, scale=1.0, pt=SUBLANES):
    P, _, _, na, C = y.shape
    const = lambda c, r: (0, 0)
    y_spec = pl.BlockSpec((None, 2, NB, pt, C), lambda c, r: (r, 0, 0, c, 0))
    g_spec = pl.BlockSpec(g_fwd.shape, const)
    out_spec = pl.BlockSpec((None, 2, pt, NB, C), lambda c, r: (r, 0, c, 0, 0))
    if kf is None:
        body = lambda y_ref, gf_ref, o_ref: _fftb_kernel(y_ref, None, gf_ref, None, o_ref, pt=pt, scale=scale)
        in_specs, args = [y_spec, g_spec], (y, g_fwd)
    else:
        body = functools.partial(_fftb_kernel, pt=pt, scale=scale)
        k_spec = pl.BlockSpec((2, pt, NB, C), lambda c, r: (0, c, 0, 0))
        in_specs, args = [y_spec, k_spec, g_spec, g_spec], (y, kf, g_fwd, g_inv)
    return pl.pallas_call(
        body,
        grid=(na // pt, P),
        in_specs=in_specs,
        out_specs=out_spec,
        out_shape=jax.ShapeDtypeStruct((P, 2, na, NB, C), f32),
        compiler_params=_params("parallel", "arbitrary"),
        name="fft_b",
    )(*args)


def _fftc_kernel(v_ref, m_ref, x0_ref, u_ref, sk_ref, o_ref, *, bt):
    for j in range(bt):
        rhs = jnp.concatenate([v_ref[0, :, j, :], v_ref[1, :, j, :]], axis=0).astype(bf16)
        conv = jnp.dot(m_ref[j], rhs, preferred_element_type=f32)
        o_ref[j] = x0_ref[j] * (conv + sk_ref[...] * u_ref[j])


def _fft_c(v, m_inv, x0, u, skip, bt=SUBLANES):
    P, _, na, _, C = v.shape
    ba_spec = pl.BlockSpec((None, bt, na, C), lambda r, j: (r, j, 0, 0))
    return pl.pallas_call(
        functools.partial(_fftc_kernel, bt=bt),
        grid=(P, NB // bt),
        in_specs=[pl.BlockSpec((None, 2, na, bt, C), lambda r, j: (r, 0, 0, j, 0)),
                  pl.BlockSpec((bt, na, 2 * na), lambda r, j: (j, 0, 0)),
                  ba_spec, ba_spec,
                  pl.BlockSpec((1, C), lambda r, j: (0, 0))],
        out_specs=ba_spec,
        out_shape=jax.ShapeDtypeStruct((P, NB, na, C), f32),
        compiler_params=_params("parallel", "arbitrary"),
        name="fft_c",
    )(v, m_inv, x0, u, skip)


def _scan_chunk(a, b, h0, sa_ref, sb_ref, reverse):
    T = a.shape[0]
    nv = T // SUBLANES
    a3 = a.reshape(nv, SUBLANES, LANES)
    b3 = b.reshape(nv, SUBLANES, LANES)
    sub = lax.broadcasted_iota(jnp.int32, (nv, SUBLANES, LANES), 1)
    s = 1
    while s < SUBLANES:
        keep = (sub < SUBLANES - s) if reverse else (sub >= s)
        shift = SUBLANES - s if reverse else s
        ash = jnp.where(keep, pltpu.roll(a3, shift, 1), 1.0)
        bsh = jnp.where(keep, pltpu.roll(b3, shift, 1), 0.0)
        b3 = a3 * bsh + b3
        a3 = a3 * ash
        s *= 2
    a2 = a3.reshape(T, LANES)
    b2 = b3.reshape(T, LANES)
    sa_ref[...] = a2
    sb_ref[...] = b2
    last = 0 if reverse else SUBLANES - 1
    at = sa_ref[pl.ds(last, nv, stride=SUBLANES), :]
    bt = sb_ref[pl.ds(last, nv, stride=SUBLANES), :]
    row = lax.broadcasted_iota(jnp.int32, (nv, LANES), 0)
    s = 1
    while s < nv:
        keep = (row < nv - s) if reverse else (row >= s)
        shift = nv - s if reverse else s
        ash = jnp.where(keep, pltpu.roll(at, shift, 0), 1.0)
        bsh = jnp.where(keep, pltpu.roll(bt, shift, 0), 0.0)
        bt = at * bsh + bt
        at = at * ash
        s *= 2
    hc = at * h0 + bt
    if reverse:
        cin = jnp.where(row < nv - 1, pltpu.roll(hc, nv - 1, 0), h0)
        h_end = hc[0:1]
    else:
        cin = jnp.where(row >= 1, pltpu.roll(hc, 1, 0), h0)
        h_end = hc[nv - 1:nv]
    cb = jnp.broadcast_to(cin[:, None, :], (nv, SUBLANES, LANES))
    return (a3 * cb + b3).reshape(T, LANES), h_end


def _lru_kernel(xb_ref, wg_ref, bg_ref, lam_ref, o_ref, sa_ref, sb_ref, *, seq):
    nc = seq // LRU_CHUNK
    lam = lam_ref[...]
    neg_c_sp = -LRU_C * (jnp.maximum(-lam, 0.0) + jnp.log1p(jnp.exp(-jnp.abs(lam))))
    rowi = lax.broadcasted_iota(jnp.int32, (LRU_CHUNK, 1), 0)

    def one(d, k, h0, first_row):
        t0 = pl.multiple_of(k * LRU_CHUNK, LRU_CHUNK)
        xc = xb_ref[pl.ds(t0, LRU_CHUNK), :]
        g = jnp.dot(xc.astype(bf16), wg_ref[d], preferred_element_type=f32) + bg_ref[d:d + 1, :]
        r = _sigmoid(g[:, :LANES])
        ig = _sigmoid(g[:, LANES:])
        log_a = neg_c_sp[d:d + 1, :] * r
        a = jnp.exp(log_a)
        mult = jnp.sqrt(-jnp.tanh(log_a) * (a * a + 1.0))
        mult = jnp.where(rowi + t0 == first_row, 1.0, mult)
        return _scan_chunk(a, mult * (ig * xc), h0, sa_ref, sb_ref, reverse=(d == 1)), t0

    def make_body(accumulate):
        def body(k, carry):
            hf0, hb0 = carry
            (hf, hf1), tf = one(0, k, hf0, 0)
            if accumulate:
                o_ref[pl.ds(tf, LRU_CHUNK), :] += hf
            else:
                o_ref[pl.ds(tf, LRU_CHUNK), :] = hf
            (hb, hb1), tb = one(1, nc - 1 - k, hb0, seq - 1)
            if accumulate:
                o_ref[pl.ds(tb, LRU_CHUNK), :] += hb
            else:
                o_ref[pl.ds(tb, LRU_CHUNK), :] = hb
            return hf1, hb1
        return body

    zero = jnp.zeros((1, LANES), f32)
    carry = lax.fori_loop(0, nc // 2, make_body(False), (zero, zero))
    lax.fori_loop(nc // 2, nc, make_body(True), carry)


def _lru_gate_weights(wa, wx):
    def blockdiag(w):
        w = w.reshape(2, -1, 2, HEAD, HEAD)
        z = jnp.zeros_like(w[:, :, 0])
        top = jnp.concatenate([w[:, :, 0], z], axis=-1)
        bot = jnp.concatenate([z, w[:, :, 1]], axis=-1)
        return jnp.concatenate([top, bot], axis=-2)
    return jnp.concatenate([blockdiag(wa), blockdiag(wx)], axis=-1).astype(bf16)


def _lru(xb, wa, ba, wx, bx, lam):
    B, L, C = xb.shape
    nblk = C // LANES
    wg = _lru_gate_weights(wa, wx)
    bg = jnp.concatenate([ba.reshape(2, nblk, 1, LANES), bx.reshape(2, nblk, 1, LANES)], axis=-1)
    return pl.pallas_call(
        functools.partial(_lru_kernel, seq=L),
        grid=(B, nblk),
        in_specs=[pl.BlockSpec((None, L, LANES), lambda b, c: (b, 0, c)),
                  pl.BlockSpec((2, None, LANES, 2 * LANES), lambda b, c: (0, c, 0, 0)),
                  pl.BlockSpec((2, None, None, 2 * LANES), lambda b, c: (0, c, 0, 0)),
                  pl.BlockSpec((2, LANES), lambda b, c: (0, c))],
        out_specs=pl.BlockSpec((None, L, LANES), lambda b, c: (b, 0, c)),
        out_shape=jax.ShapeDtypeStruct((B, L, C), f32),
        scratch_shapes=[pltpu.VMEM((LRU_CHUNK, LANES), f32),
                        pltpu.VMEM((LRU_CHUNK, LANES), f32)],
        compiler_params=_params("parallel", "arbitrary"),
        name="lru",
    )(xb, wg, bg, lam)


def _out_kernel(yh_ref, hg_ref, yl_ref, lg_ref, x_ref, hog_ref, log_ref, wo_ref, fg_ref, o_ref):
    yh = jnp.concatenate([yh_ref[:, a, :] for a in range(SUBLANES)], axis=0)
    ycat = jnp.concatenate([_rms(yh, hog_ref[...]) * hg_ref[...],
                            _rms(yl_ref[...], log_ref[...]) * lg_ref[...]], axis=-1)
    y = jnp.dot(ycat.astype(bf16), wo_ref[...], preferred_element_type=f32)
    o_ref[...] = _rms(x_ref[...] + y, fg_ref[...])


def _out(yh, hg, yl, lg, x, hog, log_g, w_out, fg):
    B, L, D = x.shape
    const = lambda b, i: (0, 0)
    nat = lambda c: pl.BlockSpec((None, TILE, c), lambda b, i: (b, i, 0))
    return pl.pallas_call(
        _out_kernel,
        grid=(B, L // TILE),
        in_specs=[pl.BlockSpec((None, NB, None, SUBLANES, D_HY), lambda b, i: (b // 2, 0, b % 2, i, 0)),
                  nat(D_HY), nat(D_LRU), nat(D_LRU), nat(D),
                  pl.BlockSpec((1, D_HY), const), pl.BlockSpec((1, D_LRU), const),
                  pl.BlockSpec(w_out.shape, const), pl.BlockSpec((1, D), const)],
        out_specs=nat(D),
        out_shape=jax.ShapeDtypeStruct((B, L, D), f32),
        compiler_params=_params("parallel", "arbitrary"),
        name="out",
    )(yh, hg, yl, lg, x, hog, log_g, w_out, fg)


def kernel(x, norm_g, w_in, hy_conv_w, hy_conv_b, flt_w1, flt_b1, flt_f1, flt_w2, flt_b2, flt_f2,
           flt_w3, flt_b3, flt_f3, flt_w4, hy_skip, lru_conv_w, lru_conv_b, lru_wa, lru_ba, lru_wx,
           lru_bx, lru_lam, hy_out_g, lru_out_g, w_out, final_g):
    B, L, D = x.shape
    assert norm_g.shape[0] == 1, "one layer"
    assert B % 2 == 0 and L % TILE == 0
    ha = L // NB
    na = 2 * ha
    row = lambda v: v.reshape(1, -1)

    u, x0, hg, xb, lg = _inproj(x, row(norm_g[0]), w_in[0].astype(bf16), hy_conv_w[0], row(hy_conv_b[0]),
                                lru_conv_w[0], row(lru_conv_b[0]))
    u = u.reshape(B // 2, NB, na, D_HY)
    x0 = x0.reshape(B // 2, NB, na, D_HY)

    m_data, m_real, m_inv, g_fwd, g_inv = _dft_matrices(L)
    kt = _filter(L, flt_w1[0], flt_b1[0], flt_f1[0], flt_w2[0], flt_b2[0], flt_f2[0],
                 flt_w3[0], flt_b3[0], flt_f3[0], flt_w4[0])
    kf = _fft_b(_fft_a(kt, m_real), None, g_fwd, g_inv, scale=1.0 / (2 * L))[0]

    v = _fft_b(_fft_a(u, m_data), kf, g_fwd, g_inv)
    yh = _fft_c(v, m_inv, x0, u, row(hy_skip[0]))
    yh = yh.reshape(B // 2, NB, 2, ha, D_HY)

    yl = _lru(xb, lru_wa[0], lru_ba[0], lru_wx[0], lru_bx[0], lru_lam[0])
    return _out(yh, hg, yl, lg, x, row(hy_out_g[0]), row(lru_out_g[0]), w_out[0].astype(bf16), row(final_g))
```

```python
import functools
import math

import jax
import jax.numpy as jnp
from jax import lax
from jax.experimental import pallas as pl
from jax.experimental.pallas import tpu as pltpu

f32 = jnp.float32
bf16 = jnp.bfloat16

D_HY = 768
D_LRU = 768
HEAD = 64
LANES = 128
SUBLANES = 8
NB = 128
TILE = SUBLANES * NB
HALO = SUBLANES
FILTER_BANDS = 16
FILTER_EMB = 2 * FILTER_BANDS + 1
MASK_COL = FILTER_EMB
FILTER_TARGET = 1e-2
MIN_DECAY = math.log(FILTER_TARGET) / 0.3
MAX_DECAY = math.log(FILTER_TARGET) / 1.5
LRU_C = 8.0
EPS = 1e-6
VMEM_LIMIT = 60 * 1024 * 1024


def _params(*sem):
    return pltpu.CompilerParams(dimension_semantics=sem, vmem_limit_bytes=VMEM_LIMIT)


def _rms(y, g):
    return y * lax.rsqrt(jnp.mean(y * y, axis=-1, keepdims=True) + EPS) * g


def _sigmoid(x):
    return 0.5 * jnp.tanh(0.5 * x) + 0.5


def _inproj_kernel(x_ref, xp_ref, xn_ref, g_ref, w_ref, hcw_ref, hcb_ref, lcw_ref, lcb_ref,
                   u_ref, x0_ref, hg_ref, xb_ref, lg_ref, xs_ref, *, n_tiles):
    i = pl.program_id(1)
    g = g_ref[...]
    for a in range(SUBLANES):
        xs_ref[0:NB, a, :] = _rms(x_ref[a * NB:(a + 1) * NB, :], g)
    xs_ref[NB] = jnp.where(i > 0, _rms(xp_ref[...], g), 0.0)
    xs_ref[NB + 1] = jnp.where(i < n_tiles - 1, _rms(xn_ref[...], g), 0.0)
    xn = xs_ref[...].reshape(TILE + 2 * HALO, -1).astype(bf16)
    sub = lax.broadcasted_iota(jnp.int32, (SUBLANES, D_HY), 0)

    def proj(c0):
        p = jnp.dot(xn, w_ref[:, c0:c0 + D_HY], preferred_element_type=f32)
        return p.reshape(NB + 2, SUBLANES, D_HY)

    def edge(p3, s):
        if s < 0:
            return jnp.where(sub == 0, p3[NB][SUBLANES + s:SUBLANES + s + 1],
                             pltpu.roll(p3[NB + s], 1, 0))
        return jnp.where(sub == SUBLANES - 1, p3[NB + 1][s - NB:s - NB + 1],
                         pltpu.roll(p3[s - NB], SUBLANES - 1, 0))

    def conv(p3, cw_ref, cb_ref, c0, offsets):
        lo, hi = max(0, -min(offsets)), NB - max(offsets)

        def acc(get):
            y = cb_ref[:, c0:c0 + D_HY]
            for k, o in enumerate(offsets):
                y = y + get(o) * cw_ref[k:k + 1, c0:c0 + D_HY]
            return y

        inner = acc(lambda o: p3[lo + o:hi + o])
        edges = {b: acc(lambda o, b=b: p3[b + o] if 0 <= b + o < NB else edge(p3, b + o))
                 for b in list(range(lo)) + list(range(hi, NB))}
        return lo, hi, inner, edges

    def store(ref, conv_out, other=None):
        lo, hi, inner, edges = conv_out
        if other is not None:
            inner = inner * other[2]
            edges = {b: edges[b] * other[3][b] for b in edges}
        ref[lo:hi] = inner
        for b, y in edges.items():
            ref[b] = y

    hy = (-1, 0, 1)
    store(u_ref, conv(proj(0), hcw_ref, hcb_ref, 0, hy), conv(proj(2 * D_HY), hcw_ref, hcb_ref, 2 * D_HY, hy))
    store(x0_ref, conv(proj(D_HY), hcw_ref, hcb_ref, D_HY, hy))
    store(xb_ref, conv(proj(4 * D_HY), lcw_ref, lcb_ref, 0, (-1, 0, 1, 2)))
    xm = xn[:TILE]
    hg = jnp.dot(xm, w_ref[:, 3 * D_HY:4 * D_HY], preferred_element_type=f32)
    hg_ref[...] = (hg * _sigmoid(hg)).reshape(NB, SUBLANES, D_HY)
    lg = jnp.dot(xm, w_ref[:, 4 * D_HY + D_LRU:], preferred_element_type=f32)
    lg_ref[...] = (lg * _sigmoid(lg)).reshape(NB, SUBLANES, D_LRU)


def _ba_spec(c):
    return pl.BlockSpec((None, NB, None, SUBLANES, c), lambda b, i: (b // 2, 0, b % 2, i, 0))


def _inproj(x, norm_g, w_in, hcw, hcb, lcw, lcb):
    B, L, D = x.shape
    n_tiles = L // TILE
    ha = L // NB
    hb = TILE // HALO
    n_hb = L // HALO
    const = lambda b, i: (0, 0)
    ba_shape = jax.ShapeDtypeStruct((B // 2, NB, 2, ha, D_HY), f32)
    return pl.pallas_call(
        functools.partial(_inproj_kernel, n_tiles=n_tiles),
        grid=(B, n_tiles),
        in_specs=[
            pl.BlockSpec((None, TILE, D), lambda b, i: (b, i, 0)),
            pl.BlockSpec((None, HALO, D), lambda b, i: (b, jnp.maximum(i * hb - 1, 0), 0)),
            pl.BlockSpec((None, HALO, D), lambda b, i: (b, jnp.minimum((i + 1) * hb, n_hb - 1), 0)),
            pl.BlockSpec((1, D), const),
            pl.BlockSpec(w_in.shape, const, pipeline_mode=pl.Buffered(1)),
            pl.BlockSpec(hcw.shape, const),
            pl.BlockSpec(hcb.shape, const),
            pl.BlockSpec(lcw.shape, const),
            pl.BlockSpec(lcb.shape, const),
        ],
        out_specs=[_ba_spec(D_HY)] * 5,
        out_shape=[ba_shape] * 5,
        scratch_shapes=[pltpu.VMEM((NB + 2, SUBLANES, D), f32)],
        compiler_params=_params("parallel", "arbitrary"),
        name="inproj",
    )(x, x, x, norm_g, w_in, hcw, hcb, lcw, lcb)


def _filt_kernel(zf_ref, zb_ref, dl_ref, w1_ref, b1_ref, f1_ref, w2_ref, b2_ref, f2_ref,
                 w3_ref, b3_ref, f3_ref, w4f_ref, w4b_ref, o_ref, s_ref, *, ha):
    hi = lax.Precision.HIGHEST
    dl = dl_ref[...]

    def half(z, w4_ref):
        h = jnp.sin(f1_ref[...] * (jnp.dot(z, w1_ref[...], precision=hi, preferred_element_type=f32) + b1_ref[...]))
        h = jnp.sin(f2_ref[...] * (jnp.dot(h, w2_ref[...], precision=hi, preferred_element_type=f32) + b2_ref[...]))
        h = jnp.sin(f3_ref[...] * (jnp.dot(h, w3_ref[...], precision=hi, preferred_element_type=f32) + b3_ref[...]))
        h = jnp.dot(h, w4_ref[...], precision=hi, preferred_element_type=f32)
        return h * jnp.exp(-z[:, 0:1] * dl)

    hf = half(zf_ref[...], w4f_ref)
    zb = zb_ref[...]
    hb = half(zb, w4b_ref) * zb[:, MASK_COL:MASK_COL + 1]
    o_ref[:, :ha, :] = hf.reshape(SUBLANES, ha, D_HY)
    o_ref[:, ha:, :] = hb.reshape(SUBLANES, ha, D_HY)

    @pl.when(pl.program_id(0) == 0)
    def _():
        s_ref[...] = jnp.zeros_like(s_ref)

    s_ref[...] += jnp.sum(jnp.abs(hf), axis=0, keepdims=True) + jnp.sum(jnp.abs(hb), axis=0, keepdims=True)


def _filter_tables(L):
    ha = L // NB
    t = jnp.linspace(0.0, 1.0, L, dtype=f32)[:, None]
    w = (2.0 * math.pi / L) * jnp.arange(L, dtype=f32)[:, None]
    f = jnp.linspace(1e-4, FILTER_BANDS - 1, FILTER_BANDS, dtype=f32)[None, :]
    z = jnp.concatenate([t, jnp.cos(w * f), -jnp.sin(w * f)], axis=-1)
    z = jnp.pad(z, ((0, 0), (0, LANES - z.shape[1])))
    b = jnp.arange(NB, dtype=jnp.int32)[:, None]
    a = jnp.arange(ha, dtype=jnp.int32)[None, :]
    lag_f = (NB * a + b).reshape(-1)
    lag_b = (L - NB * a - b).reshape(-1)
    zb = z[jnp.where(lag_b < L, lag_b, 0)].at[:, MASK_COL].set((lag_b < L).astype(f32))
    return z[lag_f], zb


def _filter(L, w1, b1, f1, w2, b2, f2, w3, b3, f3, w4):
    ha = L // NB
    rc = SUBLANES * ha
    zf, zb = _filter_tables(L)
    dl = jnp.abs(jnp.linspace(MIN_DECAY, MAX_DECAY, D_HY, dtype=f32))[None, :]
    w1p = jnp.pad(w1, ((0, LANES - w1.shape[0]), (0, 0)))
    row = lambda v: v.reshape(1, -1)
    const = lambda j: (0, 0)
    full = lambda arr: pl.BlockSpec(arr.shape, const)
    args = [zf, zb, dl, w1p, row(b1), row(f1), w2, row(b2), row(f2), w3, row(b3), row(f3), w4, w4]
    specs = [full(a) for a in args]
    specs[0] = specs[1] = pl.BlockSpec((rc, LANES), lambda j: (j, 0))
    specs[12] = pl.BlockSpec((w4.shape[0], D_HY), lambda j: (0, 0))
    specs[13] = pl.BlockSpec((w4.shape[0], D_HY), lambda j: (0, 1))
    return pl.pallas_call(
        functools.partial(_filt_kernel, ha=ha),
        grid=(NB // SUBLANES,),
        in_specs=specs,
        out_specs=[pl.BlockSpec((None, SUBLANES, 2 * ha, D_HY), lambda j: (0, j, 0, 0)),
                   pl.BlockSpec((1, D_HY), const)],
        out_shape=[jax.ShapeDtypeStruct((1, NB, 2 * ha, D_HY), f32),
                   jax.ShapeDtypeStruct((1, D_HY), f32)],
        compiler_params=_params("arbitrary"),
        name="filt",
    )(*args)


def _cos_sin(idx, n):
    ang = (2.0 * math.pi / n) * (idx % n).astype(f32)
    return jnp.cos(ang), jnp.sin(ang)


def _dft_matrices(L):
    n = 2 * L
    na = n // NB
    ha = na // 2
    b = jnp.arange(NB, dtype=jnp.int32)[:, None, None]
    p = jnp.arange(na, dtype=jnp.int32)[None, :, None]
    a = jnp.arange(na, dtype=jnp.int32)[None, None, :]
    c, s = _cos_sin(p * (NB * a + b), n)
    ch, sh = c[:, :, :ha], s[:, :, :ha]
    m_data = jnp.concatenate([jnp.concatenate([ch, sh], axis=2),
                              jnp.concatenate([-sh, ch], axis=2)], axis=1)
    m_real = jnp.concatenate([c, -s], axis=1)
    cht, sht = jnp.swapaxes(ch, 1, 2), jnp.swapaxes(sh, 1, 2)
    m_inv = jnp.concatenate([jnp.concatenate([cht, -sht], axis=2),
                             jnp.concatenate([sht, cht], axis=2)], axis=1)
    q = jnp.arange(NB, dtype=jnp.int32)
    gc, gs = _cos_sin(q[:, None] * q[None, :], NB)
    g_fwd = jnp.concatenate([jnp.concatenate([gc, gs], axis=1),
                             jnp.concatenate([-gs, gc], axis=1)], axis=0)
    g_inv = jnp.concatenate([jnp.concatenate([gc, -gs], axis=1),
                             jnp.concatenate([gs, gc], axis=1)], axis=0)
    return (m_data.astype(bf16), m_real.astype(bf16), m_inv.astype(bf16),
            g_fwd.astype(bf16), g_inv.astype(bf16))


def _ffta_kernel(x_ref, m_ref, o_ref, *, bt, na):
    for j in range(bt):
        res = jnp.dot(m_ref[j], x_ref[j].astype(bf16), preferred_element_type=f32)
        o_ref[0, j] = res[:na]
        o_ref[1, j] = res[na:]


def _fft_a(x, m, bt=SUBLANES):
    P, _, na, C = x.shape
    return pl.pallas_call(
        functools.partial(_ffta_kernel, bt=bt, na=na),
        grid=(P, NB // bt),
        in_specs=[pl.BlockSpec((None, bt, na, C), lambda r, j: (r, j, 0, 0)),
                  pl.BlockSpec((bt, 2 * na, na), lambda r, j: (j, 0, 0))],
        out_specs=pl.BlockSpec((None, 2, bt, na, C), lambda r, j: (r, 0, j, 0, 0)),
        out_shape=jax.ShapeDtypeStruct((P, 2, NB, na, C), f32),
        compiler_params=_params("parallel", "arbitrary"),
        name="fft_a",
    )(x, m)


def _fftb_kernel(y_ref, k_ref, gf_ref, gi_ref, o_ref, *, pt, scale):
    spectrum_mode = gi_ref is None
    if spectrum_mode:
        inv = scale / (k_ref[...] + EPS)
    for j in range(pt):
        rhs = jnp.concatenate([y_ref[0, :, j, :], y_ref[1, :, j, :]], axis=0).astype(bf16)
        z = jnp.dot(gf_ref[...], rhs, preferred_element_type=f32)
        zr, zi = z[:NB], z[NB:]
        if spectrum_mode:
            o_ref[0, j] = zr * inv
            o_ref[1, j] = zi * inv
        else:
            kr, ki = k_ref[0, j], k_ref[1, j]
            fr = zr * kr - zi * ki
            fi = zr * ki + zi * kr
            v = jnp.dot(gi_ref[...], jnp.concatenate([fr, fi], axis=0).astype(bf16),
                        preferred_element_type=f32)
            o_ref[0, j] = v[:NB]
            o_ref[1, j] = v[NB:]


def _fft_b(y, kf, g_fwd, g_inv, l1=None, scale=1.0, pt=SUBLANES):
    P, _, _, na, C = y.shape
    const = lambda c, r: (0, 0)
    y_spec = pl.BlockSpec((None, 2, NB, pt, C), lambda c, r: (r, 0, 0, c, 0))
    g_spec = pl.BlockSpec(g_fwd.shape, const)
    out_spec = pl.BlockSpec((None, 2, pt, NB, C), lambda c, r: (r, 0, c, 0, 0))
    if kf is None:
        body = lambda y_ref, l1_ref, gf_ref, o_ref: _fftb_kernel(y_ref, l1_ref, gf_ref, None, o_ref, pt=pt, scale=scale)
        in_specs, args = [y_spec, pl.BlockSpec((1, C), const), g_spec], (y, l1, g_fwd)
    else:
        body = functools.partial(_fftb_kernel, pt=pt, scale=scale)
        k_spec = pl.BlockSpec((2, pt, NB, C), lambda c, r: (0, c, 0, 0))
        in_specs, args = [y_spec, k_spec, g_spec, g_spec], (y, kf, g_fwd, g_inv)
    return pl.pallas_call(
        body,
        grid=(na // pt, P),
        in_specs=in_specs,
        out_specs=out_spec,
        out_shape=jax.ShapeDtypeStruct((P, 2, na, NB, C), f32),
        compiler_params=_params("parallel", "arbitrary"),
        name="fft_b",
    )(*args)


def _fftc_kernel(v_ref, m_ref, x0_ref, u_ref, sk_ref, o_ref, *, bt):
    for j in range(bt):
        rhs = jnp.concatenate([v_ref[0, :, j, :], v_ref[1, :, j, :]], axis=0).astype(bf16)
        conv = jnp.dot(m_ref[j], rhs, preferred_element_type=f32)
        o_ref[j] = x0_ref[j] * (conv + sk_ref[...] * u_ref[j])


def _fft_c(v, m_inv, x0, u, skip, bt=SUBLANES):
    P, _, na, _, C = v.shape
    ba_spec = pl.BlockSpec((None, bt, na, C), lambda r, j: (r, j, 0, 0))
    return pl.pallas_call(
        functools.partial(_fftc_kernel, bt=bt),
        grid=(P, NB // bt),
        in_specs=[pl.BlockSpec((None, 2, na, bt, C), lambda r, j: (r, 0, 0, j, 0)),
                  pl.BlockSpec((bt, na, 2 * na), lambda r, j: (j, 0, 0)),
                  ba_spec, ba_spec,
                  pl.BlockSpec((1, C), lambda r, j: (0, 0))],
        out_specs=ba_spec,
        out_shape=jax.ShapeDtypeStruct((P, NB, na, C), f32),
        compiler_params=_params("parallel", "arbitrary"),
        name="fft_c",
    )(v, m_inv, x0, u, skip)


def _carry_scan(at, bt, reverse):
    n = at.shape[0]
    row = lax.broadcasted_iota(jnp.int32, at.shape, 0)
    s = 1
    while s < n:
        keep = (row < n - s) if reverse else (row >= s)
        shift = n - s if reverse else s
        ash = jnp.where(keep, pltpu.roll(at, shift, 0), 1.0)
        bsh = jnp.where(keep, pltpu.roll(bt, shift, 0), 0.0)
        bt = at * bsh + bt
        at = at * ash
        s *= 2
    if reverse:
        return jnp.where(row < n - 1, pltpu.roll(bt, n - 1, 0), 0.0)
    return jnp.where(row >= 1, pltpu.roll(bt, 1, 0), 0.0)


def _lru_kernel(xb_ref, wg_ref, bg_ref, lam_ref, o_ref, hf_ref, af_ref, hb_ref, ab_ref, *, ha):
    bt = SUBLANES
    nc = NB // bt
    lam = lam_ref[...]
    neg_c_sp = -LRU_C * (jnp.maximum(-lam, 0.0) + jnp.log1p(jnp.exp(-jnp.abs(lam))))
    rowi = lax.broadcasted_iota(jnp.int32, (bt * ha, 1), 0)

    def gates(d, k, first):
        b0 = pl.multiple_of(k * bt, bt)
        xc = xb_ref[pl.ds(b0, bt)].reshape(bt * ha, LANES)
        g = jnp.dot(xc.astype(bf16), wg_ref[d], preferred_element_type=f32) + bg_ref[d:d + 1, :]
        r = _sigmoid(g[:, :LANES])
        ig = _sigmoid(g[:, LANES:])
        log_a = neg_c_sp[d:d + 1, :] * r
        a = jnp.exp(log_a)
        m2 = -jnp.tanh(log_a) * (a * a + 1.0)
        mult = jnp.where(m2 > 0.0, m2 * lax.rsqrt(m2), 0.0)
        if first:
            mult = jnp.where(rowi == (bt * ha - 1 if d else 0), 1.0, mult)
        return b0, a.reshape(bt, ha, LANES), (mult * (ig * xc)).reshape(bt, ha, LANES)

    def step(k, carry, first=False):
        hf, af, hb, ab = carry
        b0, a, bv = gates(0, k, first)
        for j in range(bt):
            hf = a[j] * hf + bv[j]
            af = a[j] * af
            hf_ref[b0 + j] = hf
            af_ref[b0 + j] = af
        b0, a, bv = gates(1, nc - 1 - k, first)
        for j in reversed(range(bt)):
            hb = a[j] * hb + bv[j]
            ab = a[j] * ab
            hb_ref[b0 + j] = hb
            ab_ref[b0 + j] = ab
        return hf, af, hb, ab

    zero = jnp.zeros((ha, LANES), f32)
    one = jnp.ones((ha, LANES), f32)
    carry = step(0, (zero, one, zero, one), first=True)
    hf, af, hb, ab = lax.fori_loop(1, nc, step, carry)
    cf = _carry_scan(af, hf, reverse=False)
    cb = _carry_scan(ab, hb, reverse=True)

    def finish(k, c):
        b0 = pl.multiple_of(k * bt, bt)
        sl = pl.ds(b0, bt)
        o_ref[sl] = (hf_ref[sl] + af_ref[sl] * cf) + (hb_ref[sl] + ab_ref[sl] * cb)
        return c

    lax.fori_loop(0, nc, finish, 0)


def _lru_gate_weights(wa, wx):
    def blockdiag(w):
        w = w.reshape(2, -1, 2, HEAD, HEAD)
        z = jnp.zeros_like(w[:, :, 0])
        top = jnp.concatenate([w[:, :, 0], z], axis=-1)
        bot = jnp.concatenate([z, w[:, :, 1]], axis=-1)
        return jnp.concatenate([top, bot], axis=-2)
    return jnp.concatenate([blockdiag(wa), blockdiag(wx)], axis=-1).astype(bf16)


def _lru(xb, wa, ba, wx, bx, lam):
    P, _, _, ha, C = xb.shape
    nblk = C // LANES
    wg = _lru_gate_weights(wa, wx)
    bg = jnp.concatenate([ba.reshape(2, nblk, 1, LANES), bx.reshape(2, nblk, 1, LANES)], axis=-1)
    ba_spec = pl.BlockSpec((None, NB, None, ha, LANES), lambda b, c: (b // 2, 0, b % 2, 0, c))
    return pl.pallas_call(
        functools.partial(_lru_kernel, ha=ha),
        grid=(2 * P, nblk),
        in_specs=[ba_spec,
                  pl.BlockSpec((2, None, LANES, 2 * LANES), lambda b, c: (0, c, 0, 0)),
                  pl.BlockSpec((2, None, None, 2 * LANES), lambda b, c: (0, c, 0, 0)),
                  pl.BlockSpec((2, LANES), lambda b, c: (0, c))],
        out_specs=ba_spec,
        out_shape=jax.ShapeDtypeStruct(xb.shape, f32),
        scratch_shapes=[pltpu.VMEM((NB, ha, LANES), f32)] * 4,
        compiler_params=_params("parallel", "arbitrary"),
        name="lru",
    )(xb, wg, bg, lam)


def _out_kernel(yh_ref, hg_ref, yl_ref, lg_ref, x_ref, hog_ref, log_ref, wo_ref, fg_ref, o_ref, ys_ref):
    flat = lambda ref: ref[...].reshape(TILE, -1)
    ycat = jnp.concatenate([_rms(flat(yh_ref), hog_ref[...]) * flat(hg_ref),
                            _rms(flat(yl_ref), log_ref[...]) * flat(lg_ref)], axis=-1)
    y = jnp.dot(ycat.astype(bf16), wo_ref[...], preferred_element_type=f32)
    ys_ref[...] = y.reshape(ys_ref.shape)
    for a in range(SUBLANES):
        rows = slice(a * NB, (a + 1) * NB)
        o_ref[rows, :] = _rms(x_ref[rows, :] + ys_ref[:, a, :], fg_ref[...])


def _out(yh, hg, yl, lg, x, hog, log_g, w_out, fg):
    B, L, D = x.shape
    const = lambda b, i: (0, 0)
    nat = pl.BlockSpec((None, TILE, D), lambda b, i: (b, i, 0))
    return pl.pallas_call(
        _out_kernel,
        grid=(B, L // TILE),
        in_specs=[_ba_spec(D_HY), _ba_spec(D_HY), _ba_spec(D_LRU), _ba_spec(D_LRU), nat,
                  pl.BlockSpec((1, D_HY), const), pl.BlockSpec((1, D_LRU), const),
                  pl.BlockSpec(w_out.shape, const), pl.BlockSpec((1, D), const)],
        out_specs=nat,
        out_shape=jax.ShapeDtypeStruct((B, L, D), f32),
        scratch_shapes=[pltpu.VMEM((NB, SUBLANES, D), f32)],
        compiler_params=_params("parallel", "arbitrary"),
        name="out",
    )(yh, hg, yl, lg, x, hog, log_g, w_out, fg)


def kernel(x, norm_g, w_in, hy_conv_w, hy_conv_b, flt_w1, flt_b1, flt_f1, flt_w2, flt_b2, flt_f2,
           flt_w3, flt_b3, flt_f3, flt_w4, hy_skip, lru_conv_w, lru_conv_b, lru_wa, lru_ba, lru_wx,
           lru_bx, lru_lam, hy_out_g, lru_out_g, w_out, final_g):
    B, L, D = x.shape
    assert norm_g.shape[0] == 1, "one layer"
    assert B % 2 == 0 and L % TILE == 0
    ha = L // NB
    na = 2 * ha
    row = lambda v: v.reshape(1, -1)

    u, x0, hg, xb, lg = _inproj(x, row(norm_g[0]), w_in[0].astype(bf16), hy_conv_w[0], row(hy_conv_b[0]),
                                lru_conv_w[0], row(lru_conv_b[0]))
    pair = lambda t: t.reshape(B // 2, NB, na, D_HY)

    m_data, m_real, m_inv, g_fwd, g_inv = _dft_matrices(L)
    kt, l1 = _filter(L, flt_w1[0], flt_b1[0], flt_f1[0], flt_w2[0], flt_b2[0], flt_f2[0],
                     flt_w3[0], flt_b3[0], flt_f3[0], flt_w4[0])
    kf = _fft_b(_fft_a(kt, m_real), None, g_fwd, g_inv, l1=l1, scale=1.0 / (2 * L))[0]

    v = _fft_b(_fft_a(pair(u), m_data), kf, g_fwd, g_inv)
    yh = _fft_c(v, m_inv, pair(x0), pair(u), row(hy_skip[0])).reshape(u.shape)

    yl = _lru(xb, lru_wa[0], lru_ba[0], lru_wx[0], lru_bx[0], lru_lam[0])
    return _out(yh, hg, yl, lg, x, row(hy_out_g[0]), row(lru_out_g[0]), w_out[0].astype(bf16), row(final_g))
```

```python
import functools
import math

import jax
import jax.numpy as jnp
from jax import lax
from jax.experimental import pallas as pl
from jax.experimental.pallas import tpu as pltpu

f32 = jnp.float32
bf16 = jnp.bfloat16

D_HY = 768
D_LRU = 768
HEAD = 64
LANES = 128
SUBLANES = 8
NB = 128
TILE = SUBLANES * NB
HALO = SUBLANES
FFT_CB = 256
FFT_BA = 16
FFT_PB = 8
FILTER_BANDS = 16
FILTER_EMB = 2 * FILTER_BANDS + 1
MASK_COL = FILTER_EMB
FILTER_TARGET = 1e-2
MIN_DECAY = math.log(FILTER_TARGET) / 0.3
MAX_DECAY = math.log(FILTER_TARGET) / 1.5
LRU_C = 8.0
EPS = 1e-6
VMEM_LIMIT = 60 * 1024 * 1024


def _params(*sem):
    return pltpu.CompilerParams(dimension_semantics=sem, vmem_limit_bytes=VMEM_LIMIT)


def _rms(y, g):
    return y * lax.rsqrt(jnp.mean(y * y, axis=-1, keepdims=True) + EPS) * g


def _sigmoid(x):
    return 0.5 * jnp.tanh(0.5 * x) + 0.5


def _inproj_kernel(x_ref, xp_ref, xn_ref, g_ref, w_ref, hcw_ref, hcb_ref, lcw_ref, lcb_ref,
                   u_ref, x0_ref, hg_ref, xb_ref, lg_ref, xs_ref, *, n_tiles):
    i = pl.program_id(1)
    g = g_ref[...]
    for a in range(SUBLANES):
        xs_ref[0:NB, a, :] = _rms(x_ref[a * NB:(a + 1) * NB, :], g)
    xs_ref[NB] = jnp.where(i > 0, _rms(xp_ref[...], g), 0.0)
    xs_ref[NB + 1] = jnp.where(i < n_tiles - 1, _rms(xn_ref[...], g), 0.0)
    xn = xs_ref[...].reshape(TILE + 2 * HALO, -1).astype(bf16)
    sub = lax.broadcasted_iota(jnp.int32, (SUBLANES, D_HY), 0)

    def proj(c0):
        p = jnp.dot(xn, w_ref[:, c0:c0 + D_HY], preferred_element_type=f32)
        return p.reshape(NB + 2, SUBLANES, D_HY)

    def edge(p3, s):
        if s < 0:
            return jnp.where(sub == 0, p3[NB][SUBLANES + s:SUBLANES + s + 1],
                             pltpu.roll(p3[NB + s], 1, 0))
        return jnp.where(sub == SUBLANES - 1, p3[NB + 1][s - NB:s - NB + 1],
                         pltpu.roll(p3[s - NB], SUBLANES - 1, 0))

    def conv(p3, cw_ref, cb_ref, c0, offsets):
        lo, hi = max(0, -min(offsets)), NB - max(offsets)

        def acc(get):
            y = cb_ref[:, c0:c0 + D_HY]
            for k, o in enumerate(offsets):
                y = y + get(o) * cw_ref[k:k + 1, c0:c0 + D_HY]
            return y

        inner = acc(lambda o: p3[lo + o:hi + o])
        edges = {b: acc(lambda o, b=b: p3[b + o] if 0 <= b + o < NB else edge(p3, b + o))
                 for b in list(range(lo)) + list(range(hi, NB))}
        return lo, hi, inner, edges

    def store(ref, conv_out, other=None):
        lo, hi, inner, edges = conv_out
        if other is not None:
            inner = inner * other[2]
            edges = {b: edges[b] * other[3][b] for b in edges}
        ref[lo:hi] = inner
        for b, y in edges.items():
            ref[b] = y

    hy = (-1, 0, 1)
    store(u_ref, conv(proj(0), hcw_ref, hcb_ref, 0, hy), conv(proj(2 * D_HY), hcw_ref, hcb_ref, 2 * D_HY, hy))
    store(x0_ref, conv(proj(D_HY), hcw_ref, hcb_ref, D_HY, hy))
    store(xb_ref, conv(proj(4 * D_HY), lcw_ref, lcb_ref, 0, (-1, 0, 1, 2)))
    xm = xn[:TILE]
    hg = jnp.dot(xm, w_ref[:, 3 * D_HY:4 * D_HY], preferred_element_type=f32)
    hg_ref[...] = (hg * _sigmoid(hg)).astype(bf16)
    lg = jnp.dot(xm, w_ref[:, 4 * D_HY + D_LRU:], preferred_element_type=f32)
    lg_ref[...] = (lg * _sigmoid(lg)).astype(bf16)


def _ba_spec(c):
    return pl.BlockSpec((None, NB, None, SUBLANES, c), lambda b, i: (b // 2, 0, b % 2, i, 0))


def _gate_spec(c):
    return pl.BlockSpec((None, None, TILE, c), lambda b, i: (b, i, 0, 0))


def _inproj(x, norm_g, w_in, hcw, hcb, lcw, lcb):
    B, L, D = x.shape
    n_tiles = L // TILE
    ha = L // NB
    hb = TILE // HALO
    n_hb = L // HALO
    const = lambda b, i: (0, 0)
    ba_shape = jax.ShapeDtypeStruct((B // 2, NB, 2, ha, D_HY), f32)
    gate_shape = jax.ShapeDtypeStruct((B, n_tiles, TILE, D_HY), bf16)
    return pl.pallas_call(
        functools.partial(_inproj_kernel, n_tiles=n_tiles),
        grid=(B, n_tiles),
        in_specs=[
            pl.BlockSpec((None, TILE, D), lambda b, i: (b, i, 0)),
            pl.BlockSpec((None, HALO, D), lambda b, i: (b, jnp.maximum(i * hb - 1, 0), 0)),
            pl.BlockSpec((None, HALO, D), lambda b, i: (b, jnp.minimum((i + 1) * hb, n_hb - 1), 0)),
            pl.BlockSpec((1, D), const),
            pl.BlockSpec(w_in.shape, const, pipeline_mode=pl.Buffered(1)),
            pl.BlockSpec(hcw.shape, const),
            pl.BlockSpec(hcb.shape, const),
            pl.BlockSpec(lcw.shape, const),
            pl.BlockSpec(lcb.shape, const),
        ],
        out_specs=[_ba_spec(D_HY), _ba_spec(D_HY), _gate_spec(D_HY), _ba_spec(D_LRU), _gate_spec(D_LRU)],
        out_shape=[ba_shape, ba_shape, gate_shape, ba_shape, gate_shape],
        scratch_shapes=[pltpu.VMEM((NB + 2, SUBLANES, D), f32)],
        compiler_params=_params("parallel", "arbitrary"),
        name="inproj",
    )(x, x, x, norm_g, w_in, hcw, hcb, lcw, lcb)


def _filt_kernel(zf_ref, zb_ref, dl_ref, w1_ref, b1_ref, f1_ref, w2_ref, b2_ref, f2_ref,
                 w3_ref, b3_ref, f3_ref, w4f_ref, w4b_ref, o_ref, s_ref, *, ha):
    hi = lax.Precision.HIGHEST
    dl = dl_ref[...]

    def half(z, w4_ref):
        h = jnp.sin(f1_ref[...] * (jnp.dot(z, w1_ref[...], precision=hi, preferred_element_type=f32) + b1_ref[...]))
        h = jnp.sin(f2_ref[...] * (jnp.dot(h, w2_ref[...], precision=hi, preferred_element_type=f32) + b2_ref[...]))
        h = jnp.sin(f3_ref[...] * (jnp.dot(h, w3_ref[...], precision=hi, preferred_element_type=f32) + b3_ref[...]))
        h = jnp.dot(h, w4_ref[...], precision=hi, preferred_element_type=f32)
        return h * jnp.exp(-z[:, 0:1] * dl)

    hf = half(zf_ref[...], w4f_ref)
    zb = zb_ref[...]
    hb = half(zb, w4b_ref) * zb[:, MASK_COL:MASK_COL + 1]
    o_ref[:, :ha, :] = hf.reshape(SUBLANES, ha, D_HY)
    o_ref[:, ha:, :] = hb.reshape(SUBLANES, ha, D_HY)

    @pl.when(pl.program_id(0) == 0)
    def _():
        s_ref[...] = jnp.zeros_like(s_ref)

    s_ref[...] += jnp.sum(jnp.abs(hf), axis=0, keepdims=True) + jnp.sum(jnp.abs(hb), axis=0, keepdims=True)


def _filter_tables(L):
    ha = L // NB
    t = jnp.linspace(0.0, 1.0, L, dtype=f32)[:, None]
    w = (2.0 * math.pi / L) * jnp.arange(L, dtype=f32)[:, None]
    f = jnp.linspace(1e-4, FILTER_BANDS - 1, FILTER_BANDS, dtype=f32)[None, :]
    z = jnp.concatenate([t, jnp.cos(w * f), -jnp.sin(w * f)], axis=-1)
    z = jnp.pad(z, ((0, 0), (0, LANES - z.shape[1])))
    b = jnp.arange(NB, dtype=jnp.int32)[:, None]
    a = jnp.arange(ha, dtype=jnp.int32)[None, :]
    lag_f = (NB * a + b).reshape(-1)
    lag_b = (L - NB * a - b).reshape(-1)
    zb = z[jnp.where(lag_b < L, lag_b, 0)].at[:, MASK_COL].set((lag_b < L).astype(f32))
    return z[lag_f], zb


def _filter(L, w1, b1, f1, w2, b2, f2, w3, b3, f3, w4):
    ha = L // NB
    rc = SUBLANES * ha
    zf, zb = _filter_tables(L)
    dl = jnp.abs(jnp.linspace(MIN_DECAY, MAX_DECAY, D_HY, dtype=f32))[None, :]
    w1p = jnp.pad(w1, ((0, LANES - w1.shape[0]), (0, 0)))
    row = lambda v: v.reshape(1, -1)
    const = lambda j: (0, 0)
    full = lambda arr: pl.BlockSpec(arr.shape, const)
    args = [zf, zb, dl, w1p, row(b1), row(f1), w2, row(b2), row(f2), w3, row(b3), row(f3), w4, w4]
    specs = [full(a) for a in args]
    specs[0] = specs[1] = pl.BlockSpec((rc, LANES), lambda j: (j, 0))
    specs[12] = pl.BlockSpec((w4.shape[0], D_HY), lambda j: (0, 0))
    specs[13] = pl.BlockSpec((w4.shape[0], D_HY), lambda j: (0, 1))
    return pl.pallas_call(
        functools.partial(_filt_kernel, ha=ha),
        grid=(NB // SUBLANES,),
        in_specs=specs,
        out_specs=[pl.BlockSpec((None, SUBLANES, 2 * ha, D_HY), lambda j: (0, j, 0, 0)),
                   pl.BlockSpec((1, D_HY), const)],
        out_shape=[jax.ShapeDtypeStruct((1, NB, 2 * ha, D_HY), f32),
                   jax.ShapeDtypeStruct((1, D_HY), f32)],
        compiler_params=_params("arbitrary"),
        name="filt",
    )(*args)


def _cos_sin(idx, n):
    ang = (2.0 * math.pi / n) * (idx % n).astype(f32)
    return jnp.cos(ang), jnp.sin(ang)


def _dft_matrices(L):
    n = 2 * L
    na = n // NB
    ha = na // 2
    b = jnp.arange(NB, dtype=jnp.int32)[:, None, None]
    p = jnp.arange(na, dtype=jnp.int32)[None, :, None]
    a = jnp.arange(na, dtype=jnp.int32)[None, None, :]
    c, s = _cos_sin(p * (NB * a + b), n)
    ch, sh = c[:, :, :ha], s[:, :, :ha]
    m_data = jnp.concatenate([jnp.concatenate([ch, sh], axis=2),
                              jnp.concatenate([-sh, ch], axis=2)], axis=1)
    m_real = jnp.concatenate([c, -s], axis=1)
    cht, sht = jnp.swapaxes(ch, 1, 2), jnp.swapaxes(sh, 1, 2)
    m_inv = jnp.concatenate([jnp.concatenate([cht, -sht], axis=2),
                             jnp.concatenate([sht, cht], axis=2)], axis=1)
    q = jnp.arange(NB, dtype=jnp.int32)
    gc, gs = _cos_sin(q[:, None] * q[None, :], NB)
    g_fwd = jnp.concatenate([jnp.concatenate([gc, gs], axis=1),
                             jnp.concatenate([-gs, gc], axis=1)], axis=0)
    g_inv = jnp.concatenate([jnp.concatenate([gc, -gs], axis=1),
                             jnp.concatenate([gs, gc], axis=1)], axis=0)
    return (m_data.astype(bf16), m_real.astype(bf16), m_inv.astype(bf16),
            g_fwd.astype(bf16), g_inv.astype(bf16))


def _ffta_kernel(x_ref, m_ref, o_ref, *, bt, na):
    for j in range(bt):
        res = jnp.dot(m_ref[j], x_ref[j].astype(bf16), preferred_element_type=f32)
        o_ref[0, j] = res[:na]
        o_ref[1, j] = res[na:]


def _fft_a(x, m, bt=SUBLANES):
    P, _, na, C = x.shape
    return pl.pallas_call(
        functools.partial(_ffta_kernel, bt=bt, na=na),
        grid=(P, NB // bt),
        in_specs=[pl.BlockSpec((None, bt, na, C), lambda r, j: (r, j, 0, 0)),
                  pl.BlockSpec((bt, 2 * na, na), lambda r, j: (j, 0, 0))],
        out_specs=pl.BlockSpec((None, 2, bt, na, C), lambda r, j: (r, 0, j, 0, 0)),
        out_shape=jax.ShapeDtypeStruct((P, 2, NB, na, C), f32),
        compiler_params=_params("parallel", "arbitrary"),
        name="fft_a",
    )(x, m)


def _fftb_kernel(y_ref, l1_ref, gf_ref, o_ref, *, pt, scale):
    inv = scale / (l1_ref[...] + EPS)
    for j in range(pt):
        rhs = jnp.concatenate([y_ref[0, :, j, :], y_ref[1, :, j, :]], axis=0).astype(bf16)
        z = jnp.dot(gf_ref[...], rhs, preferred_element_type=f32)
        o_ref[0, j] = z[:NB] * inv
        o_ref[1, j] = z[NB:] * inv


def _fft_b(y, l1, g_fwd, scale, pt=SUBLANES):
    _, _, _, na, C = y.shape
    const = lambda c: (0, 0)
    return pl.pallas_call(
        functools.partial(_fftb_kernel, pt=pt, scale=scale),
        grid=(na // pt,),
        in_specs=[pl.BlockSpec((None, 2, NB, pt, C), lambda c: (0, 0, 0, c, 0)),
                  pl.BlockSpec((1, C), const),
                  pl.BlockSpec(g_fwd.shape, const)],
        out_specs=pl.BlockSpec((2, pt, NB, C), lambda c: (0, c, 0, 0)),
        out_shape=jax.ShapeDtypeStruct((2, na, NB, C), f32),
        compiler_params=_params("arbitrary"),
        name="fft_b",
    )(y, l1, g_fwd)


def _fftconv_kernel(ua_ref, ma_ref, k_ref, gf_ref, gi_ref, mc_ref, x0_ref, uc_ref, sk_ref, o_ref, s_ref,
                    *, na, a_steps, b_steps):
    s = pl.program_id(2)
    groups = NB // SUBLANES

    @pl.when(s < a_steps)
    def _():
        for j in range(FFT_BA):
            res = jnp.dot(ma_ref[j], ua_ref[j].astype(bf16), preferred_element_type=f32)
            grp = s * (FFT_BA // SUBLANES) + j // SUBLANES
            s_ref[0, :, grp, j % SUBLANES, :] = res[:na]
            s_ref[1, :, grp, j % SUBLANES, :] = res[na:]

    @pl.when((s >= a_steps) & (s < a_steps + b_steps))
    def _():
        for j in range(FFT_PB):
            p = (s - a_steps) * FFT_PB + j
            y = jnp.concatenate([s_ref[0, p, 0:groups].reshape(NB, -1),
                                 s_ref[1, p, 0:groups].reshape(NB, -1)], axis=0)
            z = jnp.dot(gf_ref[...], y.astype(bf16), preferred_element_type=f32)
            zr, zi = z[:NB], z[NB:]
            kr, ki = k_ref[0, j], k_ref[1, j]
            f = jnp.concatenate([zr * kr - zi * ki, zr * ki + zi * kr], axis=0)
            v = jnp.dot(gi_ref[...], f.astype(bf16), preferred_element_type=f32)
            s_ref[0, p, 0:groups] = v[:NB].reshape(groups, SUBLANES, -1)
            s_ref[1, p, 0:groups] = v[NB:].reshape(groups, SUBLANES, -1)

    @pl.when(s >= a_steps + b_steps)
    def _():
        grp = s - (a_steps + b_steps)
        for j in range(SUBLANES):
            v = jnp.concatenate([s_ref[0, :, grp, j, :], s_ref[1, :, grp, j, :]], axis=0)
            conv = jnp.dot(mc_ref[j], v.astype(bf16), preferred_element_type=f32)
            o_ref[j] = x0_ref[j] * (conv + sk_ref[...] * uc_ref[j])


def _fft_conv(u, x0, kf, skip, m_data, m_inv, g_fwd, g_inv):
    P, _, na, C = u.shape
    a_steps, b_steps, c_steps = NB // FFT_BA, na // FFT_PB, NB // SUBLANES
    last_a = a_steps - 1
    b_idx = lambda s: jnp.clip(s - a_steps, 0, b_steps - 1)
    c_idx = lambda s: jnp.clip(s - (a_steps + b_steps), 0, c_steps - 1)
    const = lambda c, r, s: (0, 0)
    c_spec = pl.BlockSpec((None, SUBLANES, na, FFT_CB), lambda c, r, s: (r, c_idx(s), 0, c))
    return pl.pallas_call(
        functools.partial(_fftconv_kernel, na=na, a_steps=a_steps, b_steps=b_steps),
        grid=(C // FFT_CB, P, a_steps + b_steps + c_steps),
        in_specs=[pl.BlockSpec((None, FFT_BA, na, FFT_CB), lambda c, r, s: (r, jnp.minimum(s, last_a), 0, c)),
                  pl.BlockSpec((FFT_BA, 2 * na, na), lambda c, r, s: (jnp.minimum(s, last_a), 0, 0)),
                  pl.BlockSpec((2, FFT_PB, NB, FFT_CB), lambda c, r, s: (0, b_idx(s), 0, c)),
                  pl.BlockSpec(g_fwd.shape, const),
                  pl.BlockSpec(g_inv.shape, const),
                  pl.BlockSpec((SUBLANES, na, 2 * na), lambda c, r, s: (c_idx(s), 0, 0)),
                  c_spec, c_spec,
                  pl.BlockSpec((1, FFT_CB), lambda c, r, s: (0, c))],
        out_specs=c_spec,
        out_shape=jax.ShapeDtypeStruct(u.shape, f32),
        scratch_shapes=[pltpu.VMEM((2, na, NB // SUBLANES + 1, SUBLANES, FFT_CB), f32)],
        compiler_params=_params("arbitrary", "arbitrary", "arbitrary"),
        name="fft_conv",
    )(u, m_data, kf, g_fwd, g_inv, m_inv, x0, u, skip)


def _carry_scan(at, bt, reverse):
    n = at.shape[0]
    row = lax.broadcasted_iota(jnp.int32, at.shape, 0)
    s = 1
    while s < n:
        keep = (row < n - s) if reverse else (row >= s)
        shift = n - s if reverse else s
        ash = jnp.where(keep, pltpu.roll(at, shift, 0), 1.0)
        bsh = jnp.where(keep, pltpu.roll(bt, shift, 0), 0.0)
        bt = at * bsh + bt
        at = at * ash
        s *= 2
    if reverse:
        return jnp.where(row < n - 1, pltpu.roll(bt, n - 1, 0), 0.0)
    return jnp.where(row >= 1, pltpu.roll(bt, 1, 0), 0.0)


def _lru_kernel(xb_ref, wg_ref, bg_ref, lam_ref, o_ref, hf_ref, af_ref, hb_ref, ab_ref, *, ha):
    bt = SUBLANES
    nc = NB // bt
    lam = lam_ref[...]
    half_c = (-0.5 * LRU_C) * (jnp.maximum(-lam, 0.0) + jnp.log1p(jnp.exp(-jnp.abs(lam))))
    rowi = lax.broadcasted_iota(jnp.int32, (bt * ha, 1), 0)

    def gates(d, k, first):
        b0 = pl.multiple_of(k * bt, bt)
        xc = xb_ref[pl.ds(b0, bt)].reshape(bt * ha, LANES)
        t = jnp.tanh(jnp.dot(xc.astype(bf16), wg_ref[d], preferred_element_type=f32) + bg_ref[d:d + 1, :])
        hc = half_c[d:d + 1, :]
        log_a = hc * t[:, :LANES] + hc
        a = jnp.exp(log_a)
        m2 = jnp.tanh(log_a) * (-1.0 - a * a)
        mult = jnp.where(m2 > 0.0, m2 * lax.rsqrt(m2), 0.0)
        if first:
            mult = jnp.where(rowi == (bt * ha - 1 if d else 0), 1.0, mult)
        hx = 0.5 * xc
        bv = mult * (t[:, LANES:] * hx + hx)
        return b0, a.reshape(bt, ha, LANES), bv.reshape(bt, ha, LANES)

    def step(k, carry, first=False):
        hf, af, hb, ab = carry
        b0, a, bv = gates(0, k, first)
        for j in range(bt):
            hf = a[j] * hf + bv[j]
            af = a[j] * af
            hf_ref[b0 + j] = hf
            af_ref[b0 + j] = af
        b0, a, bv = gates(1, nc - 1 - k, first)
        for j in reversed(range(bt)):
            hb = a[j] * hb + bv[j]
            ab = a[j] * ab
            hb_ref[b0 + j] = hb
            ab_ref[b0 + j] = ab
        return hf, af, hb, ab

    zero = jnp.zeros((ha, LANES), f32)
    one = jnp.ones((ha, LANES), f32)
    carry = step(0, (zero, one, zero, one), first=True)
    hf, af, hb, ab = lax.fori_loop(1, nc, step, carry)
    cf = _carry_scan(af, hf, reverse=False)
    cb = _carry_scan(ab, hb, reverse=True)

    def finish(k, c):
        b0 = pl.multiple_of(k * bt, bt)
        sl = pl.ds(b0, bt)
        o_ref[sl] = (hf_ref[sl] + af_ref[sl] * cf) + (hb_ref[sl] + ab_ref[sl] * cb)
        return c

    lax.fori_loop(0, nc, finish, 0)


def _lru_gate_weights(wa, wx):
    def blockdiag(w):
        w = w.reshape(2, -1, 2, HEAD, HEAD)
        z = jnp.zeros_like(w[:, :, 0])
        top = jnp.concatenate([w[:, :, 0], z], axis=-1)
        bot = jnp.concatenate([z, w[:, :, 1]], axis=-1)
        return jnp.concatenate([top, bot], axis=-2)
    return jnp.concatenate([blockdiag(wa), blockdiag(wx)], axis=-1).astype(bf16)


def _lru(xb, wa, ba, wx, bx, lam):
    P, _, _, ha, C = xb.shape
    nblk = C // LANES
    wg = _lru_gate_weights(0.5 * wa, 0.5 * wx)
    bg = 0.5 * jnp.concatenate([ba.reshape(2, nblk, 1, LANES), bx.reshape(2, nblk, 1, LANES)], axis=-1)
    ba_spec = pl.BlockSpec((None, NB, None, ha, LANES), lambda b, c: (b // 2, 0, b % 2, 0, c))
    return pl.pallas_call(
        functools.partial(_lru_kernel, ha=ha),
        grid=(2 * P, nblk),
        in_specs=[ba_spec,
                  pl.BlockSpec((2, None, LANES, 2 * LANES), lambda b, c: (0, c, 0, 0)),
                  pl.BlockSpec((2, None, None, 2 * LANES), lambda b, c: (0, c, 0, 0)),
                  pl.BlockSpec((2, LANES), lambda b, c: (0, c))],
        out_specs=ba_spec,
        out_shape=jax.ShapeDtypeStruct(xb.shape, f32),
        scratch_shapes=[pltpu.VMEM((NB, ha, LANES), f32)] * 4,
        compiler_params=_params("parallel", "arbitrary"),
        name="lru",
    )(xb, wg, bg, lam)


def _out_kernel(yh_ref, hg_ref, yl_ref, lg_ref, x_ref, hog_ref, log_ref, wo_ref, fg_ref, o_ref, ys_ref):
    flat = lambda ref: ref[...].reshape(TILE, -1)
    ycat = jnp.concatenate([_rms(flat(yh_ref), hog_ref[...]) * hg_ref[...],
                            _rms(flat(yl_ref), log_ref[...]) * lg_ref[...]], axis=-1)
    y = jnp.dot(ycat.astype(bf16), wo_ref[...], preferred_element_type=f32)
    ys_ref[...] = y.reshape(ys_ref.shape)
    for a in range(SUBLANES):
        rows = slice(a * NB, (a + 1) * NB)
        o_ref[rows, :] = _rms(x_ref[rows, :] + ys_ref[:, a, :], fg_ref[...])


def _out(yh, hg, yl, lg, x, hog, log_g, w_out, fg):
    B, L, D = x.shape
    const = lambda b, i: (0, 0)
    nat = pl.BlockSpec((None, TILE, D), lambda b, i: (b, i, 0))
    return pl.pallas_call(
        _out_kernel,
        grid=(B, L // TILE),
        in_specs=[_ba_spec(D_HY), _gate_spec(D_HY), _ba_spec(D_LRU), _gate_spec(D_LRU), nat,
                  pl.BlockSpec((1, D_HY), const), pl.BlockSpec((1, D_LRU), const),
                  pl.BlockSpec(w_out.shape, const), pl.BlockSpec((1, D), const)],
        out_specs=nat,
        out_shape=jax.ShapeDtypeStruct((B, L, D), f32),
        scratch_shapes=[pltpu.VMEM((NB, SUBLANES, D), f32)],
        compiler_params=_params("parallel", "arbitrary"),
        name="out",
    )(yh, hg, yl, lg, x, hog, log_g, w_out, fg)


def kernel(x, norm_g, w_in, hy_conv_w, hy_conv_b, flt_w1, flt_b1, flt_f1, flt_w2, flt_b2, flt_f2,
           flt_w3, flt_b3, flt_f3, flt_w4, hy_skip, lru_conv_w, lru_conv_b, lru_wa, lru_ba, lru_wx,
           lru_bx, lru_lam, hy_out_g, lru_out_g, w_out, final_g):
    B, L, D = x.shape
    assert norm_g.shape[0] == 1, "one layer"
    assert B % 2 == 0 and L % TILE == 0
    ha = L // NB
    na = 2 * ha
    row = lambda v: v.reshape(1, -1)

    u, x0, hg, xb, lg = _inproj(x, row(norm_g[0]), w_in[0].astype(bf16), hy_conv_w[0], row(hy_conv_b[0]),
                                lru_conv_w[0], row(lru_conv_b[0]))
    pair = lambda t: t.reshape(B // 2, NB, na, D_HY)

    m_data, m_real, m_inv, g_fwd, g_inv = _dft_matrices(L)
    kt, l1 = _filter(L, flt_w1[0], flt_b1[0], flt_f1[0], flt_w2[0], flt_b2[0], flt_f2[0],
                     flt_w3[0], flt_b3[0], flt_f3[0], flt_w4[0])
    kf = _fft_b(_fft_a(kt, m_real), l1, g_fwd, scale=1.0 / (2 * L))
    yh = _fft_conv(pair(u), pair(x0), kf, row(hy_skip[0]), m_data, m_inv, g_fwd, g_inv).reshape(u.shape)

    yl = _lru(xb, lru_wa[0], lru_ba[0], lru_wx[0], lru_bx[0], lru_lam[0])
    return _out(yh, hg, yl, lg, x, row(hy_out_g[0]), row(lru_out_g[0]), w_out[0].astype(bf16), row(final_g))
```

```python
import functools
import math

import jax
import jax.numpy as jnp
from jax import lax
from jax.experimental import pallas as pl
from jax.experimental.pallas import tpu as pltpu

f32 = jnp.float32
bf16 = jnp.bfloat16

D_HY = 768
D_LRU = 768
HEAD = 64
LANES = 128
SUBLANES = 8
NB = 128
TILE = SUBLANES * NB
HALO = SUBLANES
FFT_CB = 256
FFT_BA = 16
FFT_PB = 8
FILTER_BANDS = 16
FILTER_EMB = 2 * FILTER_BANDS + 1
MASK_COL = FILTER_EMB
FILTER_TARGET = 1e-2
MIN_DECAY = math.log(FILTER_TARGET) / 0.3
MAX_DECAY = math.log(FILTER_TARGET) / 1.5
LRU_C = 8.0
EPS = 1e-6
VMEM_LIMIT = 60 * 1024 * 1024


def _params(*sem):
    return pltpu.CompilerParams(dimension_semantics=sem, vmem_limit_bytes=VMEM_LIMIT)


def _rms(y, g):
    return y * lax.rsqrt(jnp.mean(y * y, axis=-1, keepdims=True) + EPS) * g


def _sigmoid(x):
    return 0.5 * jnp.tanh(0.5 * x) + 0.5


def _inproj_kernel(x_ref, xp_ref, xn_ref, g_ref, w_ref, hcw_ref, hcb_ref, lcw_ref, lcb_ref,
                   u_ref, x0_ref, hg_ref, xb_ref, lg_ref, xs_ref, *, n_tiles):
    i = pl.program_id(1)
    g = g_ref[...]
    for a in range(SUBLANES):
        xs_ref[0:NB, a, :] = _rms(x_ref[a * NB:(a + 1) * NB, :], g)
    xs_ref[NB] = jnp.where(i > 0, _rms(xp_ref[...], g), 0.0)
    xs_ref[NB + 1] = jnp.where(i < n_tiles - 1, _rms(xn_ref[...], g), 0.0)
    xn = xs_ref[...].reshape(TILE + 2 * HALO, -1).astype(bf16)
    sub = lax.broadcasted_iota(jnp.int32, (SUBLANES, D_HY), 0)

    def proj(c0):
        p = jnp.dot(xn, w_ref[:, c0:c0 + D_HY], preferred_element_type=f32)
        return p.reshape(NB + 2, SUBLANES, D_HY)

    def edge(p3, s):
        if s < 0:
            return jnp.where(sub == 0, p3[NB][SUBLANES + s:SUBLANES + s + 1],
                             pltpu.roll(p3[NB + s], 1, 0))
        return jnp.where(sub == SUBLANES - 1, p3[NB + 1][s - NB:s - NB + 1],
                         pltpu.roll(p3[s - NB], SUBLANES - 1, 0))

    def conv(p3, cw_ref, cb_ref, c0, offsets):
        lo, hi = max(0, -min(offsets)), NB - max(offsets)

        def acc(get):
            y = cb_ref[:, c0:c0 + D_HY]
            for k, o in enumerate(offsets):
                y = y + get(o) * cw_ref[k:k + 1, c0:c0 + D_HY]
            return y

        inner = acc(lambda o: p3[lo + o:hi + o])
        edges = {b: acc(lambda o, b=b: p3[b + o] if 0 <= b + o < NB else edge(p3, b + o))
                 for b in list(range(lo)) + list(range(hi, NB))}
        return lo, hi, inner, edges

    def store(ref, conv_out, other=None):
        lo, hi, inner, edges = conv_out
        if other is not None:
            inner = inner * other[2]
            edges = {b: edges[b] * other[3][b] for b in edges}
        ref[lo:hi] = inner
        for b, y in edges.items():
            ref[b] = y

    hy = (-1, 0, 1)
    store(u_ref, conv(proj(0), hcw_ref, hcb_ref, 0, hy), conv(proj(2 * D_HY), hcw_ref, hcb_ref, 2 * D_HY, hy))
    store(x0_ref, conv(proj(D_HY), hcw_ref, hcb_ref, D_HY, hy))
    store(xb_ref, conv(proj(4 * D_HY), lcw_ref, lcb_ref, 0, (-1, 0, 1, 2)))
    xm = xn[:TILE]
    hg = jnp.dot(xm, w_ref[:, 3 * D_HY:4 * D_HY], preferred_element_type=f32)
    hg_ref[...] = (hg * _sigmoid(hg)).astype(bf16)
    lg = jnp.dot(xm, w_ref[:, 4 * D_HY + D_LRU:], preferred_element_type=f32)
    lg_ref[...] = (lg * _sigmoid(lg)).astype(bf16)


def _ba_spec(c):
    return pl.BlockSpec((None, NB, None, SUBLANES, c), lambda b, i: (b // 2, 0, b % 2, i, 0))


def _gate_spec(c):
    return pl.BlockSpec((None, None, TILE, c), lambda b, i: (b, i, 0, 0))


def _inproj(x, norm_g, w_in, hcw, hcb, lcw, lcb):
    B, L, D = x.shape
    n_tiles = L // TILE
    ha = L // NB
    hb = TILE // HALO
    n_hb = L // HALO
    const = lambda b, i: (0, 0)
    ba_shape = jax.ShapeDtypeStruct((B // 2, NB, 2, ha, D_HY), f32)
    gate_shape = jax.ShapeDtypeStruct((B, n_tiles, TILE, D_HY), bf16)
    return pl.pallas_call(
        functools.partial(_inproj_kernel, n_tiles=n_tiles),
        grid=(B, n_tiles),
        in_specs=[
            pl.BlockSpec((None, TILE, D), lambda b, i: (b, i, 0)),
            pl.BlockSpec((None, HALO, D), lambda b, i: (b, jnp.maximum(i * hb - 1, 0), 0)),
            pl.BlockSpec((None, HALO, D), lambda b, i: (b, jnp.minimum((i + 1) * hb, n_hb - 1), 0)),
            pl.BlockSpec((1, D), const),
            pl.BlockSpec(w_in.shape, const, pipeline_mode=pl.Buffered(1)),
            pl.BlockSpec(hcw.shape, const),
            pl.BlockSpec(hcb.shape, const),
            pl.BlockSpec(lcw.shape, const),
            pl.BlockSpec(lcb.shape, const),
        ],
        out_specs=[_ba_spec(D_HY), _ba_spec(D_HY), _gate_spec(D_HY), _ba_spec(D_LRU), _gate_spec(D_LRU)],
        out_shape=[ba_shape, ba_shape, gate_shape, ba_shape, gate_shape],
        scratch_shapes=[pltpu.VMEM((NB + 2, SUBLANES, D), f32)],
        compiler_params=_params("parallel", "arbitrary"),
        name="inproj",
    )(x, x, x, norm_g, w_in, hcw, hcb, lcw, lcb)


def _filt_kernel(zf_ref, zb_ref, dl_ref, w1_ref, b1_ref, f1_ref, w2_ref, b2_ref, f2_ref,
                 w3_ref, b3_ref, f3_ref, w4f_ref, w4b_ref, o_ref, s_ref, *, ha):
    dot = functools.partial(jnp.dot, precision=lax.Precision.HIGHEST, preferred_element_type=f32)
    dl = dl_ref[...]
    zf = zf_ref[...]
    zb = zb_ref[...]
    h = jnp.concatenate([dot(zf, w1_ref[...]), dot(zb, w1_ref[...])], axis=-1)
    h = jnp.sin(f1_ref[...] * (h + b1_ref[...]))
    h = jnp.sin(f2_ref[...] * (dot(h, w2_ref[...]) + b2_ref[...]))
    h = jnp.sin(f3_ref[...] * (dot(h, w3_ref[...]) + b3_ref[...]))
    nh = h.shape[-1] // 2
    hf = dot(h[:, :nh], w4f_ref[...]) * jnp.exp(-zf[:, 0:1] * dl)
    hb = dot(h[:, nh:], w4b_ref[...]) * jnp.exp(-zb[:, 0:1] * dl) * zb[:, MASK_COL:MASK_COL + 1]
    o_ref[:, :ha, :] = hf.reshape(SUBLANES, ha, D_HY)
    o_ref[:, ha:, :] = hb.reshape(SUBLANES, ha, D_HY)

    @pl.when(pl.program_id(0) == 0)
    def _():
        s_ref[...] = jnp.zeros_like(s_ref)

    s_ref[...] += jnp.sum(jnp.abs(hf), axis=0, keepdims=True) + jnp.sum(jnp.abs(hb), axis=0, keepdims=True)


def _filter_tables(L):
    ha = L // NB
    t = jnp.linspace(0.0, 1.0, L, dtype=f32)[:, None]
    w = (2.0 * math.pi / L) * jnp.arange(L, dtype=f32)[:, None]
    f = jnp.linspace(1e-4, FILTER_BANDS - 1, FILTER_BANDS, dtype=f32)[None, :]
    z = jnp.concatenate([t, jnp.cos(w * f), -jnp.sin(w * f)], axis=-1)
    z = jnp.pad(z, ((0, 0), (0, LANES - z.shape[1])))
    b = jnp.arange(NB, dtype=jnp.int32)[:, None]
    a = jnp.arange(ha, dtype=jnp.int32)[None, :]
    lag_f = (NB * a + b).reshape(-1)
    lag_b = (L - NB * a - b).reshape(-1)
    zb = z[jnp.where(lag_b < L, lag_b, 0)].at[:, MASK_COL].set((lag_b < L).astype(f32))
    return z[lag_f], zb


def _filter(L, w1, b1, f1, w2, b2, f2, w3, b3, f3, w4):
    ha = L // NB
    rc = SUBLANES * ha
    zf, zb = _filter_tables(L)
    dl = jnp.abs(jnp.linspace(MIN_DECAY, MAX_DECAY, D_HY, dtype=f32))[None, :]
    w1p = jnp.pad(w1, ((0, LANES - w1.shape[0]), (0, 0)))
    row = lambda v: jnp.tile(v.reshape(1, -1), (1, 2))
    zero = jnp.zeros_like(w2)
    diag2 = lambda w: jnp.concatenate([jnp.concatenate([w, zero], axis=1),
                                       jnp.concatenate([zero, w], axis=1)], axis=0)
    const = lambda j: (0, 0)
    full = lambda arr: pl.BlockSpec(arr.shape, const)
    args = [zf, zb, dl, w1p, row(b1), row(f1), diag2(w2), row(b2), row(f2), diag2(w3), row(b3), row(f3), w4, w4]
    specs = [full(a) for a in args]
    specs[0] = specs[1] = pl.BlockSpec((rc, LANES), lambda j: (j, 0))
    specs[12] = pl.BlockSpec((w4.shape[0], D_HY), lambda j: (0, 0))
    specs[13] = pl.BlockSpec((w4.shape[0], D_HY), lambda j: (0, 1))
    return pl.pallas_call(
        functools.partial(_filt_kernel, ha=ha),
        grid=(NB // SUBLANES,),
        in_specs=specs,
        out_specs=[pl.BlockSpec((None, SUBLANES, 2 * ha, D_HY), lambda j: (0, j, 0, 0)),
                   pl.BlockSpec((1, D_HY), const)],
        out_shape=[jax.ShapeDtypeStruct((1, NB, 2 * ha, D_HY), f32),
                   jax.ShapeDtypeStruct((1, D_HY), f32)],
        compiler_params=_params("arbitrary"),
        name="filt",
    )(*args)


def _cos_sin(idx, n):
    ang = (2.0 * math.pi / n) * (idx % n).astype(f32)
    return jnp.cos(ang), jnp.sin(ang)


def _dft_matrices(L):
    n = 2 * L
    na = n // NB
    ha = na // 2
    b = jnp.arange(NB, dtype=jnp.int32)[:, None]
    p = jnp.arange(na, dtype=jnp.int32)
    ca, sa = _cos_sin(NB * p[:, None] * p[None, :], n)
    cb, sb = _cos_sin(b * p[None, :], n)
    cb, sb = cb[:, :, None], sb[:, :, None]
    c = ca[None] * cb - sa[None] * sb
    s = sa[None] * cb + ca[None] * sb
    ch, sh = c[:, :, :ha], s[:, :, :ha]
    m_data = jnp.concatenate([jnp.concatenate([ch, sh], axis=2),
                              jnp.concatenate([-sh, ch], axis=2)], axis=1)
    m_real = jnp.concatenate([c, -s], axis=1)
    cht, sht = jnp.swapaxes(ch, 1, 2), jnp.swapaxes(sh, 1, 2)
    m_inv = jnp.concatenate([jnp.concatenate([cht, -sht], axis=2),
                             jnp.concatenate([sht, cht], axis=2)], axis=1)
    q = jnp.arange(NB, dtype=jnp.int32)
    gc, gs = _cos_sin(q[:, None] * q[None, :], NB)
    g_fwd = jnp.concatenate([jnp.concatenate([gc, gs], axis=1),
                             jnp.concatenate([-gs, gc], axis=1)], axis=0)
    g_inv = jnp.concatenate([jnp.concatenate([gc, -gs], axis=1),
                             jnp.concatenate([gs, gc], axis=1)], axis=0)
    return (m_data.astype(bf16), m_real.astype(bf16), m_inv.astype(bf16),
            g_fwd.astype(bf16), g_inv.astype(bf16))


def _first_stage(x_ref, m_ref, s_ref, s, na):
    for j in range(FFT_BA):
        res = jnp.dot(m_ref[j], x_ref[j].astype(bf16), preferred_element_type=f32)
        grp = s * (FFT_BA // SUBLANES) + j // SUBLANES
        s_ref[0, :, grp, j % SUBLANES, :] = res[:na]
        s_ref[1, :, grp, j % SUBLANES, :] = res[na:]


def _second_stage(s_ref, gf_ref, p):
    groups = NB // SUBLANES
    y = jnp.concatenate([s_ref[0, p, 0:groups].reshape(NB, -1),
                         s_ref[1, p, 0:groups].reshape(NB, -1)], axis=0)
    return jnp.dot(gf_ref[...], y.astype(bf16), preferred_element_type=f32)


def _fftk_kernel(k_ref, ma_ref, l1_ref, gf_ref, o_ref, s_ref, *, na, a_steps, scale):
    s = pl.program_id(1)

    @pl.when(s < a_steps)
    def _():
        _first_stage(k_ref, ma_ref, s_ref, s, na)

    @pl.when(s >= a_steps)
    def _():
        inv = scale / (l1_ref[...] + EPS)
        for j in range(FFT_PB):
            z = _second_stage(s_ref, gf_ref, (s - a_steps) * FFT_PB + j)
            o_ref[0, j] = z[:NB] * inv
            o_ref[1, j] = z[NB:] * inv


def _fft_k(kt, l1, m_real, g_fwd, scale):
    _, _, na, C = kt.shape
    a_steps, b_steps = NB // FFT_BA, na // FFT_PB
    last_a = a_steps - 1
    const = lambda c, s: (0, 0)
    return pl.pallas_call(
        functools.partial(_fftk_kernel, na=na, a_steps=a_steps, scale=scale),
        grid=(C // FFT_CB, a_steps + b_steps),
        in_specs=[pl.BlockSpec((None, FFT_BA, na, FFT_CB), lambda c, s: (0, jnp.minimum(s, last_a), 0, c)),
                  pl.BlockSpec((FFT_BA, 2 * na, na), lambda c, s: (jnp.minimum(s, last_a), 0, 0)),
                  pl.BlockSpec((1, FFT_CB), lambda c, s: (0, c)),
                  pl.BlockSpec(g_fwd.shape, const)],
        out_specs=pl.BlockSpec((2, FFT_PB, NB, FFT_CB), lambda c, s: (0, jnp.clip(s - a_steps, 0, b_steps - 1), 0, c)),
        out_shape=jax.ShapeDtypeStruct((2, na, NB, C), f32),
        scratch_shapes=[pltpu.VMEM((2, na, NB // SUBLANES + 1, SUBLANES, FFT_CB), f32)],
        compiler_params=_params("arbitrary", "arbitrary"),
        name="fft_k",
    )(kt, m_real, l1, g_fwd)


def _fftconv_kernel(ua_ref, ma_ref, k_ref, gf_ref, gi_ref, mc_ref, x0_ref, uc_ref, sk_ref, o_ref, s_ref,
                    *, na, a_steps, b_steps):
    s = pl.program_id(2)
    groups = NB // SUBLANES

    @pl.when(s < a_steps)
    def _():
        _first_stage(ua_ref, ma_ref, s_ref, s, na)

    @pl.when((s >= a_steps) & (s < a_steps + b_steps))
    def _():
        for j in range(FFT_PB):
            p = (s - a_steps) * FFT_PB + j
            z = _second_stage(s_ref, gf_ref, p)
            zr, zi = z[:NB], z[NB:]
            kr, ki = k_ref[0, j], k_ref[1, j]
            f = jnp.concatenate([zr * kr - zi * ki, zr * ki + zi * kr], axis=0)
            v = jnp.dot(gi_ref[...], f.astype(bf16), preferred_element_type=f32)
            s_ref[0, p, 0:groups] = v[:NB].reshape(groups, SUBLANES, -1)
            s_ref[1, p, 0:groups] = v[NB:].reshape(groups, SUBLANES, -1)

    @pl.when(s >= a_steps + b_steps)
    def _():
        grp = s - (a_steps + b_steps)
        for j in range(SUBLANES):
            v = jnp.concatenate([s_ref[0, :, grp, j, :], s_ref[1, :, grp, j, :]], axis=0)
            conv = jnp.dot(mc_ref[j], v.astype(bf16), preferred_element_type=f32)
            o_ref[j] = x0_ref[j] * (conv + sk_ref[...] * uc_ref[j])


def _fft_conv(u, x0, kf, skip, m_data, m_inv, g_fwd, g_inv):
    P, _, na, C = u.shape
    a_steps, b_steps, c_steps = NB // FFT_BA, na // FFT_PB, NB // SUBLANES
    last_a = a_steps - 1
    b_idx = lambda s: jnp.clip(s - a_steps, 0, b_steps - 1)
    c_idx = lambda s: jnp.clip(s - (a_steps + b_steps), 0, c_steps - 1)
    const = lambda c, r, s: (0, 0)
    c_spec = pl.BlockSpec((None, SUBLANES, na, FFT_CB), lambda c, r, s: (r, c_idx(s), 0, c))
    return pl.pallas_call(
        functools.partial(_fftconv_kernel, na=na, a_steps=a_steps, b_steps=b_steps),
        grid=(C // FFT_CB, P, a_steps + b_steps + c_steps),
        in_specs=[pl.BlockSpec((None, FFT_BA, na, FFT_CB), lambda c, r, s: (r, jnp.minimum(s, last_a), 0, c)),
                  pl.BlockSpec((FFT_BA, 2 * na, na), lambda c, r, s: (jnp.minimum(s, last_a), 0, 0)),
                  pl.BlockSpec((2, FFT_PB, NB, FFT_CB), lambda c, r, s: (0, b_idx(s), 0, c)),
                  pl.BlockSpec(g_fwd.shape, const),
                  pl.BlockSpec(g_inv.shape, const),
                  pl.BlockSpec((SUBLANES, na, 2 * na), lambda c, r, s: (c_idx(s), 0, 0)),
                  c_spec, c_spec,
                  pl.BlockSpec((1, FFT_CB), lambda c, r, s: (0, c))],
        out_specs=c_spec,
        out_shape=jax.ShapeDtypeStruct(u.shape, f32),
        scratch_shapes=[pltpu.VMEM((2, na, NB // SUBLANES + 1, SUBLANES, FFT_CB), f32)],
        compiler_params=_params("arbitrary", "arbitrary", "arbitrary"),
        name="fft_conv",
    )(u, m_data, kf, g_fwd, g_inv, m_inv, x0, u, skip)


def _carry_scan(at, bt, reverse):
    n = at.shape[0]
    row = lax.broadcasted_iota(jnp.int32, at.shape, 0)
    s = 1
    while s < n:
        keep = (row < n - s) if reverse else (row >= s)
        shift = n - s if reverse else s
        ash = jnp.where(keep, pltpu.roll(at, shift, 0), 1.0)
        bsh = jnp.where(keep, pltpu.roll(bt, shift, 0), 0.0)
        bt = at * bsh + bt
        at = at * ash
        s *= 2
    if reverse:
        return jnp.where(row < n - 1, pltpu.roll(bt, n - 1, 0), 0.0)
    return jnp.where(row >= 1, pltpu.roll(bt, 1, 0), 0.0)


def _lru_kernel(xb_ref, wg_ref, bg_ref, lam_ref, o_ref, hf_ref, af_ref, hb_ref, ab_ref, *, ha):
    bt = SUBLANES
    nc = NB // bt
    lam = lam_ref[...]
    half_c = (-0.5 * LRU_C) * (jnp.maximum(-lam, 0.0) + jnp.log1p(jnp.exp(-jnp.abs(lam))))
    rowi = lax.broadcasted_iota(jnp.int32, (bt * ha, 1), 0)

    def gates(d, k, first):
        b0 = pl.multiple_of(k * bt, bt)
        xc = xb_ref[pl.ds(b0, bt)].reshape(bt * ha, LANES)
        t = jnp.tanh(jnp.dot(xc.astype(bf16), wg_ref[d], preferred_element_type=f32) + bg_ref[d:d + 1, :])
        hc = half_c[d:d + 1, :]
        log_a = hc * t[:, :LANES] + hc
        a = jnp.exp(log_a)
        m2 = jnp.tanh(log_a) * (-1.0 - a * a)
        mult = jnp.where(m2 > 0.0, m2 * lax.rsqrt(m2), 0.0)
        if first:
            mult = jnp.where(rowi == (bt * ha - 1 if d else 0), 1.0, mult)
        hx = 0.5 * xc
        bv = mult * (t[:, LANES:] * hx + hx)
        return b0, a.reshape(bt, ha, LANES), bv.reshape(bt, ha, LANES)

    def step(k, carry, first=False):
        hf, af, hb, ab = carry
        b0, a, bv = gates(0, k, first)
        for j in range(bt):
            hf = a[j] * hf + bv[j]
            af = a[j] * af
            hf_ref[b0 + j] = hf
            af_ref[b0 + j] = af
        b0, a, bv = gates(1, nc - 1 - k, first)
        for j in reversed(range(bt)):
            hb = a[j] * hb + bv[j]
            ab = a[j] * ab
            hb_ref[b0 + j] = hb
            ab_ref[b0 + j] = ab
        return hf, af, hb, ab

    zero = jnp.zeros((ha, LANES), f32)
    one = jnp.ones((ha, LANES), f32)
    carry = step(0, (zero, one, zero, one), first=True)
    hf, af, hb, ab = lax.fori_loop(1, nc, step, carry)
    cf = _carry_scan(af, hf, reverse=False)
    cb = _carry_scan(ab, hb, reverse=True)

    def finish(k, c):
        b0 = pl.multiple_of(k * bt, bt)
        sl = pl.ds(b0, bt)
        o_ref[sl] = (hf_ref[sl] + af_ref[sl] * cf) + (hb_ref[sl] + ab_ref[sl] * cb)
        return c

    lax.fori_loop(0, nc, finish, 0)


def _lru_gate_weights(wa, wx):
    def blockdiag(w):
        w = w.reshape(2, -1, 2, HEAD, HEAD)
        z = jnp.zeros_like(w[:, :, 0])
        top = jnp.concatenate([w[:, :, 0], z], axis=-1)
        bot = jnp.concatenate([z, w[:, :, 1]], axis=-1)
        return jnp.concatenate([top, bot], axis=-2)
    return jnp.concatenate([blockdiag(wa), blockdiag(wx)], axis=-1).astype(bf16)


def _lru(xb, wa, ba, wx, bx, lam):
    P, _, _, ha, C = xb.shape
    nblk = C // LANES
    wg = _lru_gate_weights(0.5 * wa, 0.5 * wx)
    bg = 0.5 * jnp.concatenate([ba.reshape(2, nblk, 1, LANES), bx.reshape(2, nblk, 1, LANES)], axis=-1)
    ba_spec = pl.BlockSpec((None, NB, None, ha, LANES), lambda b, c: (b // 2, 0, b % 2, 0, c))
    return pl.pallas_call(
        functools.partial(_lru_kernel, ha=ha),
        grid=(2 * P, nblk),
        in_specs=[ba_spec,
                  pl.BlockSpec((2, None, LANES, 2 * LANES), lambda b, c: (0, c, 0, 0)),
                  pl.BlockSpec((2, None, None, 2 * LANES), lambda b, c: (0, c, 0, 0)),
                  pl.BlockSpec((2, LANES), lambda b, c: (0, c))],
        out_specs=ba_spec,
        out_shape=jax.ShapeDtypeStruct(xb.shape, f32),
        scratch_shapes=[pltpu.VMEM((NB, ha, LANES), f32)] * 4,
        compiler_params=_params("parallel", "arbitrary"),
        name="lru",
    )(xb, wg, bg, lam)


def _out_kernel(yh_ref, hg_ref, yl_ref, lg_ref, x_ref, hog_ref, log_ref, wo_ref, fg_ref, o_ref, ys_ref):
    flat = lambda ref: ref[...].reshape(TILE, -1)
    ycat = jnp.concatenate([_rms(flat(yh_ref), hog_ref[...]) * hg_ref[...],
                            _rms(flat(yl_ref), log_ref[...]) * lg_ref[...]], axis=-1)
    y = jnp.dot(ycat.astype(bf16), wo_ref[...], preferred_element_type=f32)
    ys_ref[...] = y.reshape(ys_ref.shape)
    for a in range(SUBLANES):
        rows = slice(a * NB, (a + 1) * NB)
        o_ref[rows, :] = _rms(x_ref[rows, :] + ys_ref[:, a, :], fg_ref[...])


def _out(yh, hg, yl, lg, x, hog, log_g, w_out, fg):
    B, L, D = x.shape
    const = lambda b, i: (0, 0)
    nat = pl.BlockSpec((None, TILE, D), lambda b, i: (b, i, 0))
    return pl.pallas_call(
        _out_kernel,
        grid=(B, L // TILE),
        in_specs=[_ba_spec(D_HY), _gate_spec(D_HY), _ba_spec(D_LRU), _gate_spec(D_LRU), nat,
                  pl.BlockSpec((1, D_HY), const), pl.BlockSpec((1, D_LRU), const),
                  pl.BlockSpec(w_out.shape, const), pl.BlockSpec((1, D), const)],
        out_specs=nat,
        out_shape=jax.ShapeDtypeStruct((B, L, D), f32),
        scratch_shapes=[pltpu.VMEM((NB, SUBLANES, D), f32)],
        compiler_params=_params("parallel", "arbitrary"),
        name="out",
    )(yh, hg, yl, lg, x, hog, log_g, w_out, fg)


def kernel(x, norm_g, w_in, hy_conv_w, hy_conv_b, flt_w1, flt_b1, flt_f1, flt_w2, flt_b2, flt_f2,
           flt_w3, flt_b3, flt_f3, flt_w4, hy_skip, lru_conv_w, lru_conv_b, lru_wa, lru_ba, lru_wx,
           lru_bx, lru_lam, hy_out_g, lru_out_g, w_out, final_g):
    B, L, D = x.shape
    assert norm_g.shape[0] == 1, "one layer"
    assert B % 2 == 0 and L % TILE == 0
    ha = L // NB
    na = 2 * ha
    row = lambda v: v.reshape(1, -1)

    u, x0, hg, xb, lg = _inproj(x, row(norm_g[0]), w_in[0].astype(bf16), hy_conv_w[0], row(hy_conv_b[0]),
                                lru_conv_w[0], row(lru_conv_b[0]))
    pair = lambda t: t.reshape(B // 2, NB, na, D_HY)

    m_data, m_real, m_inv, g_fwd, g_inv = _dft_matrices(L)
    kt, l1 = _filter(L, flt_w1[0], flt_b1[0], flt_f1[0], flt_w2[0], flt_b2[0], flt_f2[0],
                     flt_w3[0], flt_b3[0], flt_f3[0], flt_w4[0])
    kf = _fft_k(kt, l1, m_real, g_fwd, scale=1.0 / (2 * L))
    yh = _fft_conv(pair(u), pair(x0), kf, row(hy_skip[0]), m_data, m_inv, g_fwd, g_inv).reshape(u.shape)

    yl = _lru(xb, lru_wa[0], lru_ba[0], lru_wx[0], lru_bx[0], lru_lam[0])
    return _out(yh, hg, yl, lg, x, row(hy_out_g[0]), row(lru_out_g[0]), w_out[0].astype(bf16), row(final_g))
```

```python
import functools
import math

import jax
import jax.numpy as jnp
from jax import lax
from jax.experimental import pallas as pl
from jax.experimental.pallas import tpu as pltpu

f32 = jnp.float32
bf16 = jnp.bfloat16

D_HY = 768
D_LRU = 768
HEAD = 64
LANES = 128
SUBLANES = 8
NB = 128
TILE = SUBLANES * NB
HALO = SUBLANES
FFT_CB = 256
FFT_BA = 16
FFT_PB = 8
FFT_PITCH = NB + SUBLANES
FILTER_BANDS = 16
FILTER_EMB = 2 * FILTER_BANDS + 1
MASK_COL = FILTER_EMB
FILTER_TARGET = 1e-2
MIN_DECAY = math.log(FILTER_TARGET) / 0.3
MAX_DECAY = math.log(FILTER_TARGET) / 1.5
LRU_C = 8.0
EPS = 1e-6
VMEM_LIMIT = 60 * 1024 * 1024


def _params(*sem):
    return pltpu.CompilerParams(dimension_semantics=sem, vmem_limit_bytes=VMEM_LIMIT)


def _rms(y, g):
    return y * lax.rsqrt(jnp.mean(y * y, axis=-1, keepdims=True) + EPS) * g


def _put_cols(ref, rows, val):
    for h in range(ref.shape[0]):
        ref[h, rows, :] = val[:, h * LANES:(h + 1) * LANES]


def _get_cols(ref, rows):
    return jnp.concatenate([ref[h, rows, :] for h in range(ref.shape[0])], axis=-1)


def _sigmoid(x):
    return 0.5 * jnp.tanh(0.5 * x) + 0.5


def _inproj_kernel(x_ref, xp_ref, xn_ref, g_ref, w_ref, hcw_ref, hcb_ref, lcw_ref, lcb_ref,
                   u_ref, x0_ref, hg_ref, xb_ref, lg_ref, xs_ref, *, n_tiles):
    i = pl.program_id(1)
    g = g_ref[...]
    for a in range(SUBLANES):
        _put_cols(xs_ref, pl.ds(a, NB, stride=SUBLANES), _rms(x_ref[a * NB:(a + 1) * NB, :], g))
    _put_cols(xs_ref, pl.ds(TILE, HALO), jnp.where(i > 0, _rms(xp_ref[...], g), 0.0))
    _put_cols(xs_ref, pl.ds(TILE + HALO, HALO), jnp.where(i < n_tiles - 1, _rms(xn_ref[...], g), 0.0))
    xn = _get_cols(xs_ref, pl.ds(0, TILE + 2 * HALO)).astype(bf16)
    sub = lax.broadcasted_iota(jnp.int32, (SUBLANES, D_HY), 0)

    def proj(c0):
        p = jnp.dot(xn, w_ref[:, c0:c0 + D_HY], preferred_element_type=f32)
        return p.reshape(NB + 2, SUBLANES, D_HY)

    def edge(p3, s):
        if s < 0:
            return jnp.where(sub == 0, p3[NB][SUBLANES + s:SUBLANES + s + 1],
                             pltpu.roll(p3[NB + s], 1, 0))
        return jnp.where(sub == SUBLANES - 1, p3[NB + 1][s - NB:s - NB + 1],
                         pltpu.roll(p3[s - NB], SUBLANES - 1, 0))

    def conv(p3, cw_ref, cb_ref, c0, offsets):
        lo, hi = max(0, -min(offsets)), NB - max(offsets)

        def acc(get):
            y = cb_ref[:, c0:c0 + D_HY]
            for k, o in enumerate(offsets):
                y = y + get(o) * cw_ref[k:k + 1, c0:c0 + D_HY]
            return y

        inner = acc(lambda o: p3[lo + o:hi + o])
        edges = {b: acc(lambda o, b=b: p3[b + o] if 0 <= b + o < NB else edge(p3, b + o))
                 for b in list(range(lo)) + list(range(hi, NB))}
        return lo, hi, inner, edges

    def store(ref, conv_out, other=None):
        lo, hi, inner, edges = conv_out
        if other is not None:
            inner = inner * other[2]
            edges = {b: edges[b] * other[3][b] for b in edges}
        ref[lo:hi] = inner
        for b, y in edges.items():
            ref[b] = y

    hy = (-1, 0, 1)
    store(u_ref, conv(proj(0), hcw_ref, hcb_ref, 0, hy), conv(proj(2 * D_HY), hcw_ref, hcb_ref, 2 * D_HY, hy))
    store(x0_ref, conv(proj(D_HY), hcw_ref, hcb_ref, D_HY, hy))
    store(xb_ref, conv(proj(4 * D_HY), lcw_ref, lcb_ref, 0, (-1, 0, 1, 2)))
    xm = xn[:TILE]
    hg = jnp.dot(xm, w_ref[:, 3 * D_HY:4 * D_HY], preferred_element_type=f32)
    hg_ref[...] = (hg * _sigmoid(hg)).astype(bf16)
    lg = jnp.dot(xm, w_ref[:, 4 * D_HY + D_LRU:], preferred_element_type=f32)
    lg_ref[...] = (lg * _sigmoid(lg)).astype(bf16)


def _ba_spec(c):
    return pl.BlockSpec((None, NB, None, SUBLANES, c), lambda b, i: (b // 2, 0, b % 2, i, 0))


def _gate_spec(c):
    return pl.BlockSpec((None, None, TILE, c), lambda b, i: (b, i, 0, 0))


def _inproj(x, norm_g, w_in, hcw, hcb, lcw, lcb):
    B, L, D = x.shape
    n_tiles = L // TILE
    ha = L // NB
    hb = TILE // HALO
    n_hb = L // HALO
    const = lambda b, i: (0, 0)
    ba_shape = jax.ShapeDtypeStruct((B // 2, NB, 2, ha, D_HY), f32)
    gate_shape = jax.ShapeDtypeStruct((B, n_tiles, TILE, D_HY), bf16)
    return pl.pallas_call(
        functools.partial(_inproj_kernel, n_tiles=n_tiles),
        grid=(B, n_tiles),
        in_specs=[
            pl.BlockSpec((None, TILE, D), lambda b, i: (b, i, 0)),
            pl.BlockSpec((None, HALO, D), lambda b, i: (b, jnp.maximum(i * hb - 1, 0), 0)),
            pl.BlockSpec((None, HALO, D), lambda b, i: (b, jnp.minimum((i + 1) * hb, n_hb - 1), 0)),
            pl.BlockSpec((1, D), const),
            pl.BlockSpec(w_in.shape, const, pipeline_mode=pl.Buffered(1)),
            pl.BlockSpec(hcw.shape, const),
            pl.BlockSpec(hcb.shape, const),
            pl.BlockSpec(lcw.shape, const),
            pl.BlockSpec(lcb.shape, const),
        ],
        out_specs=[_ba_spec(D_HY), _ba_spec(D_HY), _gate_spec(D_HY), _ba_spec(D_LRU), _gate_spec(D_LRU)],
        out_shape=[ba_shape, ba_shape, gate_shape, ba_shape, gate_shape],
        scratch_shapes=[pltpu.VMEM((D // LANES, TILE + 2 * HALO, LANES), f32)],
        compiler_params=_params("parallel", "arbitrary"),
        name="inproj",
    )(x, x, x, norm_g, w_in, hcw, hcb, lcw, lcb)


def _filt_kernel(zf_ref, zb_ref, dl_ref, w1_ref, b1_ref, f1_ref, w2_ref, b2_ref, f2_ref,
                 w3_ref, b3_ref, f3_ref, w4f_ref, w4b_ref, o_ref, s_ref, *, ha):
    dot = functools.partial(jnp.dot, precision=lax.Precision.HIGHEST, preferred_element_type=f32)
    dl = dl_ref[...]
    zf = zf_ref[...]
    zb = zb_ref[...]
    h = jnp.concatenate([dot(zf, w1_ref[...]), dot(zb, w1_ref[...])], axis=-1)
    h = jnp.sin(f1_ref[...] * (h + b1_ref[...]))
    h = jnp.sin(f2_ref[...] * (dot(h, w2_ref[...]) + b2_ref[...]))
    h = jnp.sin(f3_ref[...] * (dot(h, w3_ref[...]) + b3_ref[...]))
    nh = h.shape[-1] // 2
    hf = dot(h[:, :nh], w4f_ref[...]) * jnp.exp(-zf[:, 0:1] * dl)
    hb = dot(h[:, nh:], w4b_ref[...]) * jnp.exp(-zb[:, 0:1] * dl) * zb[:, MASK_COL:MASK_COL + 1]
    o_ref[:, :ha, :] = hf.reshape(SUBLANES, ha, D_HY)
    o_ref[:, ha:, :] = hb.reshape(SUBLANES, ha, D_HY)

    @pl.when(pl.program_id(0) == 0)
    def _():
        s_ref[...] = jnp.zeros_like(s_ref)

    s_ref[...] += jnp.sum(jnp.abs(hf), axis=0, keepdims=True) + jnp.sum(jnp.abs(hb), axis=0, keepdims=True)


def _filter_tables(L):
    ha = L // NB
    t = jnp.linspace(0.0, 1.0, L, dtype=f32)[:, None]
    w = (2.0 * math.pi / L) * jnp.arange(L, dtype=f32)[:, None]
    f = jnp.linspace(1e-4, FILTER_BANDS - 1, FILTER_BANDS, dtype=f32)[None, :]
    z = jnp.concatenate([t, jnp.cos(w * f), -jnp.sin(w * f)], axis=-1)
    z = jnp.pad(z, ((0, 0), (0, LANES - z.shape[1])))
    b = jnp.arange(NB, dtype=jnp.int32)[:, None]
    a = jnp.arange(ha, dtype=jnp.int32)[None, :]
    lag_f = (NB * a + b).reshape(-1)
    lag_b = (L - NB * a - b).reshape(-1)
    zb = z[jnp.where(lag_b < L, lag_b, 0)].at[:, MASK_COL].set((lag_b < L).astype(f32))
    return z[lag_f], zb


def _filter(L, w1, b1, f1, w2, b2, f2, w3, b3, f3, w4):
    ha = L // NB
    rc = SUBLANES * ha
    zf, zb = _filter_tables(L)
    dl = jnp.abs(jnp.linspace(MIN_DECAY, MAX_DECAY, D_HY, dtype=f32))[None, :]
    w1p = jnp.pad(w1, ((0, LANES - w1.shape[0]), (0, 0)))
    row = lambda v: jnp.tile(v.reshape(1, -1), (1, 2))
    zero = jnp.zeros_like(w2)
    diag2 = lambda w: jnp.concatenate([jnp.concatenate([w, zero], axis=1),
                                       jnp.concatenate([zero, w], axis=1)], axis=0)
    const = lambda j: (0, 0)
    full = lambda arr: pl.BlockSpec(arr.shape, const)
    args = [zf, zb, dl, w1p, row(b1), row(f1), diag2(w2), row(b2), row(f2), diag2(w3), row(b3), row(f3), w4, w4]
    specs = [full(a) for a in args]
    specs[0] = specs[1] = pl.BlockSpec((rc, LANES), lambda j: (j, 0))
    specs[12] = pl.BlockSpec((w4.shape[0], D_HY), lambda j: (0, 0))
    specs[13] = pl.BlockSpec((w4.shape[0], D_HY), lambda j: (0, 1))
    return pl.pallas_call(
        functools.partial(_filt_kernel, ha=ha),
        grid=(NB // SUBLANES,),
        in_specs=specs,
        out_specs=[pl.BlockSpec((None, SUBLANES, 2 * ha, D_HY), lambda j: (0, j, 0, 0)),
                   pl.BlockSpec((1, D_HY), const)],
        out_shape=[jax.ShapeDtypeStruct((1, NB, 2 * ha, D_HY), f32),
                   jax.ShapeDtypeStruct((1, D_HY), f32)],
        compiler_params=_params("arbitrary"),
        name="filt",
    )(*args)


def _cos_sin(idx, n):
    ang = (2.0 * math.pi / n) * (idx % n).astype(f32)
    return jnp.cos(ang), jnp.sin(ang)


def _dft_matrices(L):
    n = 2 * L
    na = n // NB
    ha = na // 2
    b = jnp.arange(NB, dtype=jnp.int32)[:, None]
    p = jnp.arange(na, dtype=jnp.int32)
    ca, sa = _cos_sin(NB * p[:, None] * p[None, :], n)
    cb, sb = _cos_sin(b * p[None, :], n)
    cb, sb = cb[:, :, None], sb[:, :, None]
    c = ca[None] * cb - sa[None] * sb
    s = sa[None] * cb + ca[None] * sb
    ch, sh = c[:, :, :ha], s[:, :, :ha]
    m_data = jnp.concatenate([jnp.concatenate([ch, sh], axis=2),
                              jnp.concatenate([-sh, ch], axis=2)], axis=1)
    m_real = jnp.concatenate([c, -s], axis=1)
    cht, sht = jnp.swapaxes(ch, 1, 2), jnp.swapaxes(sh, 1, 2)
    m_inv = jnp.concatenate([jnp.concatenate([cht, -sht], axis=2),
                             jnp.concatenate([sht, cht], axis=2)], axis=1)
    q = jnp.arange(NB, dtype=jnp.int32)
    gc, gs = _cos_sin(q[:, None] * q[None, :], NB)
    g_fwd = jnp.concatenate([jnp.concatenate([gc, gs], axis=1),
                             jnp.concatenate([-gs, gc], axis=1)], axis=0)
    g_inv = jnp.concatenate([jnp.concatenate([gc, -gs], axis=1),
                             jnp.concatenate([gs, gc], axis=1)], axis=0)
    return (m_data.astype(bf16), m_real.astype(bf16), m_inv.astype(bf16),
            g_fwd.astype(bf16), g_inv.astype(bf16))


def _first_stage(x_ref, m_ref, s_ref, s, na):
    for j in range(FFT_BA):
        res = jnp.dot(m_ref[j], x_ref[j].astype(bf16), preferred_element_type=f32)
        rows = pl.ds(s * FFT_BA + j, na, stride=FFT_PITCH)
        _put_cols(s_ref.at[0], rows, res[:na])
        _put_cols(s_ref.at[1], rows, res[na:])


def _slab(p):
    return pl.ds(pl.multiple_of(p * FFT_PITCH, SUBLANES), NB)


def _second_stage(s_ref, gf_ref, p):
    rows = _slab(p)
    y = jnp.concatenate([_get_cols(s_ref.at[0], rows), _get_cols(s_ref.at[1], rows)], axis=0)
    return jnp.dot(gf_ref[...], y.astype(bf16), preferred_element_type=f32)


def _fftk_kernel(k_ref, ma_ref, l1_ref, gf_ref, o_ref, s_ref, *, na, a_steps, scale):
    s = pl.program_id(1)

    @pl.when(s < a_steps)
    def _():
        _first_stage(k_ref, ma_ref, s_ref, s, na)

    @pl.when(s >= a_steps)
    def _():
        inv = scale / (l1_ref[...] + EPS)
        for j in range(FFT_PB):
            z = _second_stage(s_ref, gf_ref, (s - a_steps) * FFT_PB + j)
            o_ref[0, j] = z[:NB] * inv
            o_ref[1, j] = z[NB:] * inv


def _fft_k(kt, l1, m_real, g_fwd, scale):
    _, _, na, C = kt.shape
    a_steps, b_steps = NB // FFT_BA, na // FFT_PB
    last_a = a_steps - 1
    const = lambda c, s: (0, 0)
    return pl.pallas_call(
        functools.partial(_fftk_kernel, na=na, a_steps=a_steps, scale=scale),
        grid=(C // FFT_CB, a_steps + b_steps),
        in_specs=[pl.BlockSpec((None, FFT_BA, na, FFT_CB), lambda c, s: (0, jnp.minimum(s, last_a), 0, c)),
                  pl.BlockSpec((FFT_BA, 2 * na, na), lambda c, s: (jnp.minimum(s, last_a), 0, 0)),
                  pl.BlockSpec((1, FFT_CB), lambda c, s: (0, c)),
                  pl.BlockSpec(g_fwd.shape, const)],
        out_specs=pl.BlockSpec((2, FFT_PB, NB, FFT_CB), lambda c, s: (0, jnp.clip(s - a_steps, 0, b_steps - 1), 0, c)),
        out_shape=jax.ShapeDtypeStruct((2, na, NB, C), f32),
        scratch_shapes=[pltpu.VMEM((2, FFT_CB // LANES, na * FFT_PITCH, LANES), f32)],
        compiler_params=_params("arbitrary", "arbitrary"),
        name="fft_k",
    )(kt, m_real, l1, g_fwd)


def _fftconv_kernel(ua_ref, ma_ref, k_ref, gf_ref, gi_ref, mc_ref, x0_ref, uc_ref, sk_ref, o_ref, s_ref,
                    *, na, a_steps, b_steps):
    s = pl.program_id(2)

    @pl.when(s < a_steps)
    def _():
        _first_stage(ua_ref, ma_ref, s_ref, s, na)

    @pl.when((s >= a_steps) & (s < a_steps + b_steps))
    def _():
        for j in range(FFT_PB):
            p = (s - a_steps) * FFT_PB + j
            z = _second_stage(s_ref, gf_ref, p)
            zr, zi = z[:NB], z[NB:]
            kr, ki = k_ref[0, j], k_ref[1, j]
            f = jnp.concatenate([zr * kr - zi * ki, zr * ki + zi * kr], axis=0)
            v = jnp.dot(gi_ref[...], f.astype(bf16), preferred_element_type=f32)
            _put_cols(s_ref.at[0], _slab(p), v[:NB])
            _put_cols(s_ref.at[1], _slab(p), v[NB:])

    @pl.when(s >= a_steps + b_steps)
    def _():
        for j in range(SUBLANES):
            rows = pl.ds((s - (a_steps + b_steps)) * SUBLANES + j, na, stride=FFT_PITCH)
            v = jnp.concatenate([_get_cols(s_ref.at[0], rows), _get_cols(s_ref.at[1], rows)], axis=0)
            conv = jnp.dot(mc_ref[j], v.astype(bf16), preferred_element_type=f32)
            o_ref[j] = x0_ref[j] * (conv + sk_ref[...] * uc_ref[j])


def _fft_conv(u, x0, kf, skip, m_data, m_inv, g_fwd, g_inv):
    P, _, na, C = u.shape
    a_steps, b_steps, c_steps = NB // FFT_BA, na // FFT_PB, NB // SUBLANES
    last_a = a_steps - 1
    b_idx = lambda s: jnp.clip(s - a_steps, 0, b_steps - 1)
    c_idx = lambda s: jnp.clip(s - (a_steps + b_steps), 0, c_steps - 1)
    const = lambda c, r, s: (0, 0)
    c_spec = pl.BlockSpec((None, SUBLANES, na, FFT_CB), lambda c, r, s: (r, c_idx(s), 0, c))
    return pl.pallas_call(
        functools.partial(_fftconv_kernel, na=na, a_steps=a_steps, b_steps=b_steps),
        grid=(C // FFT_CB, P, a_steps + b_steps + c_steps),
        in_specs=[pl.BlockSpec((None, FFT_BA, na, FFT_CB), lambda c, r, s: (r, jnp.minimum(s, last_a), 0, c)),
                  pl.BlockSpec((FFT_BA, 2 * na, na), lambda c, r, s: (jnp.minimum(s, last_a), 0, 0)),
                  pl.BlockSpec((2, FFT_PB, NB, FFT_CB), lambda c, r, s: (0, b_idx(s), 0, c)),
                  pl.BlockSpec(g_fwd.shape, const),
                  pl.BlockSpec(g_inv.shape, const),
                  pl.BlockSpec((SUBLANES, na, 2 * na), lambda c, r, s: (c_idx(s), 0, 0)),
                  c_spec, c_spec,
                  pl.BlockSpec((1, FFT_CB), lambda c, r, s: (0, c))],
        out_specs=c_spec,
        out_shape=jax.ShapeDtypeStruct(u.shape, f32),
        scratch_shapes=[pltpu.VMEM((2, FFT_CB // LANES, na * FFT_PITCH, LANES), f32)],
        compiler_params=_params("arbitrary", "arbitrary", "arbitrary"),
        name="fft_conv",
    )(u, m_data, kf, g_fwd, g_inv, m_inv, x0, u, skip)


def _carry_scan(at, bt, reverse):
    n = at.shape[0]
    row = lax.broadcasted_iota(jnp.int32, at.shape, 0)
    s = 1
    while s < n:
        keep = (row < n - s) if reverse else (row >= s)
        shift = n - s if reverse else s
        ash = jnp.where(keep, pltpu.roll(at, shift, 0), 1.0)
        bsh = jnp.where(keep, pltpu.roll(bt, shift, 0), 0.0)
        bt = at * bsh + bt
        at = at * ash
        s *= 2
    if reverse:
        return jnp.where(row < n - 1, pltpu.roll(bt, n - 1, 0), 0.0)
    return jnp.where(row >= 1, pltpu.roll(bt, 1, 0), 0.0)


def _lru_kernel(xb_ref, wg_ref, bg_ref, lam_ref, o_ref, hf_ref, af_ref, hb_ref, ab_ref, *, ha):
    bt = SUBLANES
    nc = NB // bt
    lam = lam_ref[...]
    half_c = (-0.5 * LRU_C) * (jnp.maximum(-lam, 0.0) + jnp.log1p(jnp.exp(-jnp.abs(lam))))
    rowi = lax.broadcasted_iota(jnp.int32, (bt * ha, 1), 0)

    def gates(d, k, first):
        b0 = pl.multiple_of(k * bt, bt)
        xc = xb_ref[pl.ds(b0, bt)].reshape(bt * ha, LANES)
        t = jnp.tanh(jnp.dot(xc.astype(bf16), wg_ref[d], preferred_element_type=f32) + bg_ref[d:d + 1, :])
        hc = half_c[d:d + 1, :]
        log_a = hc * t[:, :LANES] + hc
        a = jnp.exp(log_a)
        m2 = jnp.tanh(log_a) * (-1.0 - a * a)
        mult = jnp.where(m2 > 0.0, m2 * lax.rsqrt(m2), 0.0)
        if first:
            mult = jnp.where(rowi == (bt * ha - 1 if d else 0), 1.0, mult)
        hx = 0.5 * xc
        bv = mult * (t[:, LANES:] * hx + hx)
        return b0, a.reshape(bt, ha, LANES), bv.reshape(bt, ha, LANES)

    def step(k, carry, first=False):
        hf, af, hb, ab = carry
        b0, a, bv = gates(0, k, first)
        for j in range(bt):
            hf = a[j] * hf + bv[j]
            af = a[j] * af
            hf_ref[b0 + j] = hf
            af_ref[b0 + j] = af
        b0, a, bv = gates(1, nc - 1 - k, first)
        for j in reversed(range(bt)):
            hb = a[j] * hb + bv[j]
            ab = a[j] * ab
            hb_ref[b0 + j] = hb
            ab_ref[b0 + j] = ab
        return hf, af, hb, ab

    zero = jnp.zeros((ha, LANES), f32)
    one = jnp.ones((ha, LANES), f32)
    carry = step(0, (zero, one, zero, one), first=True)
    hf, af, hb, ab = lax.fori_loop(1, nc, step, carry)
    cf = _carry_scan(af, hf, reverse=False)
    cb = _carry_scan(ab, hb, reverse=True)

    def finish(k, c):
        b0 = pl.multiple_of(k * bt, bt)
        sl = pl.ds(b0, bt)
        o_ref[sl] = (hf_ref[sl] + af_ref[sl] * cf) + (hb_ref[sl] + ab_ref[sl] * cb)
        return c

    lax.fori_loop(0, nc, finish, 0)


def _lru_gate_weights(wa, wx):
    def blockdiag(w):
        w = w.reshape(2, -1, 2, HEAD, HEAD)
        z = jnp.zeros_like(w[:, :, 0])
        top = jnp.concatenate([w[:, :, 0], z], axis=-1)
        bot = jnp.concatenate([z, w[:, :, 1]], axis=-1)
        return jnp.concatenate([top, bot], axis=-2)
    return jnp.concatenate([blockdiag(wa), blockdiag(wx)], axis=-1).astype(bf16)


def _lru(xb, wa, ba, wx, bx, lam):
    P, _, _, ha, C = xb.shape
    nblk = C // LANES
    wg = _lru_gate_weights(0.5 * wa, 0.5 * wx)
    bg = 0.5 * jnp.concatenate([ba.reshape(2, nblk, 1, LANES), bx.reshape(2, nblk, 1, LANES)], axis=-1)
    ba_spec = pl.BlockSpec((None, NB, None, ha, LANES), lambda b, c: (b // 2, 0, b % 2, 0, c))
    return pl.pallas_call(
        functools.partial(_lru_kernel, ha=ha),
        grid=(2 * P, nblk),
        in_specs=[ba_spec,
                  pl.BlockSpec((2, None, LANES, 2 * LANES), lambda b, c: (0, c, 0, 0)),
                  pl.BlockSpec((2, None, None, 2 * LANES), lambda b, c: (0, c, 0, 0)),
                  pl.BlockSpec((2, LANES), lambda b, c: (0, c))],
        out_specs=ba_spec,
        out_shape=jax.ShapeDtypeStruct(xb.shape, f32),
        scratch_shapes=[pltpu.VMEM((NB, ha, LANES), f32)] * 4,
        compiler_params=_params("parallel", "arbitrary"),
        name="lru",
    )(xb, wg, bg, lam)


def _out_kernel(yh_ref, hg_ref, yl_ref, lg_ref, x_ref, hog_ref, log_ref, wo_ref, fg_ref, o_ref, ys_ref):
    flat = lambda ref: ref[...].reshape(TILE, -1)
    ycat = jnp.concatenate([_rms(flat(yh_ref), hog_ref[...]) * hg_ref[...],
                            _rms(flat(yl_ref), log_ref[...]) * lg_ref[...]], axis=-1)
    y = jnp.dot(ycat.astype(bf16), wo_ref[...], preferred_element_type=f32)
    _put_cols(ys_ref, pl.ds(0, TILE), y)
    for a in range(SUBLANES):
        rows = slice(a * NB, (a + 1) * NB)
        ya = _get_cols(ys_ref, pl.ds(a, NB, stride=SUBLANES))
        o_ref[rows, :] = _rms(x_ref[rows, :] + ya, fg_ref[...])


def _out(yh, hg, yl, lg, x, hog, log_g, w_out, fg):
    B, L, D = x.shape
    const = lambda b, i: (0, 0)
    nat = pl.BlockSpec((None, TILE, D), lambda b, i: (b, i, 0))
    return pl.pallas_call(
        _out_kernel,
        grid=(B, L // TILE),
        in_specs=[_ba_spec(D_HY), _gate_spec(D_HY), _ba_spec(D_LRU), _gate_spec(D_LRU), nat,
                  pl.BlockSpec((1, D_HY), const), pl.BlockSpec((1, D_LRU), const),
                  pl.BlockSpec(w_out.shape, const), pl.BlockSpec((1, D), const)],
        out_specs=nat,
        out_shape=jax.ShapeDtypeStruct((B, L, D), f32),
        scratch_shapes=[pltpu.VMEM((D // LANES, TILE, LANES), f32)],
        compiler_params=_params("parallel", "arbitrary"),
        name="out",
    )(yh, hg, yl, lg, x, hog, log_g, w_out, fg)


def kernel(x, norm_g, w_in, hy_conv_w, hy_conv_b, flt_w1, flt_b1, flt_f1, flt_w2, flt_b2, flt_f2,
           flt_w3, flt_b3, flt_f3, flt_w4, hy_skip, lru_conv_w, lru_conv_b, lru_wa, lru_ba, lru_wx,
           lru_bx, lru_lam, hy_out_g, lru_out_g, w_out, final_g):
    B, L, D = x.shape
    assert norm_g.shape[0] == 1, "one layer"
    assert B % 2 == 0 and L % TILE == 0
    ha = L // NB
    na = 2 * ha
    row = lambda v: v.reshape(1, -1)

    u, x0, hg, xb, lg = _inproj(x, row(norm_g[0]), w_in[0].astype(bf16), hy_conv_w[0], row(hy_conv_b[0]),
                                lru_conv_w[0], row(lru_conv_b[0]))
    pair = lambda t: t.reshape(B // 2, NB, na, D_HY)

    m_data, m_real, m_inv, g_fwd, g_inv = _dft_matrices(L)
    kt, l1 = _filter(L, flt_w1[0], flt_b1[0], flt_f1[0], flt_w2[0], flt_b2[0], flt_f2[0],
                     flt_w3[0], flt_b3[0], flt_f3[0], flt_w4[0])
    kf = _fft_k(kt, l1, m_real, g_fwd, scale=1.0 / (2 * L))
    yh = _fft_conv(pair(u), pair(x0), kf, row(hy_skip[0]), m_data, m_inv, g_fwd, g_inv).reshape(u.shape)

    yl = _lru(xb, lru_wa[0], lru_ba[0], lru_wx[0], lru_bx[0], lru_lam[0])
    return _out(yh, hg, yl, lg, x, row(hy_out_g[0]), row(lru_out_g[0]), w_out[0].astype(bf16), row(final_g))
```

```python
import functools
import math

import jax
import jax.numpy as jnp
from jax import lax
from jax.experimental import pallas as pl
from jax.experimental.pallas import tpu as pltpu

f32 = jnp.float32
bf16 = jnp.bfloat16

D_HY = 768
D_LRU = 768
HEAD = 64
LANES = 128
SUBLANES = 8
NB = 128
TILE = SUBLANES * NB
HALO = SUBLANES
FFT_CB = 256
FFT_BA = 16
FFT_PB = 8
FFT_BC = 16
FFT_PITCH = NB + SUBLANES
FILTER_BANDS = 16
FILTER_EMB = 2 * FILTER_BANDS + 1
MASK_COL = FILTER_EMB
FILTER_TARGET = 1e-2
MIN_DECAY = math.log(FILTER_TARGET) / 0.3
MAX_DECAY = math.log(FILTER_TARGET) / 1.5
LRU_C = 8.0
EPS = 1e-6
VMEM_LIMIT = 60 * 1024 * 1024


def _params(*sem):
    return pltpu.CompilerParams(dimension_semantics=sem, vmem_limit_bytes=VMEM_LIMIT)


def _rms(y, g):
    return y * lax.rsqrt(jnp.mean(y * y, axis=-1, keepdims=True) + EPS) * g


def _put_cols(ref, rows, val):
    for h in range(ref.shape[0]):
        ref[h, rows, :] = val[:, h * LANES:(h + 1) * LANES]


def _get_cols(ref, rows):
    return jnp.concatenate([ref[h, rows, :] for h in range(ref.shape[0])], axis=-1)


def _sigmoid(x):
    return 0.5 * jnp.tanh(0.5 * x) + 0.5


def _inproj_kernel(x_ref, xp_ref, xn_ref, g_ref, w_ref, hcw_ref, hcb_ref, lcw_ref, lcb_ref,
                   u_ref, x0_ref, hg_ref, xb_ref, lg_ref, xs_ref, *, n_tiles):
    i = pl.program_id(1)
    g = g_ref[...]
    for a in range(SUBLANES):
        _put_cols(xs_ref, pl.ds(a, NB, stride=SUBLANES), _rms(x_ref[a * NB:(a + 1) * NB, :], g))
    _put_cols(xs_ref, pl.ds(TILE, HALO), jnp.where(i > 0, _rms(xp_ref[...], g), 0.0))
    _put_cols(xs_ref, pl.ds(TILE + HALO, HALO), jnp.where(i < n_tiles - 1, _rms(xn_ref[...], g), 0.0))
    xn = _get_cols(xs_ref, pl.ds(0, TILE + 2 * HALO)).astype(bf16)
    sub = lax.broadcasted_iota(jnp.int32, (SUBLANES, D_HY), 0)

    def proj(c0):
        p = jnp.dot(xn, w_ref[:, c0:c0 + D_HY], preferred_element_type=f32)
        return p.reshape(NB + 2, SUBLANES, D_HY)

    def edge(p3, s):
        if s < 0:
            return jnp.where(sub == 0, p3[NB][SUBLANES + s:SUBLANES + s + 1],
                             pltpu.roll(p3[NB + s], 1, 0))
        return jnp.where(sub == SUBLANES - 1, p3[NB + 1][s - NB:s - NB + 1],
                         pltpu.roll(p3[s - NB], SUBLANES - 1, 0))

    def conv(p3, cw_ref, cb_ref, c0, offsets):
        lo, hi = max(0, -min(offsets)), NB - max(offsets)

        def acc(get):
            y = cb_ref[:, c0:c0 + D_HY]
            for k, o in enumerate(offsets):
                y = y + get(o) * cw_ref[k:k + 1, c0:c0 + D_HY]
            return y

        inner = acc(lambda o: p3[lo + o:hi + o])
        edges = {b: acc(lambda o, b=b: p3[b + o] if 0 <= b + o < NB else edge(p3, b + o))
                 for b in list(range(lo)) + list(range(hi, NB))}
        return lo, hi, inner, edges

    def store(ref, conv_out, other=None):
        lo, hi, inner, edges = conv_out
        if other is not None:
            inner = inner * other[2]
            edges = {b: edges[b] * other[3][b] for b in edges}
        ref[lo:hi] = inner
        for b, y in edges.items():
            ref[b] = y

    hy = (-1, 0, 1)
    store(u_ref, conv(proj(0), hcw_ref, hcb_ref, 0, hy), conv(proj(2 * D_HY), hcw_ref, hcb_ref, 2 * D_HY, hy))
    store(x0_ref, conv(proj(D_HY), hcw_ref, hcb_ref, D_HY, hy))
    store(xb_ref, conv(proj(4 * D_HY), lcw_ref, lcb_ref, 0, (-1, 0, 1, 2)))
    xm = xn[:TILE]
    hg = jnp.dot(xm, w_ref[:, 3 * D_HY:4 * D_HY], preferred_element_type=f32)
    hg_ref[...] = (hg * _sigmoid(hg)).astype(bf16)
    lg = jnp.dot(xm, w_ref[:, 4 * D_HY + D_LRU:], preferred_element_type=f32)
    lg_ref[...] = (lg * _sigmoid(lg)).astype(bf16)


def _ba_spec(c):
    return pl.BlockSpec((None, NB, None, SUBLANES, c), lambda b, i: (b // 2, 0, b % 2, i, 0))


def _gate_spec(c):
    return pl.BlockSpec((None, None, TILE, c), lambda b, i: (b, i, 0, 0))


def _inproj(x, norm_g, w_in, hcw, hcb, lcw, lcb):
    B, L, D = x.shape
    n_tiles = L // TILE
    ha = L // NB
    hb = TILE // HALO
    n_hb = L // HALO
    const = lambda b, i: (0, 0)
    ba_shape = jax.ShapeDtypeStruct((B // 2, NB, 2, ha, D_HY), f32)
    gate_shape = jax.ShapeDtypeStruct((B, n_tiles, TILE, D_HY), bf16)
    return pl.pallas_call(
        functools.partial(_inproj_kernel, n_tiles=n_tiles),
        grid=(B, n_tiles),
        in_specs=[
            pl.BlockSpec((None, TILE, D), lambda b, i: (b, i, 0)),
            pl.BlockSpec((None, HALO, D), lambda b, i: (b, jnp.maximum(i * hb - 1, 0), 0)),
            pl.BlockSpec((None, HALO, D), lambda b, i: (b, jnp.minimum((i + 1) * hb, n_hb - 1), 0)),
            pl.BlockSpec((1, D), const),
            pl.BlockSpec(w_in.shape, const, pipeline_mode=pl.Buffered(1)),
            pl.BlockSpec(hcw.shape, const),
            pl.BlockSpec(hcb.shape, const),
            pl.BlockSpec(lcw.shape, const),
            pl.BlockSpec(lcb.shape, const),
        ],
        out_specs=[_ba_spec(D_HY), _ba_spec(D_HY), _gate_spec(D_HY), _ba_spec(D_LRU), _gate_spec(D_LRU)],
        out_shape=[ba_shape, ba_shape, gate_shape, ba_shape, gate_shape],
        scratch_shapes=[pltpu.VMEM((D // LANES, TILE + 2 * HALO, LANES), f32)],
        compiler_params=_params("parallel", "arbitrary"),
        name="inproj",
    )(x, x, x, norm_g, w_in, hcw, hcb, lcw, lcb)


def _filt_kernel(zf_ref, zb_ref, dl_ref, w1_ref, b1_ref, f1_ref, w2_ref, b2_ref, f2_ref,
                 w3_ref, b3_ref, f3_ref, w4f_ref, w4b_ref, o_ref, s_ref, *, ha):
    dot = functools.partial(jnp.dot, precision=lax.Precision.HIGHEST, preferred_element_type=f32)
    dl = dl_ref[...]
    zf = zf_ref[...]
    zb = zb_ref[...]
    h = jnp.concatenate([dot(zf, w1_ref[...]), dot(zb, w1_ref[...])], axis=-1)
    h = jnp.sin(f1_ref[...] * (h + b1_ref[...]))
    h = jnp.sin(f2_ref[...] * (dot(h, w2_ref[...]) + b2_ref[...]))
    h = jnp.sin(f3_ref[...] * (dot(h, w3_ref[...]) + b3_ref[...]))
    nh = h.shape[-1] // 2
    hf = dot(h[:, :nh], w4f_ref[...]) * jnp.exp(-zf[:, 0:1] * dl)
    hb = dot(h[:, nh:], w4b_ref[...]) * jnp.exp(-zb[:, 0:1] * dl) * zb[:, MASK_COL:MASK_COL + 1]
    o_ref[:, :ha, :] = hf.reshape(SUBLANES, ha, D_HY)
    o_ref[:, ha:, :] = hb.reshape(SUBLANES, ha, D_HY)

    @pl.when(pl.program_id(0) == 0)
    def _():
        s_ref[...] = jnp.zeros_like(s_ref)

    s_ref[...] += jnp.sum(jnp.abs(hf), axis=0, keepdims=True) + jnp.sum(jnp.abs(hb), axis=0, keepdims=True)


def _filter_tables(L):
    ha = L // NB
    t = jnp.linspace(0.0, 1.0, L, dtype=f32)
    w = (2.0 * math.pi / L) * jnp.arange(L, dtype=f32)
    f = jnp.linspace(1e-4, FILTER_BANDS - 1, FILTER_BANDS, dtype=f32)[None, :]

    def features(lag, mask):
        tl, wl = t[lag][:, None], w[lag][:, None]
        pad = jnp.zeros((lag.shape[0], LANES - FILTER_EMB - 1), f32)
        return jnp.concatenate([tl, jnp.cos(wl * f), -jnp.sin(wl * f), mask[:, None], pad], axis=-1)

    b = jnp.arange(NB, dtype=jnp.int32)[:, None]
    a = jnp.arange(ha, dtype=jnp.int32)[None, :]
    lag_f = (NB * a + b).reshape(-1)
    lag_b = (L - NB * a - b).reshape(-1)
    valid = lag_b < L
    return features(lag_f, jnp.zeros(lag_f.shape, f32)), features(jnp.where(valid, lag_b, 0), valid.astype(f32))


def _filter(L, w1, b1, f1, w2, b2, f2, w3, b3, f3, w4):
    ha = L // NB
    rc = SUBLANES * ha
    zf, zb = _filter_tables(L)
    dl = jnp.abs(jnp.linspace(MIN_DECAY, MAX_DECAY, D_HY, dtype=f32))[None, :]
    w1p = jnp.pad(w1, ((0, LANES - w1.shape[0]), (0, 0)))
    row = lambda v: jnp.tile(v.reshape(1, -1), (1, 2))
    zero = jnp.zeros_like(w2)
    diag2 = lambda w: jnp.concatenate([jnp.concatenate([w, zero], axis=1),
                                       jnp.concatenate([zero, w], axis=1)], axis=0)
    const = lambda j: (0, 0)
    full = lambda arr: pl.BlockSpec(arr.shape, const)
    args = [zf, zb, dl, w1p, row(b1), row(f1), diag2(w2), row(b2), row(f2), diag2(w3), row(b3), row(f3), w4, w4]
    specs = [full(a) for a in args]
    specs[0] = specs[1] = pl.BlockSpec((rc, LANES), lambda j: (j, 0))
    specs[12] = pl.BlockSpec((w4.shape[0], D_HY), lambda j: (0, 0))
    specs[13] = pl.BlockSpec((w4.shape[0], D_HY), lambda j: (0, 1))
    return pl.pallas_call(
        functools.partial(_filt_kernel, ha=ha),
        grid=(NB // SUBLANES,),
        in_specs=specs,
        out_specs=[pl.BlockSpec((None, SUBLANES, 2 * ha, D_HY), lambda j: (0, j, 0, 0)),
                   pl.BlockSpec((1, D_HY), const)],
        out_shape=[jax.ShapeDtypeStruct((1, NB, 2 * ha, D_HY), f32),
                   jax.ShapeDtypeStruct((1, D_HY), f32)],
        compiler_params=_params("arbitrary"),
        name="filt",
    )(*args)


def _cos_sin(idx, n):
    ang = (2.0 * math.pi / n) * (idx % n).astype(f32)
    return jnp.cos(ang), jnp.sin(ang)


def _dft_matrices(L):
    n = 2 * L
    na = n // NB
    ha = na // 2
    b = jnp.arange(NB, dtype=jnp.int32)[:, None]
    p = jnp.arange(na, dtype=jnp.int32)
    ca, sa = _cos_sin(NB * p[:, None] * p[None, :], n)
    cb, sb = _cos_sin(b * p[None, :], n)
    cb, sb = cb[:, :, None], sb[:, :, None]
    c = ca[None] * cb - sa[None] * sb
    s = sa[None] * cb + ca[None] * sb
    ch, sh = c[:, :, :ha], s[:, :, :ha]
    m_data = jnp.concatenate([jnp.concatenate([ch, sh], axis=2),
                              jnp.concatenate([-sh, ch], axis=2)], axis=1)
    m_real = jnp.concatenate([c, -s], axis=1)
    cht, sht = jnp.swapaxes(ch, 1, 2), jnp.swapaxes(sh, 1, 2)
    m_inv = jnp.concatenate([jnp.concatenate([cht, -sht], axis=2),
                             jnp.concatenate([sht, cht], axis=2)], axis=1)
    q = jnp.arange(NB, dtype=jnp.int32)
    gc, gs = _cos_sin(q[:, None] * q[None, :], NB)
    g_fwd = jnp.concatenate([jnp.concatenate([gc, gs], axis=1),
                             jnp.concatenate([-gs, gc], axis=1)], axis=0)
    g_inv = jnp.concatenate([jnp.concatenate([gc, -gs], axis=1),
                             jnp.concatenate([gs, gc], axis=1)], axis=0)
    return (m_data.astype(bf16), m_real.astype(bf16), m_inv.astype(bf16),
            g_fwd.astype(bf16), g_inv.astype(bf16))


def _first_stage(x_ref, m_ref, s_ref, s, na):
    for j in range(FFT_BA):
        res = jnp.dot(m_ref[j], x_ref[j].astype(bf16), preferred_element_type=f32)
        rows = pl.ds(s * FFT_BA + j, na, stride=FFT_PITCH)
        _put_cols(s_ref.at[0], rows, res[:na])
        _put_cols(s_ref.at[1], rows, res[na:])


def _slab(p):
    return pl.ds(pl.multiple_of(p * FFT_PITCH, SUBLANES), NB)


def _second_stage(s_ref, gf_ref, p):
    rows = _slab(p)
    y = jnp.concatenate([_get_cols(s_ref.at[0], rows), _get_cols(s_ref.at[1], rows)], axis=0)
    return jnp.dot(gf_ref[...], y.astype(bf16), preferred_element_type=f32)


def _fftk_kernel(k_ref, ma_ref, l1_ref, sk_ref, gf_ref, o_ref, s_ref, *, na, a_steps, scale):
    s = pl.program_id(1)

    @pl.when(s < a_steps)
    def _():
        _first_stage(k_ref, ma_ref, s_ref, s, na)

    @pl.when(s >= a_steps)
    def _():
        inv = scale / (l1_ref[...] + EPS)
        tap = scale * sk_ref[...]
        for j in range(FFT_PB):
            z = _second_stage(s_ref, gf_ref, (s - a_steps) * FFT_PB + j)
            o_ref[0, j] = z[:NB] * inv + tap
            o_ref[1, j] = z[NB:] * inv


def _fft_k(kt, l1, skip, m_real, g_fwd, scale):
    _, _, na, C = kt.shape
    a_steps, b_steps = NB // FFT_BA, na // FFT_PB
    last_a = a_steps - 1
    const = lambda c, s: (0, 0)
    return pl.pallas_call(
        functools.partial(_fftk_kernel, na=na, a_steps=a_steps, scale=scale),
        grid=(C // FFT_CB, a_steps + b_steps),
        in_specs=[pl.BlockSpec((None, FFT_BA, na, FFT_CB), lambda c, s: (0, jnp.minimum(s, last_a), 0, c)),
                  pl.BlockSpec((FFT_BA, 2 * na, na), lambda c, s: (jnp.minimum(s, last_a), 0, 0)),
                  pl.BlockSpec((1, FFT_CB), lambda c, s: (0, c)),
                  pl.BlockSpec((1, FFT_CB), lambda c, s: (0, c)),
                  pl.BlockSpec(g_fwd.shape, const)],
        out_specs=pl.BlockSpec((2, FFT_PB, NB, FFT_CB), lambda c, s: (0, jnp.clip(s - a_steps, 0, b_steps - 1), 0, c)),
        out_shape=jax.ShapeDtypeStruct((2, na, NB, C), f32),
        scratch_shapes=[pltpu.VMEM((2, FFT_CB // LANES, na * FFT_PITCH, LANES), f32)],
        compiler_params=_params("arbitrary", "arbitrary"),
        name="fft_k",
    )(kt, m_real, l1, skip, g_fwd)


def _fftconv_kernel(ua_ref, ma_ref, k_ref, gf_ref, gi_ref, mc_ref, x0_ref, o_ref, s_ref,
                    *, na, a_steps, b_steps):
    s = pl.program_id(2)

    @pl.when(s < a_steps)
    def _():
        _first_stage(ua_ref, ma_ref, s_ref, s, na)

    @pl.when((s >= a_steps) & (s < a_steps + b_steps))
    def _():
        for j in range(FFT_PB):
            p = (s - a_steps) * FFT_PB + j
            z = _second_stage(s_ref, gf_ref, p)
            zr, zi = z[:NB], z[NB:]
            kr, ki = k_ref[0, j], k_ref[1, j]
            f = jnp.concatenate([zr * kr - zi * ki, zr * ki + zi * kr], axis=0)
            v = jnp.dot(gi_ref[...], f.astype(bf16), preferred_element_type=f32)
            _put_cols(s_ref.at[0], _slab(p), v[:NB])
            _put_cols(s_ref.at[1], _slab(p), v[NB:])

    @pl.when(s >= a_steps + b_steps)
    def _():
        for j in range(FFT_BC):
            rows = pl.ds((s - (a_steps + b_steps)) * FFT_BC + j, na, stride=FFT_PITCH)
            v = jnp.concatenate([_get_cols(s_ref.at[0], rows), _get_cols(s_ref.at[1], rows)], axis=0)
            conv = jnp.dot(mc_ref[j], v.astype(bf16), preferred_element_type=f32)
            o_ref[j] = x0_ref[j] * conv


def _fft_conv(u, x0, kf, m_data, m_inv, g_fwd, g_inv):
    P, _, na, C = u.shape
    a_steps, b_steps, c_steps = NB // FFT_BA, na // FFT_PB, NB // FFT_BC
    last_a = a_steps - 1
    b_idx = lambda s: jnp.clip(s - a_steps, 0, b_steps - 1)
    c_idx = lambda s: jnp.clip(s - (a_steps + b_steps), 0, c_steps - 1)
    const = lambda c, r, s: (0, 0)
    c_spec = pl.BlockSpec((None, FFT_BC, na, FFT_CB), lambda c, r, s: (r, c_idx(s), 0, c))
    return pl.pallas_call(
        functools.partial(_fftconv_kernel, na=na, a_steps=a_steps, b_steps=b_steps),
        grid=(C // FFT_CB, P, a_steps + b_steps + c_steps),
        in_specs=[pl.BlockSpec((None, FFT_BA, na, FFT_CB), lambda c, r, s: (r, jnp.minimum(s, last_a), 0, c)),
                  pl.BlockSpec((FFT_BA, 2 * na, na), lambda c, r, s: (jnp.minimum(s, last_a), 0, 0)),
                  pl.BlockSpec((2, FFT_PB, NB, FFT_CB), lambda c, r, s: (0, b_idx(s), 0, c)),
                  pl.BlockSpec(g_fwd.shape, const),
                  pl.BlockSpec(g_inv.shape, const),
                  pl.BlockSpec((FFT_BC, na, 2 * na), lambda c, r, s: (c_idx(s), 0, 0)),
                  c_spec],
        out_specs=c_spec,
        out_shape=jax.ShapeDtypeStruct(u.shape, f32),
        scratch_shapes=[pltpu.VMEM((2, FFT_CB // LANES, na * FFT_PITCH, LANES), f32)],
        compiler_params=_params("arbitrary", "arbitrary", "arbitrary"),
        name="fft_conv",
    )(u, m_data, kf, g_fwd, g_inv, m_inv, x0)


def _carry_scan(at, bt, reverse):
    n = at.shape[0]
    row = lax.broadcasted_iota(jnp.int32, at.shape, 0)
    s = 1
    while s < n:
        keep = (row < n - s) if reverse else (row >= s)
        shift = n - s if reverse else s
        ash = jnp.where(keep, pltpu.roll(at, shift, 0), 1.0)
        bsh = jnp.where(keep, pltpu.roll(bt, shift, 0), 0.0)
        bt = at * bsh + bt
        at = at * ash
        s *= 2
    if reverse:
        return jnp.where(row < n - 1, pltpu.roll(bt, n - 1, 0), 0.0)
    return jnp.where(row >= 1, pltpu.roll(bt, 1, 0), 0.0)


def _lru_kernel(xb_ref, wg_ref, bg_ref, lam_ref, o_ref, hf_ref, af_ref, hb_ref, ab_ref, *, ha):
    bt = SUBLANES
    nc = NB // bt
    lam = lam_ref[...]
    half_c = (-0.5 * LRU_C) * (jnp.maximum(-lam, 0.0) + jnp.log1p(jnp.exp(-jnp.abs(lam))))
    rowi = lax.broadcasted_iota(jnp.int32, (bt * ha, 1), 0)

    def gates(d, k, first):
        b0 = pl.multiple_of(k * bt, bt)
        xc = xb_ref[pl.ds(b0, bt)].reshape(bt * ha, LANES)
        t = jnp.tanh(jnp.dot(xc.astype(bf16), wg_ref[d], preferred_element_type=f32) + bg_ref[d:d + 1, :])
        hc = half_c[d:d + 1, :]
        log_a = hc * t[:, :LANES] + hc
        a = jnp.exp(log_a)
        m2 = jnp.tanh(log_a) * (-1.0 - a * a)
        mult = jnp.where(m2 > 0.0, m2 * lax.rsqrt(m2), 0.0)
        if first:
            mult = jnp.where(rowi == (bt * ha - 1 if d else 0), 1.0, mult)
        hx = 0.5 * xc
        bv = mult * (t[:, LANES:] * hx + hx)
        return b0, a.reshape(bt, ha, LANES), bv.reshape(bt, ha, LANES)

    def step(k, carry, first=False):
        hf, af, hb, ab = carry
        b0, a, bv = gates(0, k, first)
        for j in range(bt):
            hf = a[j] * hf + bv[j]
            af = a[j] * af
            hf_ref[b0 + j] = hf
            af_ref[b0 + j] = af
        b0, a, bv = gates(1, nc - 1 - k, first)
        for j in reversed(range(bt)):
            hb = a[j] * hb + bv[j]
            ab = a[j] * ab
            hb_ref[b0 + j] = hb
            ab_ref[b0 + j] = ab
        return hf, af, hb, ab

    zero = jnp.zeros((ha, LANES), f32)
    one = jnp.ones((ha, LANES), f32)
    carry = step(0, (zero, one, zero, one), first=True)
    hf, af, hb, ab = lax.fori_loop(1, nc, step, carry)
    cf = _carry_scan(af, hf, reverse=False)
    cb = _carry_scan(ab, hb, reverse=True)

    def finish(k, c):
        b0 = pl.multiple_of(k * bt, bt)
        sl = pl.ds(b0, bt)
        o_ref[sl] = (hf_ref[sl] + af_ref[sl] * cf) + (hb_ref[sl] + ab_ref[sl] * cb)
        return c

    lax.fori_loop(0, nc, finish, 0)


def _lru_gate_weights(wa, wx):
    def blockdiag(w):
        w = w.reshape(2, -1, 2, HEAD, HEAD)
        z = jnp.zeros_like(w[:, :, 0])
        top = jnp.concatenate([w[:, :, 0], z], axis=-1)
        bot = jnp.concatenate([z, w[:, :, 1]], axis=-1)
        return jnp.concatenate([top, bot], axis=-2)
    return jnp.concatenate([blockdiag(wa), blockdiag(wx)], axis=-1).astype(bf16)


def _lru(xb, wa, ba, wx, bx, lam):
    P, _, _, ha, C = xb.shape
    nblk = C // LANES
    wg = _lru_gate_weights(0.5 * wa, 0.5 * wx)
    bg = 0.5 * jnp.concatenate([ba.reshape(2, nblk, 1, LANES), bx.reshape(2, nblk, 1, LANES)], axis=-1)
    ba_spec = pl.BlockSpec((None, NB, None, ha, LANES), lambda b, c: (b // 2, 0, b % 2, 0, c))
    return pl.pallas_call(
        functools.partial(_lru_kernel, ha=ha),
        grid=(2 * P, nblk),
        in_specs=[ba_spec,
                  pl.BlockSpec((2, None, LANES, 2 * LANES), lambda b, c: (0, c, 0, 0)),
                  pl.BlockSpec((2, None, None, 2 * LANES), lambda b, c: (0, c, 0, 0)),
                  pl.BlockSpec((2, LANES), lambda b, c: (0, c))],
        out_specs=ba_spec,
        out_shape=jax.ShapeDtypeStruct(xb.shape, f32),
        scratch_shapes=[pltpu.VMEM((NB, ha, LANES), f32)] * 4,
        compiler_params=_params("parallel", "arbitrary"),
        name="lru",
    )(xb, wg, bg, lam)


def _out_kernel(yh_ref, hg_ref, yl_ref, lg_ref, x_ref, hog_ref, log_ref, wo_ref, fg_ref, o_ref, ys_ref):
    flat = lambda ref: ref[...].reshape(TILE, -1)
    ycat = jnp.concatenate([_rms(flat(yh_ref), hog_ref[...]) * hg_ref[...],
                            _rms(flat(yl_ref), log_ref[...]) * lg_ref[...]], axis=-1)
    y = jnp.dot(ycat.astype(bf16), wo_ref[...], preferred_element_type=f32)
    _put_cols(ys_ref, pl.ds(0, TILE), y)
    for a in range(SUBLANES):
        rows = slice(a * NB, (a + 1) * NB)
        ya = _get_cols(ys_ref, pl.ds(a, NB, stride=SUBLANES))
        o_ref[rows, :] = _rms(x_ref[rows, :] + ya, fg_ref[...])


def _out(yh, hg, yl, lg, x, hog, log_g, w_out, fg):
    B, L, D = x.shape
    const = lambda b, i: (0, 0)
    nat = pl.BlockSpec((None, TILE, D), lambda b, i: (b, i, 0))
    return pl.pallas_call(
        _out_kernel,
        grid=(B, L // TILE),
        in_specs=[_ba_spec(D_HY), _gate_spec(D_HY), _ba_spec(D_LRU), _gate_spec(D_LRU), nat,
                  pl.BlockSpec((1, D_HY), const), pl.BlockSpec((1, D_LRU), const),
                  pl.BlockSpec(w_out.shape, const), pl.BlockSpec((1, D), const)],
        out_specs=nat,
        out_shape=jax.ShapeDtypeStruct((B, L, D), f32),
        scratch_shapes=[pltpu.VMEM((D // LANES, TILE, LANES), f32)],
        compiler_params=_params("parallel", "arbitrary"),
        name="out",
    )(yh, hg, yl, lg, x, hog, log_g, w_out, fg)


def kernel(x, norm_g, w_in, hy_conv_w, hy_conv_b, flt_w1, flt_b1, flt_f1, flt_w2, flt_b2, flt_f2,
           flt_w3, flt_b3, flt_f3, flt_w4, hy_skip, lru_conv_w, lru_conv_b, lru_wa, lru_ba, lru_wx,
           lru_bx, lru_lam, hy_out_g, lru_out_g, w_out, final_g):
    B, L, D = x.shape
    assert norm_g.shape[0] == 1, "one layer"
    assert B % 2 == 0 and L % TILE == 0
    ha = L // NB
    na = 2 * ha
    row = lambda v: v.reshape(1, -1)

    u, x0, hg, xb, lg = _inproj(x, row(norm_g[0]), w_in[0].astype(bf16), hy_conv_w[0], row(hy_conv_b[0]),
                                lru_conv_w[0], row(lru_conv_b[0]))
    pair = lambda t: t.reshape(B // 2, NB, na, D_HY)

    m_data, m_real, m_inv, g_fwd, g_inv = _dft_matrices(L)
    kt, l1 = _filter(L, flt_w1[0], flt_b1[0], flt_f1[0], flt_w2[0], flt_b2[0], flt_f2[0],
                     flt_w3[0], flt_b3[0], flt_f3[0], flt_w4[0])
    kf = _fft_k(kt, l1, row(hy_skip[0]), m_real, g_fwd, scale=1.0 / (2 * L))
    yh = _fft_conv(pair(u), pair(x0), kf, m_data, m_inv, g_fwd, g_inv).reshape(u.shape)

    yl = _lru(xb, lru_wa[0], lru_ba[0], lru_wx[0], lru_bx[0], lru_lam[0])
    return _out(yh, hg, yl, lg, x, row(hy_out_g[0]), row(lru_out_g[0]), w_out[0].astype(bf16), row(final_g))
```

```python
import functools
import math

import jax
import jax.numpy as jnp
import numpy as np
from jax import lax
from jax.experimental import pallas as pl
from jax.experimental.pallas import tpu as pltpu

f32 = jnp.float32
bf16 = jnp.bfloat16

D_HY = 768
D_LRU = 768
HEAD = 64
LANES = 128
SUBLANES = 8
NB = 128
TILE = SUBLANES * NB
HALO = SUBLANES
FFT_CB = 256
FFT_BA = 16
FFT_PB = 8
FFT_BC = 16
FFT_PITCH = NB + SUBLANES
FILTER_BANDS = 16
FILTER_EMB = 2 * FILTER_BANDS + 1
MASK_COL = FILTER_EMB
FILTER_TARGET = 1e-2
MIN_DECAY = math.log(FILTER_TARGET) / 0.3
MAX_DECAY = math.log(FILTER_TARGET) / 1.5
LRU_C = 8.0
EPS = 1e-6
VMEM_LIMIT = 60 * 1024 * 1024


def _params(*sem):
    return pltpu.CompilerParams(dimension_semantics=sem, vmem_limit_bytes=VMEM_LIMIT)


def _rms(y, g):
    return y * lax.rsqrt(jnp.mean(y * y, axis=-1, keepdims=True) + EPS) * g


def _put_cols(ref, rows, val):
    for h in range(ref.shape[0]):
        ref[h, rows, :] = val[:, h * LANES:(h + 1) * LANES]


def _get_cols(ref, rows):
    return jnp.concatenate([ref[h, rows, :] for h in range(ref.shape[0])], axis=-1)


def _sigmoid(x):
    return 0.5 * jnp.tanh(0.5 * x) + 0.5


def _inproj_kernel(x_ref, xp_ref, xn_ref, g_ref, w_ref, hcw_ref, hcb_ref, lcw_ref, lcb_ref,
                   u_ref, x0_ref, hg_ref, xb_ref, lg_ref, xs_ref, *, n_tiles):
    i = pl.program_id(1)
    g = g_ref[...]
    for a in range(SUBLANES):
        _put_cols(xs_ref, pl.ds(a, NB, stride=SUBLANES), _rms(x_ref[a * NB:(a + 1) * NB, :], g))
    _put_cols(xs_ref, pl.ds(TILE, HALO), jnp.where(i > 0, _rms(xp_ref[...], g), 0.0))
    _put_cols(xs_ref, pl.ds(TILE + HALO, HALO), jnp.where(i < n_tiles - 1, _rms(xn_ref[...], g), 0.0))
    xn = _get_cols(xs_ref, pl.ds(0, TILE + 2 * HALO)).astype(bf16)
    sub = lax.broadcasted_iota(jnp.int32, (SUBLANES, D_HY), 0)

    def proj(c0):
        p = jnp.dot(xn, w_ref[:, c0:c0 + D_HY], preferred_element_type=f32)
        return p.reshape(NB + 2, SUBLANES, D_HY)

    def edge(p3, s):
        if s < 0:
            return jnp.where(sub == 0, p3[NB][SUBLANES + s:SUBLANES + s + 1],
                             pltpu.roll(p3[NB + s], 1, 0))
        return jnp.where(sub == SUBLANES - 1, p3[NB + 1][s - NB:s - NB + 1],
                         pltpu.roll(p3[s - NB], SUBLANES - 1, 0))

    def conv(p3, cw_ref, cb_ref, c0, offsets):
        lo, hi = max(0, -min(offsets)), NB - max(offsets)

        def acc(get):
            y = cb_ref[:, c0:c0 + D_HY]
            for k, o in enumerate(offsets):
                y = y + get(o) * cw_ref[k:k + 1, c0:c0 + D_HY]
            return y

        inner = acc(lambda o: p3[lo + o:hi + o])
        edges = {b: acc(lambda o, b=b: p3[b + o] if 0 <= b + o < NB else edge(p3, b + o))
                 for b in list(range(lo)) + list(range(hi, NB))}
        return lo, hi, inner, edges

    def store(ref, conv_out, other=None):
        lo, hi, inner, edges = conv_out
        if other is not None:
            inner = inner * other[2]
            edges = {b: edges[b] * other[3][b] for b in edges}
        ref[lo:hi] = inner
        for b, y in edges.items():
            ref[b] = y

    hy = (-1, 0, 1)
    store(u_ref, conv(proj(0), hcw_ref, hcb_ref, 0, hy), conv(proj(2 * D_HY), hcw_ref, hcb_ref, 2 * D_HY, hy))
    store(x0_ref, conv(proj(D_HY), hcw_ref, hcb_ref, D_HY, hy))
    store(xb_ref, conv(proj(4 * D_HY), lcw_ref, lcb_ref, 0, (-1, 0, 1, 2)))
    xm = xn[:TILE]
    hg = jnp.dot(xm, w_ref[:, 3 * D_HY:4 * D_HY], preferred_element_type=f32)
    hg_ref[...] = (hg * _sigmoid(hg)).astype(bf16)
    lg = jnp.dot(xm, w_ref[:, 4 * D_HY + D_LRU:], preferred_element_type=f32)
    lg_ref[...] = (lg * _sigmoid(lg)).astype(bf16)


def _ba_spec(c):
    return pl.BlockSpec((None, NB, None, SUBLANES, c), lambda b, i: (b // 2, 0, b % 2, i, 0))


def _gate_spec(c):
    return pl.BlockSpec((None, None, TILE, c), lambda b, i: (b, i, 0, 0))


def _inproj(x, norm_g, w_in, hcw, hcb, lcw, lcb):
    B, L, D = x.shape
    n_tiles = L // TILE
    ha = L // NB
    hb = TILE // HALO
    n_hb = L // HALO
    const = lambda b, i: (0, 0)
    ba_shape = jax.ShapeDtypeStruct((B // 2, NB, 2, ha, D_HY), f32)
    gate_shape = jax.ShapeDtypeStruct((B, n_tiles, TILE, D_HY), bf16)
    return pl.pallas_call(
        functools.partial(_inproj_kernel, n_tiles=n_tiles),
        grid=(B, n_tiles),
        in_specs=[
            pl.BlockSpec((None, TILE, D), lambda b, i: (b, i, 0)),
            pl.BlockSpec((None, HALO, D), lambda b, i: (b, jnp.maximum(i * hb - 1, 0), 0)),
            pl.BlockSpec((None, HALO, D), lambda b, i: (b, jnp.minimum((i + 1) * hb, n_hb - 1), 0)),
            pl.BlockSpec((1, D), const),
            pl.BlockSpec(w_in.shape, const, pipeline_mode=pl.Buffered(1)),
            pl.BlockSpec(hcw.shape, const),
            pl.BlockSpec(hcb.shape, const),
            pl.BlockSpec(lcw.shape, const),
            pl.BlockSpec(lcb.shape, const),
        ],
        out_specs=[_ba_spec(D_HY), _ba_spec(D_HY), _gate_spec(D_HY), _ba_spec(D_LRU), _gate_spec(D_LRU)],
        out_shape=[ba_shape, ba_shape, gate_shape, ba_shape, gate_shape],
        scratch_shapes=[pltpu.VMEM((D // LANES, TILE + 2 * HALO, LANES), f32)],
        compiler_params=_params("parallel", "arbitrary"),
        name="inproj",
    )(x, x, x, norm_g, w_in, hcw, hcb, lcw, lcb)


def _filt_kernel(zf_ref, zb_ref, dl_ref, w1_ref, b1_ref, f1_ref, w2_ref, b2_ref, f2_ref,
                 w3_ref, b3_ref, f3_ref, w4f_ref, w4b_ref, o_ref, s_ref, *, ha):
    dot = functools.partial(jnp.dot, precision=lax.Precision.HIGHEST, preferred_element_type=f32)
    dl = dl_ref[...]
    zf = zf_ref[...]
    zb = zb_ref[...]
    h = jnp.concatenate([dot(zf, w1_ref[...]), dot(zb, w1_ref[...])], axis=-1)
    h = jnp.sin(f1_ref[...] * (h + b1_ref[...]))
    h = jnp.sin(f2_ref[...] * (dot(h, w2_ref[...]) + b2_ref[...]))
    h = jnp.sin(f3_ref[...] * (dot(h, w3_ref[...]) + b3_ref[...]))
    nh = h.shape[-1] // 2
    hf = dot(h[:, :nh], w4f_ref[...]) * jnp.exp(-zf[:, 0:1] * dl)
    hb = dot(h[:, nh:], w4b_ref[...]) * jnp.exp(-zb[:, 0:1] * dl) * zb[:, MASK_COL:MASK_COL + 1]
    o_ref[:, :ha, :] = hf.reshape(SUBLANES, ha, D_HY)
    o_ref[:, ha:, :] = hb.reshape(SUBLANES, ha, D_HY)

    @pl.when(pl.program_id(0) == 0)
    def _():
        s_ref[...] = jnp.zeros_like(s_ref)

    s_ref[...] += jnp.sum(jnp.abs(hf), axis=0, keepdims=True) + jnp.sum(jnp.abs(hb), axis=0, keepdims=True)


@functools.lru_cache(maxsize=None)
def _filter_tables(L):
    ha = L // NB
    t = np.linspace(0.0, 1.0, L)
    w = (2.0 * math.pi / L) * np.arange(L)
    f = np.linspace(1e-4, FILTER_BANDS - 1, FILTER_BANDS)[None, :]

    def features(lag, mask):
        tl, wl = t[lag][:, None], w[lag][:, None]
        pad = np.zeros((lag.shape[0], LANES - FILTER_EMB - 1))
        z = np.concatenate([tl, np.cos(wl * f), -np.sin(wl * f), mask[:, None], pad], axis=-1)
        return z.astype(np.float32)

    b = np.arange(NB)[:, None]
    a = np.arange(ha)[None, :]
    lag_f = (NB * a + b).reshape(-1)
    lag_b = (L - NB * a - b).reshape(-1)
    valid = lag_b < L
    dl = np.abs(np.linspace(MIN_DECAY, MAX_DECAY, D_HY))[None, :].astype(np.float32)
    return (features(lag_f, np.zeros(lag_f.shape)),
            features(np.where(valid, lag_b, 0), valid.astype(np.float64)), dl)


def _filter(L, w1, b1, f1, w2, b2, f2, w3, b3, f3, w4):
    ha = L // NB
    rc = SUBLANES * ha
    zf, zb, dl = (jnp.asarray(t) for t in _filter_tables(L))
    w1p = jnp.pad(w1, ((0, LANES - w1.shape[0]), (0, 0)))
    row = lambda v: jnp.tile(v.reshape(1, -1), (1, 2))
    zero = jnp.zeros_like(w2)
    diag2 = lambda w: jnp.concatenate([jnp.concatenate([w, zero], axis=1),
                                       jnp.concatenate([zero, w], axis=1)], axis=0)
    const = lambda j: (0, 0)
    full = lambda arr: pl.BlockSpec(arr.shape, const)
    args = [zf, zb, dl, w1p, row(b1), row(f1), diag2(w2), row(b2), row(f2), diag2(w3), row(b3), row(f3), w4, w4]
    specs = [full(a) for a in args]
    specs[0] = specs[1] = pl.BlockSpec((rc, LANES), lambda j: (j, 0))
    specs[12] = pl.BlockSpec((w4.shape[0], D_HY), lambda j: (0, 0))
    specs[13] = pl.BlockSpec((w4.shape[0], D_HY), lambda j: (0, 1))
    return pl.pallas_call(
        functools.partial(_filt_kernel, ha=ha),
        grid=(NB // SUBLANES,),
        in_specs=specs,
        out_specs=[pl.BlockSpec((None, SUBLANES, 2 * ha, D_HY), lambda j: (0, j, 0, 0)),
                   pl.BlockSpec((1, D_HY), const)],
        out_shape=[jax.ShapeDtypeStruct((1, NB, 2 * ha, D_HY), f32),
                   jax.ShapeDtypeStruct((1, D_HY), f32)],
        compiler_params=_params("arbitrary"),
        name="filt",
    )(*args)


@functools.lru_cache(maxsize=None)
def _dft_matrices(L):
    n = 2 * L
    na = n // NB
    ha = na // 2
    b = np.arange(NB)[:, None, None]
    p = np.arange(na)[None, :, None]
    a = np.arange(na)[None, None, :]
    ang = (2.0 * math.pi / n) * ((p * (NB * a + b)) % n)
    c, s = np.cos(ang), np.sin(ang)
    ch, sh = c[:, :, :ha], s[:, :, :ha]
    m_data = np.concatenate([np.concatenate([ch, sh], axis=2),
                             np.concatenate([-sh, ch], axis=2)], axis=1)
    m_real = np.concatenate([c, -s], axis=1)
    cht, sht = np.swapaxes(ch, 1, 2), np.swapaxes(sh, 1, 2)
    m_inv = np.concatenate([np.concatenate([cht, -sht], axis=2),
                            np.concatenate([sht, cht], axis=2)], axis=1)
    q = np.arange(NB)
    gang = (2.0 * math.pi / NB) * ((q[:, None] * q[None, :]) % NB)
    gc, gs = np.cos(gang), np.sin(gang)
    g_fwd = np.concatenate([np.concatenate([gc, gs], axis=1),
                            np.concatenate([-gs, gc], axis=1)], axis=0)
    g_inv = np.concatenate([np.concatenate([gc, -gs], axis=1),
                            np.concatenate([gs, gc], axis=1)], axis=0)
    return tuple(m.astype(np.float32) for m in (m_data, m_real, m_inv, g_fwd, g_inv))


def _first_stage(x_ref, m_ref, s_ref, s, na):
    for j in range(FFT_BA):
        res = jnp.dot(m_ref[j], x_ref[j].astype(bf16), preferred_element_type=f32)
        rows = pl.ds(s * FFT_BA + j, na, stride=FFT_PITCH)
        _put_cols(s_ref.at[0], rows, res[:na])
        _put_cols(s_ref.at[1], rows, res[na:])


def _slab(p):
    return pl.ds(pl.multiple_of(p * FFT_PITCH, SUBLANES), NB)


def _second_stage(s_ref, gf_ref, p):
    rows = _slab(p)
    y = jnp.concatenate([_get_cols(s_ref.at[0], rows), _get_cols(s_ref.at[1], rows)], axis=0)
    return jnp.dot(gf_ref[...], y.astype(bf16), preferred_element_type=f32)


def _fftk_kernel(k_ref, ma_ref, l1_ref, sk_ref, gf_ref, o_ref, s_ref, *, na, a_steps, scale):
    s = pl.program_id(1)

    @pl.when(s < a_steps)
    def _():
        _first_stage(k_ref, ma_ref, s_ref, s, na)

    @pl.when(s >= a_steps)
    def _():
        inv = scale / (l1_ref[...] + EPS)
        tap = scale * sk_ref[...]
        for j in range(FFT_PB):
            z = _second_stage(s_ref, gf_ref, (s - a_steps) * FFT_PB + j)
            o_ref[0, j] = z[:NB] * inv + tap
            o_ref[1, j] = z[NB:] * inv


def _fft_k(kt, l1, skip, m_real, g_fwd, scale):
    _, _, na, C = kt.shape
    a_steps, b_steps = NB // FFT_BA, na // FFT_PB
    last_a = a_steps - 1
    const = lambda c, s: (0, 0)
    return pl.pallas_call(
        functools.partial(_fftk_kernel, na=na, a_steps=a_steps, scale=scale),
        grid=(C // FFT_CB, a_steps + b_steps),
        in_specs=[pl.BlockSpec((None, FFT_BA, na, FFT_CB), lambda c, s: (0, jnp.minimum(s, last_a), 0, c)),
                  pl.BlockSpec((FFT_BA, 2 * na, na), lambda c, s: (jnp.minimum(s, last_a), 0, 0)),
                  pl.BlockSpec((1, FFT_CB), lambda c, s: (0, c)),
                  pl.BlockSpec((1, FFT_CB), lambda c, s: (0, c)),
                  pl.BlockSpec(g_fwd.shape, const)],
        out_specs=pl.BlockSpec((2, FFT_PB, NB, FFT_CB), lambda c, s: (0, jnp.clip(s - a_steps, 0, b_steps - 1), 0, c)),
        out_shape=jax.ShapeDtypeStruct((2, na, NB, C), f32),
        scratch_shapes=[pltpu.VMEM((2, FFT_CB // LANES, na * FFT_PITCH, LANES), f32)],
        compiler_params=_params("arbitrary", "arbitrary"),
        name="fft_k",
    )(kt, m_real, l1, skip, g_fwd)


def _fftconv_kernel(ua_ref, ma_ref, k_ref, gf_ref, gi_ref, mc_ref, x0_ref, o_ref, s_ref,
                    *, na, a_steps, b_steps):
    s = pl.program_id(2)

    @pl.when(s < a_steps)
    def _():
        _first_stage(ua_ref, ma_ref, s_ref, s, na)

    @pl.when((s >= a_steps) & (s < a_steps + b_steps))
    def _():
        for j in range(FFT_PB):
            p = (s - a_steps) * FFT_PB + j
            z = _second_stage(s_ref, gf_ref, p)
            zr, zi = z[:NB], z[NB:]
            kr, ki = k_ref[0, j], k_ref[1, j]
            f = jnp.concatenate([zr * kr - zi * ki, zr * ki + zi * kr], axis=0)
            v = jnp.dot(gi_ref[...], f.astype(bf16), preferred_element_type=f32)
            _put_cols(s_ref.at[0], _slab(p), v[:NB])
            _put_cols(s_ref.at[1], _slab(p), v[NB:])

    @pl.when(s >= a_steps + b_steps)
    def _():
        for j in range(FFT_BC):
            rows = pl.ds((s - (a_steps + b_steps)) * FFT_BC + j, na, stride=FFT_PITCH)
            v = jnp.concatenate([_get_cols(s_ref.at[0], rows), _get_cols(s_ref.at[1], rows)], axis=0)
            conv = jnp.dot(mc_ref[j], v.astype(bf16), preferred_element_type=f32)
            o_ref[j] = x0_ref[j] * conv


def _fft_conv(u, x0, kf, m_data, m_inv, g_fwd, g_inv):
    P, _, na, C = u.shape
    a_steps, b_steps, c_steps = NB // FFT_BA, na // FFT_PB, NB // FFT_BC
    last_a = a_steps - 1
    b_idx = lambda s: jnp.clip(s - a_steps, 0, b_steps - 1)
    c_idx = lambda s: jnp.clip(s - (a_steps + b_steps), 0, c_steps - 1)
    const = lambda c, r, s: (0, 0)
    c_spec = pl.BlockSpec((None, FFT_BC, na, FFT_CB), lambda c, r, s: (r, c_idx(s), 0, c))
    return pl.pallas_call(
        functools.partial(_fftconv_kernel, na=na, a_steps=a_steps, b_steps=b_steps),
        grid=(C // FFT_CB, P, a_steps + b_steps + c_steps),
        in_specs=[pl.BlockSpec((None, FFT_BA, na, FFT_CB), lambda c, r, s: (r, jnp.minimum(s, last_a), 0, c)),
                  pl.BlockSpec((FFT_BA, 2 * na, na), lambda c, r, s: (jnp.minimum(s, last_a), 0, 0)),
                  pl.BlockSpec((2, FFT_PB, NB, FFT_CB), lambda c, r, s: (0, b_idx(s), 0, c)),
                  pl.BlockSpec(g_fwd.shape, const),
                  pl.BlockSpec(g_inv.shape, const),
                  pl.BlockSpec((FFT_BC, na, 2 * na), lambda c, r, s: (c_idx(s), 0, 0)),
                  c_spec],
        out_specs=c_spec,
        out_shape=jax.ShapeDtypeStruct(u.shape, f32),
        scratch_shapes=[pltpu.VMEM((2, FFT_CB // LANES, na * FFT_PITCH, LANES), f32)],
        compiler_params=_params("arbitrary", "arbitrary", "arbitrary"),
        name="fft_conv",
    )(u, m_data, kf, g_fwd, g_inv, m_inv, x0)


def _carry_scan(at, bt, reverse):
    n = at.shape[0]
    row = lax.broadcasted_iota(jnp.int32, at.shape, 0)
    s = 1
    while s < n:
        keep = (row < n - s) if reverse else (row >= s)
        shift = n - s if reverse else s
        ash = jnp.where(keep, pltpu.roll(at, shift, 0), 1.0)
        bsh = jnp.where(keep, pltpu.roll(bt, shift, 0), 0.0)
        bt = at * bsh + bt
        at = at * ash
        s *= 2
    if reverse:
        return jnp.where(row < n - 1, pltpu.roll(bt, n - 1, 0), 0.0)
    return jnp.where(row >= 1, pltpu.roll(bt, 1, 0), 0.0)


def _lru_kernel(xb_ref, wg_ref, bg_ref, lam_ref, o_ref, hf_ref, af_ref, hb_ref, ab_ref, *, ha):
    bt = SUBLANES
    nc = NB // bt
    lam = lam_ref[...]
    half_c = (-0.5 * LRU_C) * (jnp.maximum(-lam, 0.0) + jnp.log1p(jnp.exp(-jnp.abs(lam))))
    rowi = lax.broadcasted_iota(jnp.int32, (bt * ha, 1), 0)

    def gates(d, k, first):
        b0 = pl.multiple_of(k * bt, bt)
        xc = xb_ref[pl.ds(b0, bt)].reshape(bt * ha, LANES)
        t = jnp.tanh(jnp.dot(xc.astype(bf16), wg_ref[d], preferred_element_type=f32) + bg_ref[d:d + 1, :])
        hc = half_c[d:d + 1, :]
        log_a = hc * t[:, :LANES] + hc
        a = jnp.exp(log_a)
        m2 = jnp.tanh(log_a) * (-1.0 - a * a)
        mult = jnp.where(m2 > 0.0, m2 * lax.rsqrt(m2), 0.0)
        if first:
            mult = jnp.where(rowi == (bt * ha - 1 if d else 0), 1.0, mult)
        hx = 0.5 * xc
        bv = mult * (t[:, LANES:] * hx + hx)
        return b0, a.reshape(bt, ha, LANES), bv.reshape(bt, ha, LANES)

    def step(k, carry, first=False):
        hf, af, hb, ab = carry
        b0, a, bv = gates(0, k, first)
        for j in range(bt):
            hf = a[j] * hf + bv[j]
            af = a[j] * af
            hf_ref[b0 + j] = hf
            af_ref[b0 + j] = af
        b0, a, bv = gates(1, nc - 1 - k, first)
        for j in reversed(range(bt)):
            hb = a[j] * hb + bv[j]
            ab = a[j] * ab
            hb_ref[b0 + j] = hb
            ab_ref[b0 + j] = ab
        return hf, af, hb, ab

    zero = jnp.zeros((ha, LANES), f32)
    one = jnp.ones((ha, LANES), f32)
    carry = step(0, (zero, one, zero, one), first=True)
    hf, af, hb, ab = lax.fori_loop(1, nc, step, carry)
    cf = _carry_scan(af, hf, reverse=False)
    cb = _carry_scan(ab, hb, reverse=True)

    def finish(k, c):
        b0 = pl.multiple_of(k * bt, bt)
        sl = pl.ds(b0, bt)
        o_ref[sl] = (hf_ref[sl] + af_ref[sl] * cf) + (hb_ref[sl] + ab_ref[sl] * cb)
        return c

    lax.fori_loop(0, nc, finish, 0)


def _lru_gate_weights(wa, wx):
    def blockdiag(w):
        w = w.reshape(2, -1, 2, HEAD, HEAD)
        z = jnp.zeros_like(w[:, :, 0])
        top = jnp.concatenate([w[:, :, 0], z], axis=-1)
        bot = jnp.concatenate([z, w[:, :, 1]], axis=-1)
        return jnp.concatenate([top, bot], axis=-2)
    return jnp.concatenate([blockdiag(wa), blockdiag(wx)], axis=-1).astype(bf16)


def _lru(xb, wa, ba, wx, bx, lam):
    P, _, _, ha, C = xb.shape
    nblk = C // LANES
    wg = _lru_gate_weights(0.5 * wa, 0.5 * wx)
    bg = 0.5 * jnp.concatenate([ba.reshape(2, nblk, 1, LANES), bx.reshape(2, nblk, 1, LANES)], axis=-1)
    ba_spec = pl.BlockSpec((None, NB, None, ha, LANES), lambda b, c: (b // 2, 0, b % 2, 0, c))
    return pl.pallas_call(
        functools.partial(_lru_kernel, ha=ha),
        grid=(2 * P, nblk),
        in_specs=[ba_spec,
                  pl.BlockSpec((2, None, LANES, 2 * LANES), lambda b, c: (0, c, 0, 0)),
                  pl.BlockSpec((2, None, None, 2 * LANES), lambda b, c: (0, c, 0, 0)),
                  pl.BlockSpec((2, LANES), lambda b, c: (0, c))],
        out_specs=ba_spec,
        out_shape=jax.ShapeDtypeStruct(xb.shape, f32),
        scratch_shapes=[pltpu.VMEM((NB, ha, LANES), f32)] * 4,
        compiler_params=_params("parallel", "arbitrary"),
        name="lru",
    )(xb, wg, bg, lam)


def _out_kernel(yh_ref, hg_ref, yl_ref, lg_ref, x_ref, hog_ref, log_ref, wo_ref, fg_ref, o_ref, ys_ref):
    flat = lambda ref: ref[...].reshape(TILE, -1)
    ycat = jnp.concatenate([_rms(flat(yh_ref), hog_ref[...]) * hg_ref[...],
                            _rms(flat(yl_ref), log_ref[...]) * lg_ref[...]], axis=-1)
    y = jnp.dot(ycat.astype(bf16), wo_ref[...], preferred_element_type=f32)
    _put_cols(ys_ref, pl.ds(0, TILE), y)
    for a in range(SUBLANES):
        rows = slice(a * NB, (a + 1) * NB)
        ya = _get_cols(ys_ref, pl.ds(a, NB, stride=SUBLANES))
        o_ref[rows, :] = _rms(x_ref[rows, :] + ya, fg_ref[...])


def _out(yh, hg, yl, lg, x, hog, log_g, w_out, fg):
    B, L, D = x.shape
    const = lambda b, i: (0, 0)
    nat = pl.BlockSpec((None, TILE, D), lambda b, i: (b, i, 0))
    return pl.pallas_call(
        _out_kernel,
        grid=(B, L // TILE),
        in_specs=[_ba_spec(D_HY), _gate_spec(D_HY), _ba_spec(D_LRU), _gate_spec(D_LRU), nat,
                  pl.BlockSpec((1, D_HY), const), pl.BlockSpec((1, D_LRU), const),
                  pl.BlockSpec(w_out.shape, const), pl.BlockSpec((1, D), const)],
        out_specs=nat,
        out_shape=jax.ShapeDtypeStruct((B, L, D), f32),
        scratch_shapes=[pltpu.VMEM((D // LANES, TILE, LANES), f32)],
        compiler_params=_params("parallel", "arbitrary"),
        name="out",
    )(yh, hg, yl, lg, x, hog, log_g, w_out, fg)


def kernel(x, norm_g, w_in, hy_conv_w, hy_conv_b, flt_w1, flt_b1, flt_f1, flt_w2, flt_b2, flt_f2,
           flt_w3, flt_b3, flt_f3, flt_w4, hy_skip, lru_conv_w, lru_conv_b, lru_wa, lru_ba, lru_wx,
           lru_bx, lru_lam, hy_out_g, lru_out_g, w_out, final_g):
    B, L, D = x.shape
    assert norm_g.shape[0] == 1, "one layer"
    assert B % 2 == 0 and L % TILE == 0
    ha = L // NB
    na = 2 * ha
    row = lambda v: v.reshape(1, -1)

    u, x0, hg, xb, lg = _inproj(x, row(norm_g[0]), w_in[0].astype(bf16), hy_conv_w[0], row(hy_conv_b[0]),
                                lru_conv_w[0], row(lru_conv_b[0]))
    pair = lambda t: t.reshape(B // 2, NB, na, D_HY)

    m_data, m_real, m_inv, g_fwd, g_inv = (jnp.asarray(m).astype(bf16) for m in _dft_matrices(L))
    kt, l1 = _filter(L, flt_w1[0], flt_b1[0], flt_f1[0], flt_w2[0], flt_b2[0], flt_f2[0],
                     flt_w3[0], flt_b3[0], flt_f3[0], flt_w4[0])
    kf = _fft_k(kt, l1, row(hy_skip[0]), m_real, g_fwd, scale=1.0 / (2 * L))
    yh = _fft_conv(pair(u), pair(x0), kf, m_data, m_inv, g_fwd, g_inv).reshape(u.shape)

    yl = _lru(xb, lru_wa[0], lru_ba[0], lru_wx[0], lru_bx[0], lru_lam[0])
    return _out(yh, hg, yl, lg, x, row(hy_out_g[0]), row(lru_out_g[0]), w_out[0].astype(bf16), row(final_g))
```

```python
import functools
import math

import jax
import jax.numpy as jnp
import numpy as np
from jax import lax
from jax.experimental import pallas as pl
from jax.experimental.pallas import tpu as pltpu

f32 = jnp.float32
bf16 = jnp.bfloat16

D_HY = 768
D_LRU = 768
HEAD = 64
LANES = 128
SUBLANES = 8
NB = 128
TILE = SUBLANES * NB
HALO = SUBLANES
FFT_CB = 256
FFT_BA = 16
FFT_PB = 8
FFT_BC = 16
FFT_PITCH = NB + SUBLANES
FFTK_BA = 32
FFTK_PB = 16
FILTER_BANDS = 16
FILTER_EMB = 2 * FILTER_BANDS + 1
MASK_COL = FILTER_EMB
FILTER_TARGET = 1e-2
MIN_DECAY = math.log(FILTER_TARGET) / 0.3
MAX_DECAY = math.log(FILTER_TARGET) / 1.5
LRU_C = 8.0
EPS = 1e-6
VMEM_LIMIT = 60 * 1024 * 1024


def _params(*sem):
    return pltpu.CompilerParams(dimension_semantics=sem, vmem_limit_bytes=VMEM_LIMIT)


def _rms(y, g):
    return y * lax.rsqrt(jnp.mean(y * y, axis=-1, keepdims=True) + EPS) * g


def _put_cols(ref, rows, val):
    for h in range(ref.shape[0]):
        ref[h, rows, :] = val[:, h * LANES:(h + 1) * LANES]


def _get_cols(ref, rows):
    return jnp.concatenate([ref[h, rows, :] for h in range(ref.shape[0])], axis=-1)


def _sigmoid(x):
    return 0.5 * jnp.tanh(0.5 * x) + 0.5


def _inproj_kernel(x_ref, xp_ref, xn_ref, g_ref, w_ref, hcw_ref, hcb_ref, lcw_ref, lcb_ref,
                   u_ref, x0_ref, hg_ref, xb_ref, lg_ref, xs_ref, *, n_tiles):
    i = pl.program_id(1)
    g = g_ref[...]
    for a in range(SUBLANES):
        _put_cols(xs_ref, pl.ds(a, NB, stride=SUBLANES), _rms(x_ref[a * NB:(a + 1) * NB, :], g))
    _put_cols(xs_ref, pl.ds(TILE, HALO), jnp.where(i > 0, _rms(xp_ref[...], g), 0.0))
    _put_cols(xs_ref, pl.ds(TILE + HALO, HALO), jnp.where(i < n_tiles - 1, _rms(xn_ref[...], g), 0.0))
    xn = _get_cols(xs_ref, pl.ds(0, TILE + 2 * HALO)).astype(bf16)
    sub = lax.broadcasted_iota(jnp.int32, (SUBLANES, D_HY), 0)

    def proj(c0):
        p = jnp.dot(xn, w_ref[:, c0:c0 + D_HY], preferred_element_type=f32)
        return p.reshape(NB + 2, SUBLANES, D_HY)

    def edge(p3, s):
        if s < 0:
            return jnp.where(sub == 0, p3[NB][SUBLANES + s:SUBLANES + s + 1],
                             pltpu.roll(p3[NB + s], 1, 0))
        return jnp.where(sub == SUBLANES - 1, p3[NB + 1][s - NB:s - NB + 1],
                         pltpu.roll(p3[s - NB], SUBLANES - 1, 0))

    def conv(p3, cw_ref, cb_ref, c0, offsets):
        lo, hi = max(0, -min(offsets)), NB - max(offsets)

        def acc(get):
            y = cb_ref[:, c0:c0 + D_HY]
            for k, o in enumerate(offsets):
                y = y + get(o) * cw_ref[k:k + 1, c0:c0 + D_HY]
            return y

        inner = acc(lambda o: p3[lo + o:hi + o])
        edges = {b: acc(lambda o, b=b: p3[b + o] if 0 <= b + o < NB else edge(p3, b + o))
                 for b in list(range(lo)) + list(range(hi, NB))}
        return lo, hi, inner, edges

    def store(ref, conv_out, other=None):
        lo, hi, inner, edges = conv_out
        if other is not None:
            inner = inner * other[2]
            edges = {b: edges[b] * other[3][b] for b in edges}
        ref[lo:hi] = inner
        for b, y in edges.items():
            ref[b] = y

    hy = (-1, 0, 1)
    store(u_ref, conv(proj(0), hcw_ref, hcb_ref, 0, hy), conv(proj(2 * D_HY), hcw_ref, hcb_ref, 2 * D_HY, hy))
    store(x0_ref, conv(proj(D_HY), hcw_ref, hcb_ref, D_HY, hy))
    store(xb_ref, conv(proj(4 * D_HY), lcw_ref, lcb_ref, 0, (-1, 0, 1, 2)))
    xm = xn[:TILE]
    hg = jnp.dot(xm, w_ref[:, 3 * D_HY:4 * D_HY], preferred_element_type=f32)
    hg_ref[...] = (hg * _sigmoid(hg)).astype(bf16)
    lg = jnp.dot(xm, w_ref[:, 4 * D_HY + D_LRU:], preferred_element_type=f32)
    lg_ref[...] = (lg * _sigmoid(lg)).astype(bf16)


def _ba_spec(c):
    return pl.BlockSpec((None, NB, None, SUBLANES, c), lambda b, i: (b // 2, 0, b % 2, i, 0))


def _gate_spec(c):
    return pl.BlockSpec((None, None, TILE, c), lambda b, i: (b, i, 0, 0))


def _inproj(x, norm_g, w_in, hcw, hcb, lcw, lcb):
    B, L, D = x.shape
    n_tiles = L // TILE
    ha = L // NB
    hb = TILE // HALO
    n_hb = L // HALO
    const = lambda b, i: (0, 0)
    ba_shape = jax.ShapeDtypeStruct((B // 2, NB, 2, ha, D_HY), f32)
    gate_shape = jax.ShapeDtypeStruct((B, n_tiles, TILE, D_HY), bf16)
    return pl.pallas_call(
        functools.partial(_inproj_kernel, n_tiles=n_tiles),
        grid=(B, n_tiles),
        in_specs=[
            pl.BlockSpec((None, TILE, D), lambda b, i: (b, i, 0)),
            pl.BlockSpec((None, HALO, D), lambda b, i: (b, jnp.maximum(i * hb - 1, 0), 0)),
            pl.BlockSpec((None, HALO, D), lambda b, i: (b, jnp.minimum((i + 1) * hb, n_hb - 1), 0)),
            pl.BlockSpec((1, D), const),
            pl.BlockSpec(w_in.shape, const, pipeline_mode=pl.Buffered(1)),
            pl.BlockSpec(hcw.shape, const),
            pl.BlockSpec(hcb.shape, const),
            pl.BlockSpec(lcw.shape, const),
            pl.BlockSpec(lcb.shape, const),
        ],
        out_specs=[_ba_spec(D_HY), _ba_spec(D_HY), _gate_spec(D_HY), _ba_spec(D_LRU), _gate_spec(D_LRU)],
        out_shape=[ba_shape, ba_shape, gate_shape, ba_shape, gate_shape],
        scratch_shapes=[pltpu.VMEM((D // LANES, TILE + 2 * HALO, LANES), f32)],
        compiler_params=_params("parallel", "arbitrary"),
        name="inproj",
    )(x, x, x, norm_g, w_in, hcw, hcb, lcw, lcb)


def _filt_kernel(zf_ref, zb_ref, dl_ref, w1_ref, b1_ref, f1_ref, w2_ref, b2_ref, f2_ref,
                 w3_ref, b3_ref, f3_ref, w4f_ref, w4b_ref, o_ref, s_ref, *, ha):
    dot = functools.partial(jnp.dot, precision=lax.Precision.HIGHEST, preferred_element_type=f32)
    dl = dl_ref[...]
    zf = zf_ref[...]
    zb = zb_ref[...]
    h = jnp.concatenate([dot(zf, w1_ref[...]), dot(zb, w1_ref[...])], axis=-1)
    h = jnp.sin(f1_ref[...] * (h + b1_ref[...]))
    h = jnp.sin(f2_ref[...] * (dot(h, w2_ref[...]) + b2_ref[...]))
    h = jnp.sin(f3_ref[...] * (dot(h, w3_ref[...]) + b3_ref[...]))
    nh = h.shape[-1] // 2
    hf = dot(h[:, :nh], w4f_ref[...]) * jnp.exp(-zf[:, 0:1] * dl)
    hb = dot(h[:, nh:], w4b_ref[...]) * jnp.exp(-zb[:, 0:1] * dl) * zb[:, MASK_COL:MASK_COL + 1]
    o_ref[:, :ha, :] = hf.reshape(SUBLANES, ha, D_HY)
    o_ref[:, ha:, :] = hb.reshape(SUBLANES, ha, D_HY)

    @pl.when(pl.program_id(0) == 0)
    def _():
        s_ref[...] = jnp.zeros_like(s_ref)

    s_ref[...] += jnp.sum(jnp.abs(hf), axis=0, keepdims=True) + jnp.sum(jnp.abs(hb), axis=0, keepdims=True)


@functools.lru_cache(maxsize=None)
def _filter_tables(L):
    ha = L // NB
    t = np.linspace(0.0, 1.0, L)
    w = (2.0 * math.pi / L) * np.arange(L)
    f = np.linspace(1e-4, FILTER_BANDS - 1, FILTER_BANDS)[None, :]

    def features(lag, mask):
        tl, wl = t[lag][:, None], w[lag][:, None]
        pad = np.zeros((lag.shape[0], LANES - FILTER_EMB - 1))
        z = np.concatenate([tl, np.cos(wl * f), -np.sin(wl * f), mask[:, None], pad], axis=-1)
        return z.astype(np.float32)

    b = np.arange(NB)[:, None]
    a = np.arange(ha)[None, :]
    lag_f = (NB * a + b).reshape(-1)
    lag_b = (L - NB * a - b).reshape(-1)
    valid = lag_b < L
    dl = np.abs(np.linspace(MIN_DECAY, MAX_DECAY, D_HY))[None, :].astype(np.float32)
    return (features(lag_f, np.zeros(lag_f.shape)),
            features(np.where(valid, lag_b, 0), valid.astype(np.float64)), dl)


def _filter(L, w1, b1, f1, w2, b2, f2, w3, b3, f3, w4):
    ha = L // NB
    rc = SUBLANES * ha
    zf, zb, dl = (jnp.asarray(t) for t in _filter_tables(L))
    w1p = jnp.pad(w1, ((0, LANES - w1.shape[0]), (0, 0)))
    row = lambda v: jnp.tile(v.reshape(1, -1), (1, 2))
    zero = jnp.zeros_like(w2)
    diag2 = lambda w: jnp.concatenate([jnp.concatenate([w, zero], axis=1),
                                       jnp.concatenate([zero, w], axis=1)], axis=0)
    const = lambda j: (0, 0)
    full = lambda arr: pl.BlockSpec(arr.shape, const)
    args = [zf, zb, dl, w1p, row(b1), row(f1), diag2(w2), row(b2), row(f2), diag2(w3), row(b3), row(f3), w4, w4]
    specs = [full(a) for a in args]
    specs[0] = specs[1] = pl.BlockSpec((rc, LANES), lambda j: (j, 0))
    specs[12] = pl.BlockSpec((w4.shape[0], D_HY), lambda j: (0, 0))
    specs[13] = pl.BlockSpec((w4.shape[0], D_HY), lambda j: (0, 1))
    return pl.pallas_call(
        functools.partial(_filt_kernel, ha=ha),
        grid=(NB // SUBLANES,),
        in_specs=specs,
        out_specs=[pl.BlockSpec((None, SUBLANES, 2 * ha, D_HY), lambda j: (0, j, 0, 0)),
                   pl.BlockSpec((1, D_HY), const)],
        out_shape=[jax.ShapeDtypeStruct((1, NB, 2 * ha, D_HY), f32),
                   jax.ShapeDtypeStruct((1, D_HY), f32)],
        compiler_params=_params("arbitrary"),
        name="filt",
    )(*args)


@functools.lru_cache(maxsize=None)
def _dft_matrices(L):
    n = 2 * L
    na = n // NB
    ha = na // 2
    b = np.arange(NB)[:, None, None]
    p = np.arange(na)[None, :, None]
    a = np.arange(na)[None, None, :]
    ang = (2.0 * math.pi / n) * ((p * (NB * a + b)) % n)
    c, s = np.cos(ang), np.sin(ang)
    ch, sh = c[:, :, :ha], s[:, :, :ha]
    m_data = np.concatenate([np.concatenate([ch, sh], axis=2),
                             np.concatenate([-sh, ch], axis=2)], axis=1)
    m_real = np.concatenate([c, -s], axis=1)
    cht, sht = np.swapaxes(ch, 1, 2), np.swapaxes(sh, 1, 2)
    m_inv = np.concatenate([np.concatenate([cht, -sht], axis=2),
                            np.concatenate([sht, cht], axis=2)], axis=1)
    q = np.arange(NB)
    gang = (2.0 * math.pi / NB) * ((q[:, None] * q[None, :]) % NB)
    gc, gs = np.cos(gang), np.sin(gang)
    g_fwd = np.concatenate([np.concatenate([gc, gs], axis=1),
                            np.concatenate([-gs, gc], axis=1)], axis=0)
    g_inv = np.concatenate([np.concatenate([gc, -gs], axis=1),
                            np.concatenate([gs, gc], axis=1)], axis=0)
    return tuple(m.astype(np.float32) for m in (m_data, m_real, m_inv, g_fwd, g_inv))


def _first_stage(x_ref, m_ref, s_ref, s, na):
    nb = x_ref.shape[0]
    for j in range(nb):
        res = jnp.dot(m_ref[j], x_ref[j].astype(bf16), preferred_element_type=f32)
        rows = pl.ds(s * nb + j, na, stride=FFT_PITCH)
        _put_cols(s_ref.at[0], rows, res[:na])
        _put_cols(s_ref.at[1], rows, res[na:])


def _slab(p):
    return pl.ds(pl.multiple_of(p * FFT_PITCH, SUBLANES), NB)


def _second_stage(s_ref, gf_ref, p):
    rows = _slab(p)
    y = jnp.concatenate([_get_cols(s_ref.at[0], rows), _get_cols(s_ref.at[1], rows)], axis=0)
    return jnp.dot(gf_ref[...], y.astype(bf16), preferred_element_type=f32)


def _fftk_kernel(k_ref, ma_ref, l1_ref, sk_ref, gf_ref, o_ref, s_ref, *, na, a_steps, scale):
    s = pl.program_id(1)

    @pl.when(s < a_steps)
    def _():
        _first_stage(k_ref, ma_ref, s_ref, s, na)

    @pl.when(s >= a_steps)
    def _():
        inv = scale / (l1_ref[...] + EPS)
        tap = scale * sk_ref[...]
        for j in range(FFTK_PB):
            z = _second_stage(s_ref, gf_ref, (s - a_steps) * FFTK_PB + j)
            o_ref[0, j] = z[:NB] * inv + tap
            o_ref[1, j] = z[NB:] * inv


def _fft_k(kt, l1, skip, m_real, g_fwd, scale):
    _, _, na, C = kt.shape
    a_steps, b_steps = NB // FFTK_BA, na // FFTK_PB
    last_a = a_steps - 1
    const = lambda c, s: (0, 0)
    return pl.pallas_call(
        functools.partial(_fftk_kernel, na=na, a_steps=a_steps, scale=scale),
        grid=(C // FFT_CB, a_steps + b_steps),
        in_specs=[pl.BlockSpec((None, FFTK_BA, na, FFT_CB), lambda c, s: (0, jnp.minimum(s, last_a), 0, c)),
                  pl.BlockSpec((FFTK_BA, 2 * na, na), lambda c, s: (jnp.minimum(s, last_a), 0, 0)),
                  pl.BlockSpec((1, FFT_CB), lambda c, s: (0, c)),
                  pl.BlockSpec((1, FFT_CB), lambda c, s: (0, c)),
                  pl.BlockSpec(g_fwd.shape, const)],
        out_specs=pl.BlockSpec((2, FFTK_PB, NB, FFT_CB), lambda c, s: (0, jnp.clip(s - a_steps, 0, b_steps - 1), 0, c)),
        out_shape=jax.ShapeDtypeStruct((2, na, NB, C), f32),
        scratch_shapes=[pltpu.VMEM((2, FFT_CB // LANES, na * FFT_PITCH, LANES), f32)],
        compiler_params=_params("arbitrary", "arbitrary"),
        name="fft_k",
    )(kt, m_real, l1, skip, g_fwd)


def _fftconv_kernel(ua_ref, ma_ref, k_ref, gf_ref, gi_ref, mc_ref, x0_ref, o_ref, s_ref,
                    *, na, a_steps, b_steps):
    s = pl.program_id(2)

    @pl.when(s < a_steps)
    def _():
        _first_stage(ua_ref, ma_ref, s_ref, s, na)

    @pl.when((s >= a_steps) & (s < a_steps + b_steps))
    def _():
        for j in range(FFT_PB):
            p = (s - a_steps) * FFT_PB + j
            z = _second_stage(s_ref, gf_ref, p)
            zr, zi = z[:NB], z[NB:]
            kr, ki = k_ref[0, j], k_ref[1, j]
            f = jnp.concatenate([zr * kr - zi * ki, zr * ki + zi * kr], axis=0)
            v = jnp.dot(gi_ref[...], f.astype(bf16), preferred_element_type=f32)
            _put_cols(s_ref.at[0], _slab(p), v[:NB])
            _put_cols(s_ref.at[1], _slab(p), v[NB:])

    @pl.when(s >= a_steps + b_steps)
    def _():
        def last_stage(j):
            rows = pl.ds((s - (a_steps + b_steps)) * FFT_BC + j, na, stride=FFT_PITCH)
            v = jnp.concatenate([_get_cols(s_ref.at[0], rows), _get_cols(s_ref.at[1], rows)], axis=0)
            conv = jnp.dot(mc_ref[j], v.astype(bf16), preferred_element_type=f32)
            return x0_ref[j] * conv

        ha = na // 2
        for jj in range(FFT_BC // 2):
            y0, y1 = last_stage(2 * jj), last_stage(2 * jj + 1)
            for half in range(2):
                for t in range(ha // SUBLANES):
                    r0 = half * ha + t * SUBLANES
                    y = jnp.concatenate([y0[r0:r0 + SUBLANES], y1[r0:r0 + SUBLANES]], axis=0)
                    o_ref[half, t, 2 * SUBLANES * jj:2 * SUBLANES * (jj + 1), :] = y.astype(bf16)


def _fft_conv(u, x0, kf, m_data, m_inv, g_fwd, g_inv):
    P, _, na, C = u.shape
    a_steps, b_steps, c_steps = NB // FFT_BA, na // FFT_PB, NB // FFT_BC
    n_tiles = na // (2 * SUBLANES)
    last_a = a_steps - 1
    b_idx = lambda s: jnp.clip(s - a_steps, 0, b_steps - 1)
    c_idx = lambda s: jnp.clip(s - (a_steps + b_steps), 0, c_steps - 1)
    const = lambda c, r, s: (0, 0)
    c_spec = pl.BlockSpec((None, FFT_BC, na, FFT_CB), lambda c, r, s: (r, c_idx(s), 0, c))
    return pl.pallas_call(
        functools.partial(_fftconv_kernel, na=na, a_steps=a_steps, b_steps=b_steps),
        grid=(C // FFT_CB, P, a_steps + b_steps + c_steps),
        in_specs=[pl.BlockSpec((None, FFT_BA, na, FFT_CB), lambda c, r, s: (r, jnp.minimum(s, last_a), 0, c)),
                  pl.BlockSpec((FFT_BA, 2 * na, na), lambda c, r, s: (jnp.minimum(s, last_a), 0, 0)),
                  pl.BlockSpec((2, FFT_PB, NB, FFT_CB), lambda c, r, s: (0, b_idx(s), 0, c)),
                  pl.BlockSpec(g_fwd.shape, const),
                  pl.BlockSpec(g_inv.shape, const),
                  pl.BlockSpec((FFT_BC, na, 2 * na), lambda c, r, s: (c_idx(s), 0, 0)),
                  c_spec],
        out_specs=pl.BlockSpec((2, n_tiles, FFT_BC * SUBLANES, FFT_CB), lambda c, r, s: (r, 0, c_idx(s), c)),
        out_shape=jax.ShapeDtypeStruct((2 * P, n_tiles, TILE, C), bf16),
        scratch_shapes=[pltpu.VMEM((2, FFT_CB // LANES, na * FFT_PITCH, LANES), f32)],
        compiler_params=_params("arbitrary", "arbitrary", "arbitrary"),
        name="fft_conv",
    )(u, m_data, kf, g_fwd, g_inv, m_inv, x0)


def _carry_scan(at, bt, reverse):
    n = at.shape[0]
    row = lax.broadcasted_iota(jnp.int32, at.shape, 0)
    s = 1
    while s < n:
        keep = (row < n - s) if reverse else (row >= s)
        shift = n - s if reverse else s
        ash = jnp.where(keep, pltpu.roll(at, shift, 0), 1.0)
        bsh = jnp.where(keep, pltpu.roll(bt, shift, 0), 0.0)
        bt = at * bsh + bt
        at = at * ash
        s *= 2
    if reverse:
        return jnp.where(row < n - 1, pltpu.roll(bt, n - 1, 0), 0.0)
    return jnp.where(row >= 1, pltpu.roll(bt, 1, 0), 0.0)


def _lru_kernel(xb_ref, wg_ref, bg_ref, lam_ref, o_ref, hf_ref, af_ref, hb_ref, ab_ref, *, ha):
    bt = SUBLANES
    nc = NB // bt
    lam = lam_ref[...]
    half_c = (-0.5 * LRU_C) * (jnp.maximum(-lam, 0.0) + jnp.log1p(jnp.exp(-jnp.abs(lam))))
    rowi = lax.broadcasted_iota(jnp.int32, (bt * ha, 1), 0)

    def gates(d, k, first):
        b0 = pl.multiple_of(k * bt, bt)
        xc = xb_ref[pl.ds(b0, bt)].reshape(bt * ha, LANES)
        t = jnp.tanh(jnp.dot(xc.astype(bf16), wg_ref[d], preferred_element_type=f32) + bg_ref[d:d + 1, :])
        hc = half_c[d:d + 1, :]
        log_a = hc * t[:, :LANES] + hc
        a = jnp.exp(log_a)
        m2 = jnp.tanh(log_a) * (-1.0 - a * a)
        mult = jnp.where(m2 > 0.0, m2 * lax.rsqrt(m2), 0.0)
        if first:
            mult = jnp.where(rowi == (bt * ha - 1 if d else 0), 1.0, mult)
        hx = 0.5 * xc
        bv = mult * (t[:, LANES:] * hx + hx)
        return b0, a.reshape(bt, ha, LANES), bv.reshape(bt, ha, LANES)

    def step(k, carry, first=False):
        hf, af, hb, ab = carry
        b0, a, bv = gates(0, k, first)
        for j in range(bt):
            hf = a[j] * hf + bv[j]
            af = a[j] * af
            hf_ref[b0 + j] = hf
            af_ref[b0 + j] = af
        b0, a, bv = gates(1, nc - 1 - k, first)
        for j in reversed(range(bt)):
            hb = a[j] * hb + bv[j]
            ab = a[j] * ab
            hb_ref[b0 + j] = hb
            ab_ref[b0 + j] = ab
        return hf, af, hb, ab

    zero = jnp.zeros((ha, LANES), f32)
    one = jnp.ones((ha, LANES), f32)
    carry = step(0, (zero, one, zero, one), first=True)
    hf, af, hb, ab = lax.fori_loop(1, nc, step, carry)
    cf = _carry_scan(af, hf, reverse=False)
    cb = _carry_scan(ab, hb, reverse=True)

    def finish(k, c):
        b0 = pl.multiple_of(k * bt, bt)
        sl = pl.ds(b0, bt)
        h = (hf_ref[sl] + af_ref[sl] * cf) + (hb_ref[sl] + ab_ref[sl] * cb)
        r0 = pl.multiple_of(k * (bt * SUBLANES), bt * SUBLANES)
        for t in range(ha // SUBLANES):
            tile = h[:, t * SUBLANES:(t + 1) * SUBLANES, :].reshape(bt * SUBLANES, LANES)
            o_ref[t, pl.ds(r0, bt * SUBLANES), :] = tile.astype(bf16)
        return c

    lax.fori_loop(0, nc, finish, 0)


def _lru_gate_weights(wa, wx):
    def blockdiag(w):
        w = w.reshape(2, -1, 2, HEAD, HEAD)
        z = jnp.zeros_like(w[:, :, 0])
        top = jnp.concatenate([w[:, :, 0], z], axis=-1)
        bot = jnp.concatenate([z, w[:, :, 1]], axis=-1)
        return jnp.concatenate([top, bot], axis=-2)
    return jnp.concatenate([blockdiag(wa), blockdiag(wx)], axis=-1).astype(bf16)


def _lru(xb, wa, ba, wx, bx, lam):
    P, _, _, ha, C = xb.shape
    nblk = C // LANES
    wg = _lru_gate_weights(0.5 * wa, 0.5 * wx)
    bg = 0.5 * jnp.concatenate([ba.reshape(2, nblk, 1, LANES), bx.reshape(2, nblk, 1, LANES)], axis=-1)
    ba_spec = pl.BlockSpec((None, NB, None, ha, LANES), lambda b, c: (b // 2, 0, b % 2, 0, c))
    return pl.pallas_call(
        functools.partial(_lru_kernel, ha=ha),
        grid=(2 * P, nblk),
        in_specs=[ba_spec,
                  pl.BlockSpec((2, None, LANES, 2 * LANES), lambda b, c: (0, c, 0, 0)),
                  pl.BlockSpec((2, None, None, 2 * LANES), lambda b, c: (0, c, 0, 0)),
                  pl.BlockSpec((2, LANES), lambda b, c: (0, c))],
        out_specs=pl.BlockSpec((None, ha // SUBLANES, TILE, LANES), lambda b, c: (b, 0, 0, c)),
        out_shape=jax.ShapeDtypeStruct((2 * P, ha // SUBLANES, TILE, C), bf16),
        scratch_shapes=[pltpu.VMEM((NB, ha, LANES), f32)] * 4,
        compiler_params=_params("parallel", "arbitrary"),
        name="lru",
    )(xb, wg, bg, lam)


def _out_kernel(yh_ref, hg_ref, yl_ref, lg_ref, x_ref, hog_ref, log_ref, wo_ref, fg_ref, o_ref, ys_ref):
    ycat = jnp.concatenate([_rms(yh_ref[...].astype(f32), hog_ref[...]) * hg_ref[...],
                            _rms(yl_ref[...].astype(f32), log_ref[...]) * lg_ref[...]], axis=-1)
    y = jnp.dot(ycat.astype(bf16), wo_ref[...], preferred_element_type=f32)
    _put_cols(ys_ref, pl.ds(0, TILE), y)
    for a in range(SUBLANES):
        rows = slice(a * NB, (a + 1) * NB)
        ya = _get_cols(ys_ref, pl.ds(a, NB, stride=SUBLANES))
        o_ref[rows, :] = _rms(x_ref[rows, :] + ya, fg_ref[...])


def _out(yh, hg, yl, lg, x, hog, log_g, w_out, fg):
    B, L, D = x.shape
    const = lambda b, i: (0, 0)
    nat = pl.BlockSpec((None, TILE, D), lambda b, i: (b, i, 0))
    return pl.pallas_call(
        _out_kernel,
        grid=(B, L // TILE),
        in_specs=[_gate_spec(D_HY), _gate_spec(D_HY), _gate_spec(D_LRU), _gate_spec(D_LRU), nat,
                  pl.BlockSpec((1, D_HY), const), pl.BlockSpec((1, D_LRU), const),
                  pl.BlockSpec(w_out.shape, const), pl.BlockSpec((1, D), const)],
        out_specs=nat,
        out_shape=jax.ShapeDtypeStruct((B, L, D), f32),
        scratch_shapes=[pltpu.VMEM((D // LANES, TILE, LANES), f32)],
        compiler_params=_params("parallel", "arbitrary"),
        name="out",
    )(yh, hg, yl, lg, x, hog, log_g, w_out, fg)


def kernel(x, norm_g, w_in, hy_conv_w, hy_conv_b, flt_w1, flt_b1, flt_f1, flt_w2, flt_b2, flt_f2,
           flt_w3, flt_b3, flt_f3, flt_w4, hy_skip, lru_conv_w, lru_conv_b, lru_wa, lru_ba, lru_wx,
           lru_bx, lru_lam, hy_out_g, lru_out_g, w_out, final_g):
    B, L, D = x.shape
    assert norm_g.shape[0] == 1, "one layer"
    assert B % 2 == 0 and L % TILE == 0
    ha = L // NB
    na = 2 * ha
    row = lambda v: v.reshape(1, -1)

    u, x0, hg, xb, lg = _inproj(x, row(norm_g[0]), w_in[0].astype(bf16), hy_conv_w[0], row(hy_conv_b[0]),
                                lru_conv_w[0], row(lru_conv_b[0]))
    pair = lambda t: t.reshape(B // 2, NB, na, D_HY)

    m_data, m_real, m_inv, g_fwd, g_inv = (jnp.asarray(m).astype(bf16) for m in _dft_matrices(L))
    kt, l1 = _filter(L, flt_w1[0], flt_b1[0], flt_f1[0], flt_w2[0], flt_b2[0], flt_f2[0],
                     flt_w3[0], flt_b3[0], flt_f3[0], flt_w4[0])
    kf = _fft_k(kt, l1, row(hy_skip[0]), m_real, g_fwd, scale=1.0 / (2 * L))
    yh = _fft_conv(pair(u), pair(x0), kf, m_data, m_inv, g_fwd, g_inv)

    yl = _lru(xb, lru_wa[0], lru_ba[0], lru_wx[0], lru_bx[0], lru_lam[0])
    return _out(yh, hg, yl, lg, x, row(hy_out_g[0]), row(lru_out_g[0]), w_out[0].astype(bf16), row(final_g))
```

```python
import functools
import math

import jax
import jax.numpy as jnp
import numpy as np
from jax import lax
from jax.experimental import pallas as pl
from jax.experimental.pallas import tpu as pltpu

f32 = jnp.float32
bf16 = jnp.bfloat16

D_HY = 768
D_LRU = 768
HEAD = 64
LANES = 128
SUBLANES = 8
NB = 128
TILE = SUBLANES * NB
HALO = SUBLANES
FFT_CB = 256
NCB = D_HY // FFT_CB
FFT_BA = 16
FFT_PB = 8
FFT_BC = 16
FFT_PITCH = NB + SUBLANES
FFTK_BA = 32
FFTK_PB = 16
FILTER_BANDS = 16
FILTER_EMB = 2 * FILTER_BANDS + 1
MASK_COL = FILTER_EMB
FILTER_TARGET = 1e-2
MIN_DECAY = math.log(FILTER_TARGET) / 0.3
MAX_DECAY = math.log(FILTER_TARGET) / 1.5
LRU_C = 8.0
EPS = 1e-6
VMEM_LIMIT = 60 * 1024 * 1024


def _params(*sem):
    return pltpu.CompilerParams(dimension_semantics=sem, vmem_limit_bytes=VMEM_LIMIT)


def _rms(y, g):
    return y * lax.rsqrt(jnp.mean(y * y, axis=-1, keepdims=True) + EPS) * g


def _put_cols(ref, rows, val):
    for h in range(ref.shape[0]):
        ref[h, rows, :] = val[:, h * LANES:(h + 1) * LANES]


def _get_cols(ref, rows):
    return jnp.concatenate([ref[h, rows, :] for h in range(ref.shape[0])], axis=-1)


def _sigmoid(x):
    return 0.5 * jnp.tanh(0.5 * x) + 0.5


def _inproj_kernel(x_ref, xp_ref, xn_ref, g_ref, w_ref, hcw_ref, hcb_ref, lcw_ref, lcb_ref,
                   u_ref, x0_ref, hg_ref, xb_ref, lg_ref, xs_ref, *, n_tiles):
    i = pl.program_id(1)
    g = g_ref[...]
    for a in range(SUBLANES):
        _put_cols(xs_ref, pl.ds(a, NB, stride=SUBLANES), _rms(x_ref[a * NB:(a + 1) * NB, :], g))
    _put_cols(xs_ref, pl.ds(TILE, HALO), jnp.where(i > 0, _rms(xp_ref[...], g), 0.0))
    _put_cols(xs_ref, pl.ds(TILE + HALO, HALO), jnp.where(i < n_tiles - 1, _rms(xn_ref[...], g), 0.0))
    xn = _get_cols(xs_ref, pl.ds(0, TILE + 2 * HALO)).astype(bf16)
    sub = lax.broadcasted_iota(jnp.int32, (SUBLANES, D_HY), 0)

    def proj(c0):
        p = jnp.dot(xn, w_ref[:, c0:c0 + D_HY], preferred_element_type=f32)
        return p.reshape(NB + 2, SUBLANES, D_HY)

    def edge(p3, s):
        if s < 0:
            return jnp.where(sub == 0, p3[NB][SUBLANES + s:SUBLANES + s + 1],
                             pltpu.roll(p3[NB + s], 1, 0))
        return jnp.where(sub == SUBLANES - 1, p3[NB + 1][s - NB:s - NB + 1],
                         pltpu.roll(p3[s - NB], SUBLANES - 1, 0))

    def conv(p3, cw_ref, cb_ref, c0, offsets):
        lo, hi = max(0, -min(offsets)), NB - max(offsets)

        def acc(get):
            y = cb_ref[:, c0:c0 + D_HY]
            for k, o in enumerate(offsets):
                y = y + get(o) * cw_ref[k:k + 1, c0:c0 + D_HY]
            return y

        inner = acc(lambda o: p3[lo + o:hi + o])
        edges = {b: acc(lambda o, b=b: p3[b + o] if 0 <= b + o < NB else edge(p3, b + o))
                 for b in list(range(lo)) + list(range(hi, NB))}
        return lo, hi, inner, edges

    def store(ref, conv_out, other=None):
        lo, hi, inner, edges = conv_out
        if other is not None:
            inner = inner * other[2]
            edges = {b: edges[b] * other[3][b] for b in edges}
        if len(ref.shape) == 3:
            ref[lo:hi] = inner
            for b, y in edges.items():
                ref[b] = y
        else:
            for cb in range(NCB):
                cols = slice(cb * FFT_CB, (cb + 1) * FFT_CB)
                ref[cb, lo:hi] = inner[:, :, cols]
                for b, y in edges.items():
                    ref[cb, b] = y[:, cols]

    hy = (-1, 0, 1)
    store(u_ref, conv(proj(0), hcw_ref, hcb_ref, 0, hy), conv(proj(2 * D_HY), hcw_ref, hcb_ref, 2 * D_HY, hy))
    store(x0_ref, conv(proj(D_HY), hcw_ref, hcb_ref, D_HY, hy))
    store(xb_ref, conv(proj(4 * D_HY), lcw_ref, lcb_ref, 0, (-1, 0, 1, 2)))
    xm = xn[:TILE]
    hg = jnp.dot(xm, w_ref[:, 3 * D_HY:4 * D_HY], preferred_element_type=f32)
    hg_ref[...] = (hg * _sigmoid(hg)).astype(bf16)
    lg = jnp.dot(xm, w_ref[:, 4 * D_HY + D_LRU:], preferred_element_type=f32)
    lg_ref[...] = (lg * _sigmoid(lg)).astype(bf16)


def _ba_spec(c):
    return pl.BlockSpec((None, NB, None, SUBLANES, c), lambda b, i: (b // 2, 0, b % 2, i, 0))


def _gate_spec(c):
    return pl.BlockSpec((None, None, TILE, c), lambda b, i: (b, i, 0, 0))


def _inproj(x, norm_g, w_in, hcw, hcb, lcw, lcb):
    B, L, D = x.shape
    n_tiles = L // TILE
    ha = L // NB
    hb = TILE // HALO
    n_hb = L // HALO
    const = lambda b, i: (0, 0)
    ba_shape = jax.ShapeDtypeStruct((B // 2, NB, 2, ha, D_HY), f32)
    cb_shape = jax.ShapeDtypeStruct((B // 2, NCB, NB, 2, ha, FFT_CB), f32)
    cb_spec = pl.BlockSpec((None, NCB, NB, None, SUBLANES, FFT_CB), lambda b, i: (b // 2, 0, 0, b % 2, i, 0))
    gate_shape = jax.ShapeDtypeStruct((B, n_tiles, TILE, D_HY), bf16)
    return pl.pallas_call(
        functools.partial(_inproj_kernel, n_tiles=n_tiles),
        grid=(B, n_tiles),
        in_specs=[
            pl.BlockSpec((None, TILE, D), lambda b, i: (b, i, 0)),
            pl.BlockSpec((None, HALO, D), lambda b, i: (b, jnp.maximum(i * hb - 1, 0), 0)),
            pl.BlockSpec((None, HALO, D), lambda b, i: (b, jnp.minimum((i + 1) * hb, n_hb - 1), 0)),
            pl.BlockSpec((1, D), const),
            pl.BlockSpec(w_in.shape, const, pipeline_mode=pl.Buffered(1)),
            pl.BlockSpec(hcw.shape, const),
            pl.BlockSpec(hcb.shape, const),
            pl.BlockSpec(lcw.shape, const),
            pl.BlockSpec(lcb.shape, const),
        ],
        out_specs=[cb_spec, cb_spec, _gate_spec(D_HY), _ba_spec(D_LRU), _gate_spec(D_LRU)],
        out_shape=[cb_shape, cb_shape, gate_shape, ba_shape, gate_shape],
        scratch_shapes=[pltpu.VMEM((D // LANES, TILE + 2 * HALO, LANES), f32)],
        compiler_params=_params("parallel", "arbitrary"),
        name="inproj",
    )(x, x, x, norm_g, w_in, hcw, hcb, lcw, lcb)


def _filt_kernel(zf_ref, zb_ref, dl_ref, w1_ref, b1_ref, f1_ref, w2_ref, b2_ref, f2_ref,
                 w3_ref, b3_ref, f3_ref, w4f_ref, w4b_ref, o_ref, s_ref, *, ha):
    dot = functools.partial(jnp.dot, precision=lax.Precision.HIGHEST, preferred_element_type=f32)
    dl = dl_ref[...]
    zf = zf_ref[...]
    zb = zb_ref[...]
    h = jnp.concatenate([dot(zf, w1_ref[...]), dot(zb, w1_ref[...])], axis=-1)
    h = jnp.sin(f1_ref[...] * (h + b1_ref[...]))
    h = jnp.sin(f2_ref[...] * (dot(h, w2_ref[...]) + b2_ref[...]))
    h = jnp.sin(f3_ref[...] * (dot(h, w3_ref[...]) + b3_ref[...]))
    nh = h.shape[-1] // 2
    hf = dot(h[:, :nh], w4f_ref[...]) * jnp.exp(-zf[:, 0:1] * dl)
    hb = dot(h[:, nh:], w4b_ref[...]) * jnp.exp(-zb[:, 0:1] * dl) * zb[:, MASK_COL:MASK_COL + 1]
    for cb in range(NCB):
        cols = slice(cb * FFT_CB, (cb + 1) * FFT_CB)
        o_ref[cb, :, :ha, :] = hf[:, cols].reshape(SUBLANES, ha, FFT_CB)
        o_ref[cb, :, ha:, :] = hb[:, cols].reshape(SUBLANES, ha, FFT_CB)

    @pl.when(pl.program_id(0) == 0)
    def _():
        s_ref[...] = jnp.zeros_like(s_ref)

    s_ref[...] += jnp.sum(jnp.abs(hf), axis=0, keepdims=True) + jnp.sum(jnp.abs(hb), axis=0, keepdims=True)


@functools.lru_cache(maxsize=None)
def _filter_tables(L):
    ha = L // NB
    t = np.linspace(0.0, 1.0, L)
    w = (2.0 * math.pi / L) * np.arange(L)
    f = np.linspace(1e-4, FILTER_BANDS - 1, FILTER_BANDS)[None, :]

    def features(lag, mask):
        tl, wl = t[lag][:, None], w[lag][:, None]
        pad = np.zeros((lag.shape[0], LANES - FILTER_EMB - 1))
        z = np.concatenate([tl, np.cos(wl * f), -np.sin(wl * f), mask[:, None], pad], axis=-1)
        return z.astype(np.float32)

    b = np.arange(NB)[:, None]
    a = np.arange(ha)[None, :]
    lag_f = (NB * a + b).reshape(-1)
    lag_b = (L - NB * a - b).reshape(-1)
    valid = lag_b < L
    dl = np.abs(np.linspace(MIN_DECAY, MAX_DECAY, D_HY))[None, :].astype(np.float32)
    return (features(lag_f, np.zeros(lag_f.shape)),
            features(np.where(valid, lag_b, 0), valid.astype(np.float64)), dl)


def _filter(L, w1, b1, f1, w2, b2, f2, w3, b3, f3, w4):
    ha = L // NB
    rc = SUBLANES * ha
    zf, zb, dl = (jnp.asarray(t) for t in _filter_tables(L))
    w1p = jnp.pad(w1, ((0, LANES - w1.shape[0]), (0, 0)))
    row = lambda v: jnp.tile(v.reshape(1, -1), (1, 2))
    zero = jnp.zeros_like(w2)
    diag2 = lambda w: jnp.concatenate([jnp.concatenate([w, zero], axis=1),
                                       jnp.concatenate([zero, w], axis=1)], axis=0)
    const = lambda j: (0, 0)
    full = lambda arr: pl.BlockSpec(arr.shape, const)
    args = [zf, zb, dl, w1p, row(b1), row(f1), diag2(w2), row(b2), row(f2), diag2(w3), row(b3), row(f3), w4, w4]
    specs = [full(a) for a in args]
    specs[0] = specs[1] = pl.BlockSpec((rc, LANES), lambda j: (j, 0))
    specs[12] = pl.BlockSpec((w4.shape[0], D_HY), lambda j: (0, 0))
    specs[13] = pl.BlockSpec((w4.shape[0], D_HY), lambda j: (0, 1))
    return pl.pallas_call(
        functools.partial(_filt_kernel, ha=ha),
        grid=(NB // SUBLANES,),
        in_specs=specs,
        out_specs=[pl.BlockSpec((NCB, SUBLANES, 2 * ha, FFT_CB), lambda j: (0, j, 0, 0)),
                   pl.BlockSpec((1, D_HY), const)],
        out_shape=[jax.ShapeDtypeStruct((NCB, NB, 2 * ha, FFT_CB), f32),
                   jax.ShapeDtypeStruct((1, D_HY), f32)],
        compiler_params=_params("arbitrary"),
        name="filt",
    )(*args)


@functools.lru_cache(maxsize=None)
def _dft_matrices(L):
    n = 2 * L
    na = n // NB
    ha = na // 2
    b = np.arange(NB)[:, None, None]
    p = np.arange(na)[None, :, None]
    a = np.arange(na)[None, None, :]
    ang = (2.0 * math.pi / n) * ((p * (NB * a + b)) % n)
    c, s = np.cos(ang), np.sin(ang)
    ch, sh = c[:, :, :ha], s[:, :, :ha]
    m_data = np.concatenate([np.concatenate([ch, sh], axis=2),
                             np.concatenate([-sh, ch], axis=2)], axis=1)
    m_real = np.concatenate([c, -s], axis=1)
    cht, sht = np.swapaxes(ch, 1, 2), np.swapaxes(sh, 1, 2)
    m_inv = np.concatenate([np.concatenate([cht, -sht], axis=2),
                            np.concatenate([sht, cht], axis=2)], axis=1)
    q = np.arange(NB)
    gang = (2.0 * math.pi / NB) * ((q[:, None] * q[None, :]) % NB)
    gc, gs = np.cos(gang), np.sin(gang)
    g_fwd = np.concatenate([np.concatenate([gc, gs], axis=1),
                            np.concatenate([-gs, gc], axis=1)], axis=0)
    g_inv = np.concatenate([np.concatenate([gc, -gs], axis=1),
                            np.concatenate([gs, gc], axis=1)], axis=0)
    return tuple(m.astype(np.float32) for m in (m_data, m_real, m_inv, g_fwd, g_inv))


def _first_stage(x_ref, m_ref, s_ref, s, na):
    nb = x_ref.shape[0]
    for j in range(nb):
        res = jnp.dot(m_ref[j], x_ref[j].astype(bf16), preferred_element_type=f32)
        rows = pl.ds(s * nb + j, na, stride=FFT_PITCH)
        _put_cols(s_ref.at[0], rows, res[:na])
        _put_cols(s_ref.at[1], rows, res[na:])


def _slab(p):
    return pl.ds(pl.multiple_of(p * FFT_PITCH, SUBLANES), NB)


def _second_stage(s_ref, gf_ref, p):
    rows = _slab(p)
    y = jnp.concatenate([_get_cols(s_ref.at[0], rows), _get_cols(s_ref.at[1], rows)], axis=0)
    return jnp.dot(gf_ref[...], y.astype(bf16), preferred_element_type=f32)


def _fftk_kernel(k_ref, ma_ref, l1_ref, sk_ref, gf_ref, o_ref, s_ref, *, na, a_steps, scale):
    s = pl.program_id(1)

    @pl.when(s < a_steps)
    def _():
        _first_stage(k_ref, ma_ref, s_ref, s, na)

    @pl.when(s >= a_steps)
    def _():
        inv = scale / (l1_ref[...] + EPS)
        tap = scale * sk_ref[...]
        for j in range(FFTK_PB):
            z = _second_stage(s_ref, gf_ref, (s - a_steps) * FFTK_PB + j)
            o_ref[0, j] = z[:NB] * inv + tap
            o_ref[1, j] = z[NB:] * inv


def _fft_k(kt, l1, skip, m_real, g_fwd, scale):
    _, _, na, _ = kt.shape
    a_steps, b_steps = NB // FFTK_BA, na // FFTK_PB
    last_a = a_steps - 1
    const = lambda c, s: (0, 0)
    return pl.pallas_call(
        functools.partial(_fftk_kernel, na=na, a_steps=a_steps, scale=scale),
        grid=(NCB, a_steps + b_steps),
        in_specs=[pl.BlockSpec((None, FFTK_BA, na, FFT_CB), lambda c, s: (c, jnp.minimum(s, last_a), 0, 0)),
                  pl.BlockSpec((FFTK_BA, 2 * na, na), lambda c, s: (jnp.minimum(s, last_a), 0, 0)),
                  pl.BlockSpec((1, FFT_CB), lambda c, s: (0, c)),
                  pl.BlockSpec((1, FFT_CB), lambda c, s: (0, c)),
                  pl.BlockSpec(g_fwd.shape, const)],
        out_specs=pl.BlockSpec((2, None, FFTK_PB, NB, FFT_CB),
                               lambda c, s: (0, c, jnp.clip(s - a_steps, 0, b_steps - 1), 0, 0)),
        out_shape=jax.ShapeDtypeStruct((2, NCB, na, NB, FFT_CB), f32),
        scratch_shapes=[pltpu.VMEM((2, FFT_CB // LANES, na * FFT_PITCH, LANES), f32)],
        compiler_params=_params("arbitrary", "arbitrary"),
        name="fft_k",
    )(kt, m_real, l1, skip, g_fwd)


def _fftconv_kernel(ua_ref, ma_ref, k_ref, gf_ref, gi_ref, mc_ref, x0_ref, o_ref, s_ref,
                    *, na, a_steps, b_steps):
    s = pl.program_id(2)

    @pl.when(s < a_steps)
    def _():
        _first_stage(ua_ref, ma_ref, s_ref, s, na)

    @pl.when((s >= a_steps) & (s < a_steps + b_steps))
    def _():
        for j in range(FFT_PB):
            p = (s - a_steps) * FFT_PB + j
            z = _second_stage(s_ref, gf_ref, p)
            zr, zi = z[:NB], z[NB:]
            kr, ki = k_ref[0, j], k_ref[1, j]
            f = jnp.concatenate([zr * kr - zi * ki, zr * ki + zi * kr], axis=0)
            v = jnp.dot(gi_ref[...], f.astype(bf16), preferred_element_type=f32)
            _put_cols(s_ref.at[0], _slab(p), v[:NB])
            _put_cols(s_ref.at[1], _slab(p), v[NB:])

    @pl.when(s >= a_steps + b_steps)
    def _():
        def last_stage(j):
            rows = pl.ds((s - (a_steps + b_steps)) * FFT_BC + j, na, stride=FFT_PITCH)
            v = jnp.concatenate([_get_cols(s_ref.at[0], rows), _get_cols(s_ref.at[1], rows)], axis=0)
            conv = jnp.dot(mc_ref[j], v.astype(bf16), preferred_element_type=f32)
            return x0_ref[j] * conv

        ha = na // 2
        for jj in range(FFT_BC // 2):
            y0, y1 = last_stage(2 * jj), last_stage(2 * jj + 1)
            for half in range(2):
                for t in range(ha // SUBLANES):
                    r0 = half * ha + t * SUBLANES
                    y = jnp.concatenate([y0[r0:r0 + SUBLANES], y1[r0:r0 + SUBLANES]], axis=0)
                    o_ref[half, t, 2 * SUBLANES * jj:2 * SUBLANES * (jj + 1), :] = y.astype(bf16)


def _fft_conv(u, x0, kf, m_data, m_inv, g_fwd, g_inv):
    P, _, _, na, _ = u.shape
    a_steps, b_steps, c_steps = NB // FFT_BA, na // FFT_PB, NB // FFT_BC
    n_tiles = na // (2 * SUBLANES)
    last_a = a_steps - 1
    b_idx = lambda s: jnp.clip(s - a_steps, 0, b_steps - 1)
    c_idx = lambda s: jnp.clip(s - (a_steps + b_steps), 0, c_steps - 1)
    const = lambda c, r, s: (0, 0)
    c_spec = pl.BlockSpec((None, None, FFT_BC, na, FFT_CB), lambda c, r, s: (r, c, c_idx(s), 0, 0))
    return pl.pallas_call(
        functools.partial(_fftconv_kernel, na=na, a_steps=a_steps, b_steps=b_steps),
        grid=(NCB, P, a_steps + b_steps + c_steps),
        in_specs=[pl.BlockSpec((None, None, FFT_BA, na, FFT_CB), lambda c, r, s: (r, c, jnp.minimum(s, last_a), 0, 0)),
                  pl.BlockSpec((FFT_BA, 2 * na, na), lambda c, r, s: (jnp.minimum(s, last_a), 0, 0)),
                  pl.BlockSpec((2, None, FFT_PB, NB, FFT_CB), lambda c, r, s: (0, c, b_idx(s), 0, 0)),
                  pl.BlockSpec(g_fwd.shape, const),
                  pl.BlockSpec(g_inv.shape, const),
                  pl.BlockSpec((FFT_BC, na, 2 * na), lambda c, r, s: (c_idx(s), 0, 0)),
                  c_spec],
        out_specs=pl.BlockSpec((2, n_tiles, None, FFT_BC * SUBLANES, FFT_CB), lambda c, r, s: (r, 0, c, c_idx(s), 0)),
        out_shape=jax.ShapeDtypeStruct((2 * P, n_tiles, NCB, TILE, FFT_CB), bf16),
        scratch_shapes=[pltpu.VMEM((2, FFT_CB // LANES, na * FFT_PITCH, LANES), f32)],
        compiler_params=_params("arbitrary", "arbitrary", "arbitrary"),
        name="fft_conv",
    )(u, m_data, kf, g_fwd, g_inv, m_inv, x0)


def _carry_scan(at, bt, reverse):
    n = at.shape[0]
    row = lax.broadcasted_iota(jnp.int32, at.shape, 0)
    s = 1
    while s < n:
        keep = (row < n - s) if reverse else (row >= s)
        shift = n - s if reverse else s
        ash = jnp.where(keep, pltpu.roll(at, shift, 0), 1.0)
        bsh = jnp.where(keep, pltpu.roll(bt, shift, 0), 0.0)
        bt = at * bsh + bt
        at = at * ash
        s *= 2
    if reverse:
        return jnp.where(row < n - 1, pltpu.roll(bt, n - 1, 0), 0.0)
    return jnp.where(row >= 1, pltpu.roll(bt, 1, 0), 0.0)


def _lru_kernel(xb_ref, wg_ref, bg_ref, lam_ref, o_ref, hf_ref, af_ref, hb_ref, ab_ref, *, ha):
    bt = SUBLANES
    nc = NB // bt
    lam = lam_ref[...]
    half_c = (-0.5 * LRU_C) * (jnp.maximum(-lam, 0.0) + jnp.log1p(jnp.exp(-jnp.abs(lam))))
    rowi = lax.broadcasted_iota(jnp.int32, (bt * ha, 1), 0)

    def gates(d, k, first):
        b0 = pl.multiple_of(k * bt, bt)
        xc = xb_ref[pl.ds(b0, bt)].reshape(bt * ha, LANES)
        t = jnp.tanh(jnp.dot(xc.astype(bf16), wg_ref[d], preferred_element_type=f32) + bg_ref[d:d + 1, :])
        hc = half_c[d:d + 1, :]
        log_a = hc * t[:, :LANES] + hc
        a = jnp.exp(log_a)
        m2 = jnp.tanh(log_a) * (-1.0 - a * a)
        mult = jnp.where(m2 > 0.0, m2 * lax.rsqrt(m2), 0.0)
        if first:
            mult = jnp.where(rowi == (bt * ha - 1 if d else 0), 1.0, mult)
        hx = 0.5 * xc
        bv = mult * (t[:, LANES:] * hx + hx)
        return b0, a.reshape(bt, ha, LANES), bv.reshape(bt, ha, LANES)

    def step(k, carry, first=False):
        hf, af, hb, ab = carry
        b0, a, bv = gates(0, k, first)
        for j in range(bt):
            hf = a[j] * hf + bv[j]
            af = a[j] * af
            hf_ref[b0 + j] = hf
            af_ref[b0 + j] = af
        b0, a, bv = gates(1, nc - 1 - k, first)
        for j in reversed(range(bt)):
            hb = a[j] * hb + bv[j]
            ab = a[j] * ab
            hb_ref[b0 + j] = hb
            ab_ref[b0 + j] = ab
        return hf, af, hb, ab

    zero = jnp.zeros((ha, LANES), f32)
    one = jnp.ones((ha, LANES), f32)
    carry = step(0, (zero, one, zero, one), first=True)
    hf, af, hb, ab = lax.fori_loop(1, nc, step, carry)
    cf = _carry_scan(af, hf, reverse=False)
    cb = _carry_scan(ab, hb, reverse=True)

    def finish(k, c):
        b0 = pl.multiple_of(k * bt, bt)
        sl = pl.ds(b0, bt)
        h = (hf_ref[sl] + af_ref[sl] * cf) + (hb_ref[sl] + ab_ref[sl] * cb)
        r0 = pl.multiple_of(k * (bt * SUBLANES), bt * SUBLANES)
        for t in range(ha // SUBLANES):
            tile = h[:, t * SUBLANES:(t + 1) * SUBLANES, :].reshape(bt * SUBLANES, LANES)
            o_ref[t, pl.ds(r0, bt * SUBLANES), :] = tile.astype(bf16)
        return c

    lax.fori_loop(0, nc, finish, 0)


def _lru_gate_weights(wa, wx):
    def blockdiag(w):
        w = w.reshape(2, -1, 2, HEAD, HEAD)
        z = jnp.zeros_like(w[:, :, 0])
        top = jnp.concatenate([w[:, :, 0], z], axis=-1)
        bot = jnp.concatenate([z, w[:, :, 1]], axis=-1)
        return jnp.concatenate([top, bot], axis=-2)
    return jnp.concatenate([blockdiag(wa), blockdiag(wx)], axis=-1).astype(bf16)


def _lru(xb, wa, ba, wx, bx, lam):
    P, _, _, ha, C = xb.shape
    nblk = C // LANES
    wg = _lru_gate_weights(0.5 * wa, 0.5 * wx)
    bg = 0.5 * jnp.concatenate([ba.reshape(2, nblk, 1, LANES), bx.reshape(2, nblk, 1, LANES)], axis=-1)
    ba_spec = pl.BlockSpec((None, NB, None, ha, LANES), lambda b, c: (b // 2, 0, b % 2, 0, c))
    return pl.pallas_call(
        functools.partial(_lru_kernel, ha=ha),
        grid=(2 * P, nblk),
        in_specs=[ba_spec,
                  pl.BlockSpec((2, None, LANES, 2 * LANES), lambda b, c: (0, c, 0, 0)),
                  pl.BlockSpec((2, None, None, 2 * LANES), lambda b, c: (0, c, 0, 0)),
                  pl.BlockSpec((2, LANES), lambda b, c: (0, c))],
        out_specs=pl.BlockSpec((None, ha // SUBLANES, TILE, LANES), lambda b, c: (b, 0, 0, c)),
        out_shape=jax.ShapeDtypeStruct((2 * P, ha // SUBLANES, TILE, C), bf16),
        scratch_shapes=[pltpu.VMEM((NB, ha, LANES), f32)] * 4,
        compiler_params=_params("parallel", "arbitrary"),
        name="lru",
    )(xb, wg, bg, lam)


def _out_kernel(yh_ref, hg_ref, yl_ref, lg_ref, x_ref, hog_ref, log_ref, wo_ref, fg_ref, o_ref, ys_ref):
    yh = jnp.concatenate([yh_ref[cb] for cb in range(NCB)], axis=-1)
    ycat = jnp.concatenate([_rms(yh.astype(f32), hog_ref[...]) * hg_ref[...],
                            _rms(yl_ref[...].astype(f32), log_ref[...]) * lg_ref[...]], axis=-1)
    y = jnp.dot(ycat.astype(bf16), wo_ref[...], preferred_element_type=f32)
    _put_cols(ys_ref, pl.ds(0, TILE), y)
    for a in range(SUBLANES):
        rows = slice(a * NB, (a + 1) * NB)
        ya = _get_cols(ys_ref, pl.ds(a, NB, stride=SUBLANES))
        o_ref[rows, :] = _rms(x_ref[rows, :] + ya, fg_ref[...])


def _out(yh, hg, yl, lg, x, hog, log_g, w_out, fg):
    B, L, D = x.shape
    const = lambda b, i: (0, 0)
    nat = pl.BlockSpec((None, TILE, D), lambda b, i: (b, i, 0))
    return pl.pallas_call(
        _out_kernel,
        grid=(B, L // TILE),
        in_specs=[pl.BlockSpec((None, None, NCB, TILE, FFT_CB), lambda b, i: (b, i, 0, 0, 0)),
                  _gate_spec(D_HY), _gate_spec(D_LRU), _gate_spec(D_LRU), nat,
                  pl.BlockSpec((1, D_HY), const), pl.BlockSpec((1, D_LRU), const),
                  pl.BlockSpec(w_out.shape, const), pl.BlockSpec((1, D), const)],
        out_specs=nat,
        out_shape=jax.ShapeDtypeStruct((B, L, D), f32),
        scratch_shapes=[pltpu.VMEM((D // LANES, TILE, LANES), f32)],
        compiler_params=_params("parallel", "arbitrary"),
        name="out",
    )(yh, hg, yl, lg, x, hog, log_g, w_out, fg)


def kernel(x, norm_g, w_in, hy_conv_w, hy_conv_b, flt_w1, flt_b1, flt_f1, flt_w2, flt_b2, flt_f2,
           flt_w3, flt_b3, flt_f3, flt_w4, hy_skip, lru_conv_w, lru_conv_b, lru_wa, lru_ba, lru_wx,
           lru_bx, lru_lam, hy_out_g, lru_out_g, w_out, final_g):
    B, L, D = x.shape
    assert norm_g.shape[0] == 1, "one layer"
    assert B % 2 == 0 and L % TILE == 0
    ha = L // NB
    na = 2 * ha
    row = lambda v: v.reshape(1, -1)

    u, x0, hg, xb, lg = _inproj(x, row(norm_g[0]), w_in[0].astype(bf16), hy_conv_w[0], row(hy_conv_b[0]),
                                lru_conv_w[0], row(lru_conv_b[0]))
    pair = lambda t: t.reshape(B // 2, NCB, NB, na, FFT_CB)

    m_data, m_real, m_inv, g_fwd, g_inv = (jnp.asarray(m).astype(bf16) for m in _dft_matrices(L))
    kt, l1 = _filter(L, flt_w1[0], flt_b1[0], flt_f1[0], flt_w2[0], flt_b2[0], flt_f2[0],
                     flt_w3[0], flt_b3[0], flt_f3[0], flt_w4[0])
    kf = _fft_k(kt, l1, row(hy_skip[0]), m_real, g_fwd, scale=1.0 / (2 * L))
    yh = _fft_conv(pair(u), pair(x0), kf, m_data, m_inv, g_fwd, g_inv)

    yl = _lru(xb, lru_wa[0], lru_ba[0], lru_wx[0], lru_bx[0], lru_lam[0])
    return _out(yh, hg, yl, lg, x, row(hy_out_g[0]), row(lru_out_g[0]), w_out[0].astype(bf16), row(final_g))
```

```python
import functools
import math

import jax
import jax.numpy as jnp
import numpy as np
from jax import lax
from jax.experimental import pallas as pl
from jax.experimental.pallas import tpu as pltpu

f32 = jnp.float32
bf16 = jnp.bfloat16

D_HY = 768
D_LRU = 768
HEAD = 64
LANES = 128
SUBLANES = 8
NB = 128
TILE = SUBLANES * NB
HALO = SUBLANES
FFT_CB = 256
NCB = D_HY // FFT_CB
FFT_BA = 16
FFT_PB = 16
FFT_BC = 16
FFT_PITCH = NB + SUBLANES
FFTK_BA = 32
FFTK_PB = 16
FILTER_BANDS = 16
FILTER_EMB = 2 * FILTER_BANDS + 1
MASK_COL = FILTER_EMB
FILTER_TARGET = 1e-2
MIN_DECAY = math.log(FILTER_TARGET) / 0.3
MAX_DECAY = math.log(FILTER_TARGET) / 1.5
LRU_C = 8.0
EPS = 1e-6
VMEM_LIMIT = 60 * 1024 * 1024


def _params(*sem):
    return pltpu.CompilerParams(dimension_semantics=sem, vmem_limit_bytes=VMEM_LIMIT)


def _rms(y, g):
    return y * lax.rsqrt(jnp.mean(y * y, axis=-1, keepdims=True) + EPS) * g


def _put_cols(ref, rows, val):
    for h in range(ref.shape[0]):
        ref[h, rows, :] = val[:, h * LANES:(h + 1) * LANES]


def _get_cols(ref, rows):
    return jnp.concatenate([ref[h, rows, :] for h in range(ref.shape[0])], axis=-1)


def _dot3(a, b):
    ah = a.astype(bf16)
    al = (a - ah.astype(f32)).astype(bf16)
    bh = b.astype(bf16)
    bl = (b - bh.astype(f32)).astype(bf16)
    dot = functools.partial(jnp.dot, preferred_element_type=f32)
    return dot(ah, bh) + (dot(ah, bl) + dot(al, bh))


def _sigmoid(x):
    return 0.5 * jnp.tanh(0.5 * x) + 0.5


def _inproj_kernel(x_ref, xp_ref, xn_ref, g_ref, w_ref, hcw_ref, hcb_ref, lcw_ref, lcb_ref,
                   u_ref, x0_ref, hg_ref, xb_ref, lg_ref, xs_ref, *, n_tiles):
    i = pl.program_id(1)
    g = g_ref[...]
    for a in range(SUBLANES):
        _put_cols(xs_ref, pl.ds(a, NB, stride=SUBLANES), _rms(x_ref[a * NB:(a + 1) * NB, :], g))
    _put_cols(xs_ref, pl.ds(TILE, HALO), jnp.where(i > 0, _rms(xp_ref[...], g), 0.0))
    _put_cols(xs_ref, pl.ds(TILE + HALO, HALO), jnp.where(i < n_tiles - 1, _rms(xn_ref[...], g), 0.0))
    xn = _get_cols(xs_ref, pl.ds(0, TILE + 2 * HALO)).astype(bf16)
    sub = lax.broadcasted_iota(jnp.int32, (SUBLANES, D_HY), 0)

    def proj(c0):
        p = jnp.dot(xn, w_ref[:, c0:c0 + D_HY], preferred_element_type=f32)
        return p.reshape(NB + 2, SUBLANES, D_HY)

    def edge(p3, s):
        if s < 0:
            return jnp.where(sub == 0, p3[NB][SUBLANES + s:SUBLANES + s + 1],
                             pltpu.roll(p3[NB + s], 1, 0))
        return jnp.where(sub == SUBLANES - 1, p3[NB + 1][s - NB:s - NB + 1],
                         pltpu.roll(p3[s - NB], SUBLANES - 1, 0))

    def conv(p3, cw_ref, cb_ref, c0, offsets):
        lo, hi = max(0, -min(offsets)), NB - max(offsets)

        def acc(get):
            y = cb_ref[:, c0:c0 + D_HY]
            for k, o in enumerate(offsets):
                y = y + get(o) * cw_ref[k:k + 1, c0:c0 + D_HY]
            return y

        inner = acc(lambda o: p3[lo + o:hi + o])
        edges = {b: acc(lambda o, b=b: p3[b + o] if 0 <= b + o < NB else edge(p3, b + o))
                 for b in list(range(lo)) + list(range(hi, NB))}
        return lo, hi, inner, edges

    def store(ref, conv_out, other=None):
        lo, hi, inner, edges = conv_out
        if other is not None:
            inner = inner * other[2]
            edges = {b: edges[b] * other[3][b] for b in edges}
        if len(ref.shape) == 3:
            ref[lo:hi] = inner
            for b, y in edges.items():
                ref[b] = y
        else:
            for cb in range(NCB):
                cols = slice(cb * FFT_CB, (cb + 1) * FFT_CB)
                ref[cb, lo:hi] = inner[:, :, cols]
                for b, y in edges.items():
                    ref[cb, b] = y[:, cols]

    hy = (-1, 0, 1)
    store(u_ref, conv(proj(0), hcw_ref, hcb_ref, 0, hy), conv(proj(2 * D_HY), hcw_ref, hcb_ref, 2 * D_HY, hy))
    store(x0_ref, conv(proj(D_HY), hcw_ref, hcb_ref, D_HY, hy))
    store(xb_ref, conv(proj(4 * D_HY), lcw_ref, lcb_ref, 0, (-1, 0, 1, 2)))
    xm = xn[:TILE]
    hg = jnp.dot(xm, w_ref[:, 3 * D_HY:4 * D_HY], preferred_element_type=f32)
    hg_ref[...] = (hg * _sigmoid(hg)).astype(bf16)
    lg = jnp.dot(xm, w_ref[:, 4 * D_HY + D_LRU:], preferred_element_type=f32)
    lg_ref[...] = (lg * _sigmoid(lg)).astype(bf16)


def _ba_spec(c):
    return pl.BlockSpec((None, NB, None, SUBLANES, c), lambda b, i: (b // 2, 0, b % 2, i, 0))


def _gate_spec(c):
    return pl.BlockSpec((None, None, TILE, c), lambda b, i: (b, i, 0, 0))


def _inproj(x, norm_g, w_in, hcw, hcb, lcw, lcb):
    B, L, D = x.shape
    n_tiles = L // TILE
    ha = L // NB
    hb = TILE // HALO
    n_hb = L // HALO
    const = lambda b, i: (0, 0)
    ba_shape = jax.ShapeDtypeStruct((B // 2, NB, 2, ha, D_HY), f32)
    cb_shape = jax.ShapeDtypeStruct((B // 2, NCB, NB, 2, ha, FFT_CB), f32)
    cb_spec = pl.BlockSpec((None, NCB, NB, None, SUBLANES, FFT_CB), lambda b, i: (b // 2, 0, 0, b % 2, i, 0))
    gate_shape = jax.ShapeDtypeStruct((B, n_tiles, TILE, D_HY), bf16)
    return pl.pallas_call(
        functools.partial(_inproj_kernel, n_tiles=n_tiles),
        grid=(B, n_tiles),
        in_specs=[
            pl.BlockSpec((None, TILE, D), lambda b, i: (b, i, 0)),
            pl.BlockSpec((None, HALO, D), lambda b, i: (b, jnp.maximum(i * hb - 1, 0), 0)),
            pl.BlockSpec((None, HALO, D), lambda b, i: (b, jnp.minimum((i + 1) * hb, n_hb - 1), 0)),
            pl.BlockSpec((1, D), const),
            pl.BlockSpec(w_in.shape, const, pipeline_mode=pl.Buffered(1)),
            pl.BlockSpec(hcw.shape, const),
            pl.BlockSpec(hcb.shape, const),
            pl.BlockSpec(lcw.shape, const),
            pl.BlockSpec(lcb.shape, const),
        ],
        out_specs=[cb_spec, cb_spec, _gate_spec(D_HY), _ba_spec(D_LRU), _gate_spec(D_LRU)],
        out_shape=[cb_shape, cb_shape, gate_shape, ba_shape, gate_shape],
        scratch_shapes=[pltpu.VMEM((D // LANES, TILE + 2 * HALO, LANES), f32)],
        compiler_params=_params("parallel", "arbitrary"),
        name="inproj",
    )(x, x, x, norm_g, w_in, hcw, hcb, lcw, lcb)


def _filt_kernel(zf_ref, zb_ref, dl_ref, w1_ref, b1_ref, f1_ref, w2_ref, b2_ref, f2_ref,
                 w3_ref, b3_ref, f3_ref, w4f_ref, w4b_ref, o_ref, s_ref, *, ha):
    dot = _dot3
    dl = dl_ref[...]
    zf = zf_ref[...]
    zb = zb_ref[...]
    h = jnp.concatenate([dot(zf, w1_ref[...]), dot(zb, w1_ref[...])], axis=-1)
    h = jnp.sin(f1_ref[...] * (h + b1_ref[...]))
    h = jnp.sin(f2_ref[...] * (dot(h, w2_ref[...]) + b2_ref[...]))
    h = jnp.sin(f3_ref[...] * (dot(h, w3_ref[...]) + b3_ref[...]))
    nh = h.shape[-1] // 2
    hf = dot(h[:, :nh], w4f_ref[...]) * jnp.exp(-zf[:, 0:1] * dl)
    hb = dot(h[:, nh:], w4b_ref[...]) * jnp.exp(-zb[:, 0:1] * dl) * zb[:, MASK_COL:MASK_COL + 1]
    for cb in range(NCB):
        cols = slice(cb * FFT_CB, (cb + 1) * FFT_CB)
        o_ref[cb, :, :ha, :] = hf[:, cols].reshape(SUBLANES, ha, FFT_CB)
        o_ref[cb, :, ha:, :] = hb[:, cols].reshape(SUBLANES, ha, FFT_CB)

    @pl.when(pl.program_id(0) == 0)
    def _():
        s_ref[...] = jnp.zeros_like(s_ref)

    s_ref[...] += jnp.sum(jnp.abs(hf), axis=0, keepdims=True) + jnp.sum(jnp.abs(hb), axis=0, keepdims=True)


@functools.lru_cache(maxsize=None)
def _filter_tables(L):
    ha = L // NB
    t = np.linspace(0.0, 1.0, L)
    w = (2.0 * math.pi / L) * np.arange(L)
    f = np.linspace(1e-4, FILTER_BANDS - 1, FILTER_BANDS)[None, :]

    def features(lag, mask):
        tl, wl = t[lag][:, None], w[lag][:, None]
        pad = np.zeros((lag.shape[0], LANES - FILTER_EMB - 1))
        z = np.concatenate([tl, np.cos(wl * f), -np.sin(wl * f), mask[:, None], pad], axis=-1)
        return z.astype(np.float32)

    b = np.arange(NB)[:, None]
    a = np.arange(ha)[None, :]
    lag_f = (NB * a + b).reshape(-1)
    lag_b = (L - NB * a - b).reshape(-1)
    valid = lag_b < L
    dl = np.abs(np.linspace(MIN_DECAY, MAX_DECAY, D_HY))[None, :].astype(np.float32)
    return (features(lag_f, np.zeros(lag_f.shape)),
            features(np.where(valid, lag_b, 0), valid.astype(np.float64)), dl)


def _filter(L, w1, b1, f1, w2, b2, f2, w3, b3, f3, w4):
    ha = L // NB
    rc = SUBLANES * ha
    zf, zb, dl = (jnp.asarray(t) for t in _filter_tables(L))
    w1p = jnp.pad(w1, ((0, LANES - w1.shape[0]), (0, 0)))
    row = lambda v: jnp.tile(v.reshape(1, -1), (1, 2))
    zero = jnp.zeros_like(w2)
    diag2 = lambda w: jnp.concatenate([jnp.concatenate([w, zero], axis=1),
                                       jnp.concatenate([zero, w], axis=1)], axis=0)
    const = lambda j: (0, 0)
    full = lambda arr: pl.BlockSpec(arr.shape, const)
    args = [zf, zb, dl, w1p, row(b1), row(f1), diag2(w2), row(b2), row(f2), diag2(w3), row(b3), row(f3), w4, w4]
    specs = [full(a) for a in args]
    specs[0] = specs[1] = pl.BlockSpec((rc, LANES), lambda j: (j, 0))
    specs[12] = pl.BlockSpec((w4.shape[0], D_HY), lambda j: (0, 0))
    specs[13] = pl.BlockSpec((w4.shape[0], D_HY), lambda j: (0, 1))
    return pl.pallas_call(
        functools.partial(_filt_kernel, ha=ha),
        grid=(NB // SUBLANES,),
        in_specs=specs,
        out_specs=[pl.BlockSpec((NCB, SUBLANES, 2 * ha, FFT_CB), lambda j: (0, j, 0, 0)),
                   pl.BlockSpec((1, D_HY), const)],
        out_shape=[jax.ShapeDtypeStruct((NCB, NB, 2 * ha, FFT_CB), f32),
                   jax.ShapeDtypeStruct((1, D_HY), f32)],
        compiler_params=_params("arbitrary"),
        name="filt",
    )(*args)


@functools.lru_cache(maxsize=None)
def _dft_matrices(L):
    n = 2 * L
    na = n // NB
    ha = na // 2
    b = np.arange(NB)[:, None, None]
    p = np.arange(na)[None, :, None]
    a = np.arange(na)[None, None, :]
    ang = (2.0 * math.pi / n) * ((p * (NB * a + b)) % n)
    c, s = np.cos(ang), np.sin(ang)
    ch, sh = c[:, :, :ha], s[:, :, :ha]
    m_data = np.concatenate([np.concatenate([ch, sh], axis=2),
                             np.concatenate([-sh, ch], axis=2)], axis=1)
    m_real = np.concatenate([c, -s], axis=1)
    cht, sht = np.swapaxes(ch, 1, 2), np.swapaxes(sh, 1, 2)
    m_inv = np.concatenate([np.concatenate([cht, -sht], axis=2),
                            np.concatenate([sht, cht], axis=2)], axis=1)
    q = np.arange(NB)
    gang = (2.0 * math.pi / NB) * ((q[:, None] * q[None, :]) % NB)
    gc, gs = np.cos(gang), np.sin(gang)
    g_fwd = np.concatenate([np.concatenate([gc, gs], axis=1),
                            np.concatenate([-gs, gc], axis=1)], axis=0)
    g_inv = np.concatenate([np.concatenate([gc, -gs], axis=1),
                            np.concatenate([gs, gc], axis=1)], axis=0)
    return tuple(m.astype(np.float32) for m in (m_data, m_real, m_inv, g_fwd, g_inv))


def _first_stage(x_ref, m_ref, s_ref, s, na):
    nb = x_ref.shape[0]
    for j in range(nb):
        res = jnp.dot(m_ref[j], x_ref[j].astype(bf16), preferred_element_type=f32)
        rows = pl.ds(s * nb + j, na, stride=FFT_PITCH)
        _put_cols(s_ref.at[0], rows, res[:na])
        _put_cols(s_ref.at[1], rows, res[na:])


def _slab(p):
    return pl.ds(pl.multiple_of(p * FFT_PITCH, SUBLANES), NB)


def _second_stage(s_ref, gf_ref, p):
    rows = _slab(p)
    y = jnp.concatenate([_get_cols(s_ref.at[0], rows), _get_cols(s_ref.at[1], rows)], axis=0)
    return jnp.dot(gf_ref[...], y.astype(bf16), preferred_element_type=f32)


def _fftk_kernel(k_ref, ma_ref, l1_ref, sk_ref, gf_ref, o_ref, s_ref, *, na, a_steps, scale):
    s = pl.program_id(1)

    @pl.when(s < a_steps)
    def _():
        _first_stage(k_ref, ma_ref, s_ref, s, na)

    @pl.when(s >= a_steps)
    def _():
        inv = scale / (l1_ref[...] + EPS)
        tap = scale * sk_ref[...]
        for j in range(FFTK_PB):
            z = _second_stage(s_ref, gf_ref, (s - a_steps) * FFTK_PB + j)
            o_ref[0, j] = (z[:NB] * inv + tap).astype(o_ref.dtype)
            o_ref[1, j] = (z[NB:] * inv).astype(o_ref.dtype)


def _fft_k(kt, l1, skip, m_real, g_fwd, scale):
    _, _, na, _ = kt.shape
    a_steps, b_steps = NB // FFTK_BA, na // FFTK_PB
    last_a = a_steps - 1
    const = lambda c, s: (0, 0)
    return pl.pallas_call(
        functools.partial(_fftk_kernel, na=na, a_steps=a_steps, scale=scale),
        grid=(NCB, a_steps + b_steps),
        in_specs=[pl.BlockSpec((None, FFTK_BA, na, FFT_CB), lambda c, s: (c, jnp.minimum(s, last_a), 0, 0)),
                  pl.BlockSpec((FFTK_BA, 2 * na, na), lambda c, s: (jnp.minimum(s, last_a), 0, 0)),
                  pl.BlockSpec((1, FFT_CB), lambda c, s: (0, c)),
                  pl.BlockSpec((1, FFT_CB), lambda c, s: (0, c)),
                  pl.BlockSpec(g_fwd.shape, const)],
        out_specs=pl.BlockSpec((2, None, FFTK_PB, NB, FFT_CB),
                               lambda c, s: (0, c, jnp.clip(s - a_steps, 0, b_steps - 1), 0, 0)),
        out_shape=jax.ShapeDtypeStruct((2, NCB, na, NB, FFT_CB), bf16),
        scratch_shapes=[pltpu.VMEM((2, FFT_CB // LANES, na * FFT_PITCH, LANES), f32)],
        compiler_params=_params("arbitrary", "arbitrary"),
        name="fft_k",
    )(kt, m_real, l1, skip, g_fwd)


def _fftconv_kernel(ua_ref, ma_ref, k_ref, gf_ref, gi_ref, mc_ref, x0_ref, o_ref, s_ref,
                    *, na, a_steps, b_steps):
    s = pl.program_id(2)

    @pl.when(s < a_steps)
    def _():
        _first_stage(ua_ref, ma_ref, s_ref, s, na)

    @pl.when((s >= a_steps) & (s < a_steps + b_steps))
    def _():
        for j in range(FFT_PB):
            p = (s - a_steps) * FFT_PB + j
            z = _second_stage(s_ref, gf_ref, p)
            zr, zi = z[:NB], z[NB:]
            kr, ki = k_ref[0, j], k_ref[1, j]
            f = jnp.concatenate([zr * kr - zi * ki, zr * ki + zi * kr], axis=0)
            v = jnp.dot(gi_ref[...], f.astype(bf16), preferred_element_type=f32)
            _put_cols(s_ref.at[0], _slab(p), v[:NB])
            _put_cols(s_ref.at[1], _slab(p), v[NB:])

    @pl.when(s >= a_steps + b_steps)
    def _():
        def last_stage(j):
            rows = pl.ds((s - (a_steps + b_steps)) * FFT_BC + j, na, stride=FFT_PITCH)
            v = jnp.concatenate([_get_cols(s_ref.at[0], rows), _get_cols(s_ref.at[1], rows)], axis=0)
            conv = jnp.dot(mc_ref[j], v.astype(bf16), preferred_element_type=f32)
            return x0_ref[j] * conv

        ha = na // 2
        for jj in range(FFT_BC // 2):
            y0, y1 = last_stage(2 * jj), last_stage(2 * jj + 1)
            for half in range(2):
                for t in range(ha // SUBLANES):
                    r0 = half * ha + t * SUBLANES
                    y = jnp.concatenate([y0[r0:r0 + SUBLANES], y1[r0:r0 + SUBLANES]], axis=0)
                    o_ref[half, t, 2 * SUBLANES * jj:2 * SUBLANES * (jj + 1), :] = y.astype(bf16)


def _fft_conv(u, x0, kf, m_data, m_inv, g_fwd, g_inv):
    P, _, _, na, _ = u.shape
    a_steps, b_steps, c_steps = NB // FFT_BA, na // FFT_PB, NB // FFT_BC
    n_tiles = na // (2 * SUBLANES)
    last_a = a_steps - 1
    b_idx = lambda s: jnp.clip(s - a_steps, 0, b_steps - 1)
    c_idx = lambda s: jnp.clip(s - (a_steps + b_steps), 0, c_steps - 1)
    const = lambda c, r, s: (0, 0)
    c_spec = pl.BlockSpec((None, None, FFT_BC, na, FFT_CB), lambda c, r, s: (r, c, c_idx(s), 0, 0))
    return pl.pallas_call(
        functools.partial(_fftconv_kernel, na=na, a_steps=a_steps, b_steps=b_steps),
        grid=(NCB, P, a_steps + b_steps + c_steps),
        in_specs=[pl.BlockSpec((None, None, FFT_BA, na, FFT_CB), lambda c, r, s: (r, c, jnp.minimum(s, last_a), 0, 0)),
                  pl.BlockSpec((FFT_BA, 2 * na, na), lambda c, r, s: (jnp.minimum(s, last_a), 0, 0)),
                  pl.BlockSpec((2, None, FFT_PB, NB, FFT_CB), lambda c, r, s: (0, c, b_idx(s), 0, 0)),
                  pl.BlockSpec(g_fwd.shape, const),
                  pl.BlockSpec(g_inv.shape, const),
                  pl.BlockSpec((FFT_BC, na, 2 * na), lambda c, r, s: (c_idx(s), 0, 0)),
                  c_spec],
        out_specs=pl.BlockSpec((2, n_tiles, None, FFT_BC * SUBLANES, FFT_CB), lambda c, r, s: (r, 0, c, c_idx(s), 0)),
        out_shape=jax.ShapeDtypeStruct((2 * P, n_tiles, NCB, TILE, FFT_CB), bf16),
        scratch_shapes=[pltpu.VMEM((2, FFT_CB // LANES, na * FFT_PITCH, LANES), f32)],
        compiler_params=_params("arbitrary", "arbitrary", "arbitrary"),
        name="fft_conv",
    )(u, m_data, kf, g_fwd, g_inv, m_inv, x0)


def _carry_scan(at, bt, reverse):
    n = at.shape[0]
    row = lax.broadcasted_iota(jnp.int32, at.shape, 0)
    s = 1
    while s < n:
        keep = (row < n - s) if reverse else (row >= s)
        shift = n - s if reverse else s
        ash = jnp.where(keep, pltpu.roll(at, shift, 0), 1.0)
        bsh = jnp.where(keep, pltpu.roll(bt, shift, 0), 0.0)
        bt = at * bsh + bt
        at = at * ash
        s *= 2
    if reverse:
        return jnp.where(row < n - 1, pltpu.roll(bt, n - 1, 0), 0.0)
    return jnp.where(row >= 1, pltpu.roll(bt, 1, 0), 0.0)


def _lru_kernel(xb_ref, wg_ref, bg_ref, lam_ref, o_ref, hf_ref, af_ref, hb_ref, ab_ref, *, ha):
    bt = SUBLANES
    nc = NB // bt
    lam = lam_ref[...]
    half_c = (-0.5 * LRU_C) * (jnp.maximum(-lam, 0.0) + jnp.log1p(jnp.exp(-jnp.abs(lam))))
    rowi = lax.broadcasted_iota(jnp.int32, (bt * ha, 1), 0)

    def gates(d, k, first):
        b0 = pl.multiple_of(k * bt, bt)
        xc = xb_ref[pl.ds(b0, bt)].reshape(bt * ha, LANES)
        t = jnp.tanh(jnp.dot(xc.astype(bf16), wg_ref[d], preferred_element_type=f32) + bg_ref[d:d + 1, :])
        hc = half_c[d:d + 1, :]
        log_a = hc * t[:, :LANES] + hc
        a = jnp.exp(log_a)
        m2 = jnp.tanh(log_a) * (-1.0 - a * a)
        mult = jnp.where(m2 > 0.0, m2 * lax.rsqrt(m2), 0.0)
        if first:
            mult = jnp.where(rowi == (bt * ha - 1 if d else 0), 1.0, mult)
        hx = 0.5 * xc
        bv = mult * (t[:, LANES:] * hx + hx)
        return b0, a.reshape(bt, ha, LANES), bv.reshape(bt, ha, LANES)

    def step(k, carry, first=False):
        hf, af, hb, ab = carry
        b0, a, bv = gates(0, k, first)
        for j in range(bt):
            hf = a[j] * hf + bv[j]
            af = a[j] * af
            hf_ref[b0 + j] = hf
            af_ref[b0 + j] = af
        b0, a, bv = gates(1, nc - 1 - k, first)
        for j in reversed(range(bt)):
            hb = a[j] * hb + bv[j]
            ab = a[j] * ab
            hb_ref[b0 + j] = hb
            ab_ref[b0 + j] = ab
        return hf, af, hb, ab

    zero = jnp.zeros((ha, LANES), f32)
    one = jnp.ones((ha, LANES), f32)
    carry = step(0, (zero, one, zero, one), first=True)
    hf, af, hb, ab = lax.fori_loop(1, nc, step, carry)
    cf = _carry_scan(af, hf, reverse=False)
    cb = _carry_scan(ab, hb, reverse=True)

    def finish(k, c):
        b0 = pl.multiple_of(k * bt, bt)
        sl = pl.ds(b0, bt)
        h = (hf_ref[sl] + af_ref[sl] * cf) + (hb_ref[sl] + ab_ref[sl] * cb)
        r0 = pl.multiple_of(k * (bt * SUBLANES), bt * SUBLANES)
        for t in range(ha // SUBLANES):
            tile = h[:, t * SUBLANES:(t + 1) * SUBLANES, :].reshape(bt * SUBLANES, LANES)
            o_ref[t, pl.ds(r0, bt * SUBLANES), :] = tile.astype(bf16)
        return c

    lax.fori_loop(0, nc, finish, 0)


def _lru_gate_weights(wa, wx):
    def blockdiag(w):
        w = w.reshape(2, -1, 2, HEAD, HEAD)
        z = jnp.zeros_like(w[:, :, 0])
        top = jnp.concatenate([w[:, :, 0], z], axis=-1)
        bot = jnp.concatenate([z, w[:, :, 1]], axis=-1)
        return jnp.concatenate([top, bot], axis=-2)
    return jnp.concatenate([blockdiag(wa), blockdiag(wx)], axis=-1).astype(bf16)


def _lru(xb, wa, ba, wx, bx, lam):
    P, _, _, ha, C = xb.shape
    nblk = C // LANES
    wg = _lru_gate_weights(0.5 * wa, 0.5 * wx)
    bg = 0.5 * jnp.concatenate([ba.reshape(2, nblk, 1, LANES), bx.reshape(2, nblk, 1, LANES)], axis=-1)
    ba_spec = pl.BlockSpec((None, NB, None, ha, LANES), lambda b, c: (b // 2, 0, b % 2, 0, c))
    return pl.pallas_call(
        functools.partial(_lru_kernel, ha=ha),
        grid=(2 * P, nblk),
        in_specs=[ba_spec,
                  pl.BlockSpec((2, None, LANES, 2 * LANES), lambda b, c: (0, c, 0, 0)),
                  pl.BlockSpec((2, None, None, 2 * LANES), lambda b, c: (0, c, 0, 0)),
                  pl.BlockSpec((2, LANES), lambda b, c: (0, c))],
        out_specs=pl.BlockSpec((None, ha // SUBLANES, TILE, LANES), lambda b, c: (b, 0, 0, c)),
        out_shape=jax.ShapeDtypeStruct((2 * P, ha // SUBLANES, TILE, C), bf16),
        scratch_shapes=[pltpu.VMEM((NB, ha, LANES), f32)] * 4,
        compiler_params=_params("parallel", "arbitrary"),
        name="lru",
    )(xb, wg, bg, lam)


def _out_kernel(yh_ref, hg_ref, yl_ref, lg_ref, x_ref, hog_ref, log_ref, wo_ref, fg_ref, o_ref, ys_ref):
    yh = jnp.concatenate([yh_ref[cb] for cb in range(NCB)], axis=-1)
    ycat = jnp.concatenate([_rms(yh.astype(f32), hog_ref[...]) * hg_ref[...],
                            _rms(yl_ref[...].astype(f32), log_ref[...]) * lg_ref[...]], axis=-1)
    y = jnp.dot(ycat.astype(bf16), wo_ref[...], preferred_element_type=f32)
    _put_cols(ys_ref, pl.ds(0, TILE), y)
    for a in range(SUBLANES):
        rows = slice(a * NB, (a + 1) * NB)
        ya = _get_cols(ys_ref, pl.ds(a, NB, stride=SUBLANES))
        o_ref[rows, :] = _rms(x_ref[rows, :] + ya, fg_ref[...])


def _out(yh, hg, yl, lg, x, hog, log_g, w_out, fg):
    B, L, D = x.shape
    const = lambda b, i: (0, 0)
    nat = pl.BlockSpec((None, TILE, D), lambda b, i: (b, i, 0))
    return pl.pallas_call(
        _out_kernel,
        grid=(B, L // TILE),
        in_specs=[pl.BlockSpec((None, None, NCB, TILE, FFT_CB), lambda b, i: (b, i, 0, 0, 0)),
                  _gate_spec(D_HY), _gate_spec(D_LRU), _gate_spec(D_LRU), nat,
                  pl.BlockSpec((1, D_HY), const), pl.BlockSpec((1, D_LRU), const),
                  pl.BlockSpec(w_out.shape, const), pl.BlockSpec((1, D), const)],
        out_specs=nat,
        out_shape=jax.ShapeDtypeStruct((B, L, D), f32),
        scratch_shapes=[pltpu.VMEM((D // LANES, TILE, LANES), f32)],
        compiler_params=_params("parallel", "arbitrary"),
        name="out",
    )(yh, hg, yl, lg, x, hog, log_g, w_out, fg)


def kernel(x, norm_g, w_in, hy_conv_w, hy_conv_b, flt_w1, flt_b1, flt_f1, flt_w2, flt_b2, flt_f2,
           flt_w3, flt_b3, flt_f3, flt_w4, hy_skip, lru_conv_w, lru_conv_b, lru_wa, lru_ba, lru_wx,
           lru_bx, lru_lam, hy_out_g, lru_out_g, w_out, final_g):
    B, L, D = x.shape
    assert norm_g.shape[0] == 1, "one layer"
    assert B % 2 == 0 and L % TILE == 0
    ha = L // NB
    na = 2 * ha
    row = lambda v: v.reshape(1, -1)

    u, x0, hg, xb, lg = _inproj(x, row(norm_g[0]), w_in[0].astype(bf16), hy_conv_w[0], row(hy_conv_b[0]),
                                lru_conv_w[0], row(lru_conv_b[0]))
    pair = lambda t: t.reshape(B // 2, NCB, NB, na, FFT_CB)

    m_data, m_real, m_inv, g_fwd, g_inv = (jnp.asarray(m).astype(bf16) for m in _dft_matrices(L))
    kt, l1 = _filter(L, flt_w1[0], flt_b1[0], flt_f1[0], flt_w2[0], flt_b2[0], flt_f2[0],
                     flt_w3[0], flt_b3[0], flt_f3[0], flt_w4[0])
    kf = _fft_k(kt, l1, row(hy_skip[0]), m_real, g_fwd, scale=1.0 / (2 * L))
    yh = _fft_conv(pair(u), pair(x0), kf, m_data, m_inv, g_fwd, g_inv)

    yl = _lru(xb, lru_wa[0], lru_ba[0], lru_wx[0], lru_bx[0], lru_lam[0])
    return _out(yh, hg, yl, lg, x, row(hy_out_g[0]), row(lru_out_g[0]), w_out[0].astype(bf16), row(final_g))
```

```python
import functools
import math

import jax
import jax.numpy as jnp
import numpy as np
from jax import lax
from jax.experimental import pallas as pl
from jax.experimental.pallas import tpu as pltpu

f32 = jnp.float32
bf16 = jnp.bfloat16

D_HY = 768
D_LRU = 768
HEAD = 64
LANES = 128
SUBLANES = 8
NB = 128
TILE = SUBLANES * NB
HALO = SUBLANES
FFT_CB = 256
NCB = D_HY // FFT_CB
FFT_BA = 16
FFT_PB = 32
FFT_BC = 16
FFT_PITCH = NB + SUBLANES
FFTK_BA = 32
FFTK_PB = 32
LRU_BT = 32
FILTER_BANDS = 16
FILTER_EMB = 2 * FILTER_BANDS + 1
MASK_COL = FILTER_EMB
FILTER_TARGET = 1e-2
MIN_DECAY = math.log(FILTER_TARGET) / 0.3
MAX_DECAY = math.log(FILTER_TARGET) / 1.5
LRU_C = 8.0
EPS = 1e-6
VMEM_LIMIT = 60 * 1024 * 1024


def _params(*sem):
    return pltpu.CompilerParams(dimension_semantics=sem, vmem_limit_bytes=VMEM_LIMIT)


def _rms(y, g):
    return y * lax.rsqrt(jnp.mean(y * y, axis=-1, keepdims=True) + EPS) * g


def _put_cols(ref, rows, val):
    for h in range(ref.shape[0]):
        ref[h, rows, :] = val[:, h * LANES:(h + 1) * LANES]


def _get_cols(ref, rows):
    return jnp.concatenate([ref[h, rows, :] for h in range(ref.shape[0])], axis=-1)


def _dot3(a, b):
    ah = a.astype(bf16)
    al = (a - ah.astype(f32)).astype(bf16)
    bh = b.astype(bf16)
    bl = (b - bh.astype(f32)).astype(bf16)
    dot = functools.partial(jnp.dot, preferred_element_type=f32)
    return dot(ah, bh) + (dot(ah, bl) + dot(al, bh))


def _sigmoid(x):
    return 0.5 * jnp.tanh(0.5 * x) + 0.5


def _inproj_kernel(x_ref, xp_ref, xn_ref, g_ref, w_ref, hcw_ref, hcb_ref, lcw_ref, lcb_ref,
                   u_ref, x0_ref, hg_ref, xb_ref, lg_ref, xs_ref, *, n_tiles):
    i = pl.program_id(1)
    g = g_ref[...]
    for a in range(SUBLANES):
        _put_cols(xs_ref, pl.ds(a, NB, stride=SUBLANES), _rms(x_ref[a * NB:(a + 1) * NB, :], g))
    _put_cols(xs_ref, pl.ds(TILE, HALO), jnp.where(i > 0, _rms(xp_ref[...], g), 0.0))
    _put_cols(xs_ref, pl.ds(TILE + HALO, HALO), jnp.where(i < n_tiles - 1, _rms(xn_ref[...], g), 0.0))
    xn = _get_cols(xs_ref, pl.ds(0, TILE + 2 * HALO)).astype(bf16)
    sub = lax.broadcasted_iota(jnp.int32, (SUBLANES, D_HY), 0)

    def proj(c0):
        p = jnp.dot(xn, w_ref[:, c0:c0 + D_HY], preferred_element_type=f32)
        return p.reshape(NB + 2, SUBLANES, D_HY)

    def edge(p3, s):
        if s < 0:
            return jnp.where(sub == 0, p3[NB][SUBLANES + s:SUBLANES + s + 1],
                             pltpu.roll(p3[NB + s], 1, 0))
        return jnp.where(sub == SUBLANES - 1, p3[NB + 1][s - NB:s - NB + 1],
                         pltpu.roll(p3[s - NB], SUBLANES - 1, 0))

    def conv(p3, cw_ref, cb_ref, c0, offsets):
        lo, hi = max(0, -min(offsets)), NB - max(offsets)

        def acc(get):
            y = cb_ref[:, c0:c0 + D_HY]
            for k, o in enumerate(offsets):
                y = y + get(o) * cw_ref[k:k + 1, c0:c0 + D_HY]
            return y

        inner = acc(lambda o: p3[lo + o:hi + o])
        edges = {b: acc(lambda o, b=b: p3[b + o] if 0 <= b + o < NB else edge(p3, b + o))
                 for b in list(range(lo)) + list(range(hi, NB))}
        return lo, hi, inner, edges

    def store(ref, conv_out, other=None):
        lo, hi, inner, edges = conv_out
        if other is not None:
            inner = inner * other[2]
            edges = {b: edges[b] * other[3][b] for b in edges}
        if len(ref.shape) == 3:
            ref[lo:hi] = inner
            for b, y in edges.items():
                ref[b] = y
        else:
            for cb in range(NCB):
                cols = slice(cb * FFT_CB, (cb + 1) * FFT_CB)
                ref[cb, lo:hi] = inner[:, :, cols]
                for b, y in edges.items():
                    ref[cb, b] = y[:, cols]

    hy = (-1, 0, 1)
    store(u_ref, conv(proj(0), hcw_ref, hcb_ref, 0, hy), conv(proj(2 * D_HY), hcw_ref, hcb_ref, 2 * D_HY, hy))
    store(x0_ref, conv(proj(D_HY), hcw_ref, hcb_ref, D_HY, hy))
    store(xb_ref, conv(proj(4 * D_HY), lcw_ref, lcb_ref, 0, (-1, 0, 1, 2)))
    xm = xn[:TILE]
    hg = jnp.dot(xm, w_ref[:, 3 * D_HY:4 * D_HY], preferred_element_type=f32)
    hg_ref[...] = (hg * _sigmoid(hg)).astype(bf16)
    lg = jnp.dot(xm, w_ref[:, 4 * D_HY + D_LRU:], preferred_element_type=f32)
    lg_ref[...] = (lg * _sigmoid(lg)).astype(bf16)


def _ba_spec(c):
    return pl.BlockSpec((None, NB, None, SUBLANES, c), lambda b, i: (b // 2, 0, b % 2, i, 0))


def _gate_spec(c):
    return pl.BlockSpec((None, None, TILE, c), lambda b, i: (b, i, 0, 0))


def _inproj(x, norm_g, w_in, hcw, hcb, lcw, lcb):
    B, L, D = x.shape
    n_tiles = L // TILE
    ha = L // NB
    hb = TILE // HALO
    n_hb = L // HALO
    const = lambda b, i: (0, 0)
    ba_shape = jax.ShapeDtypeStruct((B // 2, NB, 2, ha, D_HY), f32)
    cb_shape = jax.ShapeDtypeStruct((B // 2, NCB, NB, 2, ha, FFT_CB), f32)
    cb_spec = pl.BlockSpec((None, NCB, NB, None, SUBLANES, FFT_CB), lambda b, i: (b // 2, 0, 0, b % 2, i, 0))
    gate_shape = jax.ShapeDtypeStruct((B, n_tiles, TILE, D_HY), bf16)
    return pl.pallas_call(
        functools.partial(_inproj_kernel, n_tiles=n_tiles),
        grid=(B, n_tiles),
        in_specs=[
            pl.BlockSpec((None, TILE, D), lambda b, i: (b, i, 0)),
            pl.BlockSpec((None, HALO, D), lambda b, i: (b, jnp.maximum(i * hb - 1, 0), 0)),
            pl.BlockSpec((None, HALO, D), lambda b, i: (b, jnp.minimum((i + 1) * hb, n_hb - 1), 0)),
            pl.BlockSpec((1, D), const),
            pl.BlockSpec(w_in.shape, const, pipeline_mode=pl.Buffered(1)),
            pl.BlockSpec(hcw.shape, const),
            pl.BlockSpec(hcb.shape, const),
            pl.BlockSpec(lcw.shape, const),
            pl.BlockSpec(lcb.shape, const),
        ],
        out_specs=[cb_spec, cb_spec, _gate_spec(D_HY), _ba_spec(D_LRU), _gate_spec(D_LRU)],
        out_shape=[cb_shape, cb_shape, gate_shape, ba_shape, gate_shape],
        scratch_shapes=[pltpu.VMEM((D // LANES, TILE + 2 * HALO, LANES), f32)],
        compiler_params=_params("parallel", "arbitrary"),
        name="inproj",
    )(x, x, x, norm_g, w_in, hcw, hcb, lcw, lcb)


def _filt_kernel(zf_ref, zb_ref, dl_ref, w1_ref, b1_ref, f1_ref, w2_ref, b2_ref, f2_ref,
                 w3_ref, b3_ref, f3_ref, w4f_ref, w4b_ref, o_ref, s_ref, *, ha):
    dot = _dot3
    dl = dl_ref[...]
    zf = zf_ref[...]
    zb = zb_ref[...]
    h = jnp.concatenate([dot(zf, w1_ref[...]), dot(zb, w1_ref[...])], axis=-1)
    h = jnp.sin(f1_ref[...] * (h + b1_ref[...]))
    h = jnp.sin(f2_ref[...] * (dot(h, w2_ref[...]) + b2_ref[...]))
    h = jnp.sin(f3_ref[...] * (dot(h, w3_ref[...]) + b3_ref[...]))
    nh = h.shape[-1] // 2
    hf = dot(h[:, :nh], w4f_ref[...]) * jnp.exp(-zf[:, 0:1] * dl)
    hb = dot(h[:, nh:], w4b_ref[...]) * jnp.exp(-zb[:, 0:1] * dl) * zb[:, MASK_COL:MASK_COL + 1]
    for cb in range(NCB):
        cols = slice(cb * FFT_CB, (cb + 1) * FFT_CB)
        o_ref[cb, :, :ha, :] = hf[:, cols].reshape(SUBLANES, ha, FFT_CB)
        o_ref[cb, :, ha:, :] = hb[:, cols].reshape(SUBLANES, ha, FFT_CB)

    @pl.when(pl.program_id(0) == 0)
    def _():
        s_ref[...] = jnp.zeros_like(s_ref)

    s_ref[...] += jnp.sum(jnp.abs(hf), axis=0, keepdims=True) + jnp.sum(jnp.abs(hb), axis=0, keepdims=True)


@functools.lru_cache(maxsize=None)
def _filter_tables(L):
    ha = L // NB
    t = np.linspace(0.0, 1.0, L)
    w = (2.0 * math.pi / L) * np.arange(L)
    f = np.linspace(1e-4, FILTER_BANDS - 1, FILTER_BANDS)[None, :]

    def features(lag, mask):
        tl, wl = t[lag][:, None], w[lag][:, None]
        pad = np.zeros((lag.shape[0], LANES - FILTER_EMB - 1))
        z = np.concatenate([tl, np.cos(wl * f), -np.sin(wl * f), mask[:, None], pad], axis=-1)
        return z.astype(np.float32)

    b = np.arange(NB)[:, None]
    a = np.arange(ha)[None, :]
    lag_f = (NB * a + b).reshape(-1)
    lag_b = (L - NB * a - b).reshape(-1)
    valid = lag_b < L
    dl = np.abs(np.linspace(MIN_DECAY, MAX_DECAY, D_HY))[None, :].astype(np.float32)
    return (features(lag_f, np.zeros(lag_f.shape)),
            features(np.where(valid, lag_b, 0), valid.astype(np.float64)), dl)


def _filter(L, w1, b1, f1, w2, b2, f2, w3, b3, f3, w4):
    ha = L // NB
    rc = SUBLANES * ha
    zf, zb, dl = (jnp.asarray(t) for t in _filter_tables(L))
    w1p = jnp.pad(w1, ((0, LANES - w1.shape[0]), (0, 0)))
    row = lambda v: jnp.tile(v.reshape(1, -1), (1, 2))
    zero = jnp.zeros_like(w2)
    diag2 = lambda w: jnp.concatenate([jnp.concatenate([w, zero], axis=1),
                                       jnp.concatenate([zero, w], axis=1)], axis=0)
    const = lambda j: (0, 0)
    full = lambda arr: pl.BlockSpec(arr.shape, const)
    args = [zf, zb, dl, w1p, row(b1), row(f1), diag2(w2), row(b2), row(f2), diag2(w3), row(b3), row(f3), w4, w4]
    specs = [full(a) for a in args]
    specs[0] = specs[1] = pl.BlockSpec((rc, LANES), lambda j: (j, 0))
    specs[12] = pl.BlockSpec((w4.shape[0], D_HY), lambda j: (0, 0))
    specs[13] = pl.BlockSpec((w4.shape[0], D_HY), lambda j: (0, 1))
    return pl.pallas_call(
        functools.partial(_filt_kernel, ha=ha),
        grid=(NB // SUBLANES,),
        in_specs=specs,
        out_specs=[pl.BlockSpec((NCB, SUBLANES, 2 * ha, FFT_CB), lambda j: (0, j, 0, 0)),
                   pl.BlockSpec((1, D_HY), const)],
        out_shape=[jax.ShapeDtypeStruct((NCB, NB, 2 * ha, FFT_CB), f32),
                   jax.ShapeDtypeStruct((1, D_HY), f32)],
        compiler_params=_params("arbitrary"),
        name="filt",
    )(*args)


@functools.lru_cache(maxsize=None)
def _dft_matrices(L):
    n = 2 * L
    na = n // NB
    ha = na // 2
    b = np.arange(NB)[:, None, None]
    p = np.arange(na)[None, :, None]
    a = np.arange(na)[None, None, :]
    ang = (2.0 * math.pi / n) * ((p * (NB * a + b)) % n)
    c, s = np.cos(ang), np.sin(ang)
    ch, sh = c[:, :, :ha], s[:, :, :ha]
    m_data = np.concatenate([np.concatenate([ch, sh], axis=2),
                             np.concatenate([-sh, ch], axis=2)], axis=1)
    m_real = np.concatenate([c, -s], axis=1)
    cht, sht = np.swapaxes(ch, 1, 2), np.swapaxes(sh, 1, 2)
    m_inv = np.concatenate([np.concatenate([cht, -sht], axis=2),
                            np.concatenate([sht, cht], axis=2)], axis=1)
    q = np.arange(NB)
    gang = (2.0 * math.pi / NB) * ((q[:, None] * q[None, :]) % NB)
    gc, gs = np.cos(gang), np.sin(gang)
    g_fwd = np.concatenate([np.concatenate([gc, gs], axis=1),
                            np.concatenate([-gs, gc], axis=1)], axis=0)
    g_inv = np.concatenate([np.concatenate([gc, -gs], axis=1),
                            np.concatenate([gs, gc], axis=1)], axis=0)
    return tuple(m.astype(np.float32) for m in (m_data, m_real, m_inv, g_fwd, g_inv))


def _first_stage(x_ref, m_ref, s_ref, s, na):
    nb = x_ref.shape[0]
    for j in range(nb):
        res = jnp.dot(m_ref[j], x_ref[j].astype(bf16), preferred_element_type=f32)
        rows = pl.ds(s * nb + j, na, stride=FFT_PITCH)
        _put_cols(s_ref.at[0], rows, res[:na])
        _put_cols(s_ref.at[1], rows, res[na:])


def _slab(p):
    return pl.ds(pl.multiple_of(p * FFT_PITCH, SUBLANES), NB)


def _second_stage(s_ref, gf_ref, p):
    rows = _slab(p)
    y = jnp.concatenate([_get_cols(s_ref.at[0], rows), _get_cols(s_ref.at[1], rows)], axis=0)
    return jnp.dot(gf_ref[...], y.astype(bf16), preferred_element_type=f32)


def _fftk_kernel(k_ref, ma_ref, l1_ref, sk_ref, gf_ref, o_ref, s_ref, *, na, a_steps, scale):
    s = pl.program_id(1)

    @pl.when(s < a_steps)
    def _():
        _first_stage(k_ref, ma_ref, s_ref, s, na)

    @pl.when(s >= a_steps)
    def _():
        inv = scale / (l1_ref[...] + EPS)
        tap = scale * sk_ref[...]
        for j in range(FFTK_PB):
            z = _second_stage(s_ref, gf_ref, (s - a_steps) * FFTK_PB + j)
            o_ref[0, j] = (z[:NB] * inv + tap).astype(o_ref.dtype)
            o_ref[1, j] = (z[NB:] * inv).astype(o_ref.dtype)


def _fft_k(kt, l1, skip, m_real, g_fwd, scale):
    _, _, na, _ = kt.shape
    a_steps, b_steps = NB // FFTK_BA, na // FFTK_PB
    last_a = a_steps - 1
    const = lambda c, s: (0, 0)
    return pl.pallas_call(
        functools.partial(_fftk_kernel, na=na, a_steps=a_steps, scale=scale),
        grid=(NCB, a_steps + b_steps),
        in_specs=[pl.BlockSpec((None, FFTK_BA, na, FFT_CB), lambda c, s: (c, jnp.minimum(s, last_a), 0, 0)),
                  pl.BlockSpec((FFTK_BA, 2 * na, na), lambda c, s: (jnp.minimum(s, last_a), 0, 0)),
                  pl.BlockSpec((1, FFT_CB), lambda c, s: (0, c)),
                  pl.BlockSpec((1, FFT_CB), lambda c, s: (0, c)),
                  pl.BlockSpec(g_fwd.shape, const)],
        out_specs=pl.BlockSpec((2, None, FFTK_PB, NB, FFT_CB),
                               lambda c, s: (0, c, jnp.clip(s - a_steps, 0, b_steps - 1), 0, 0)),
        out_shape=jax.ShapeDtypeStruct((2, NCB, na, NB, FFT_CB), bf16),
        scratch_shapes=[pltpu.VMEM((2, FFT_CB // LANES, na * FFT_PITCH, LANES), f32)],
        compiler_params=_params("arbitrary", "arbitrary"),
        name="fft_k",
    )(kt, m_real, l1, skip, g_fwd)


def _fftconv_kernel(ua_ref, ma_ref, k_ref, gf_ref, gi_ref, mc_ref, x0_ref, o_ref, s_ref,
                    *, na, a_steps, b_steps):
    s = pl.program_id(2)

    @pl.when(s < a_steps)
    def _():
        _first_stage(ua_ref, ma_ref, s_ref, s, na)

    @pl.when((s >= a_steps) & (s < a_steps + b_steps))
    def _():
        for j in range(FFT_PB):
            p = (s - a_steps) * FFT_PB + j
            z = _second_stage(s_ref, gf_ref, p)
            zr, zi = z[:NB], z[NB:]
            kr, ki = k_ref[0, j], k_ref[1, j]
            f = jnp.concatenate([zr * kr - zi * ki, zr * ki + zi * kr], axis=0)
            v = jnp.dot(gi_ref[...], f.astype(bf16), preferred_element_type=f32)
            _put_cols(s_ref.at[0], _slab(p), v[:NB])
            _put_cols(s_ref.at[1], _slab(p), v[NB:])

    @pl.when(s >= a_steps + b_steps)
    def _():
        def last_stage(j):
            rows = pl.ds((s - (a_steps + b_steps)) * FFT_BC + j, na, stride=FFT_PITCH)
            v = jnp.concatenate([_get_cols(s_ref.at[0], rows), _get_cols(s_ref.at[1], rows)], axis=0)
            conv = jnp.dot(mc_ref[j], v.astype(bf16), preferred_element_type=f32)
            return x0_ref[j] * conv

        ha = na // 2
        for jj in range(FFT_BC // 2):
            y0, y1 = last_stage(2 * jj), last_stage(2 * jj + 1)
            for half in range(2):
                for t in range(ha // SUBLANES):
                    r0 = half * ha + t * SUBLANES
                    y = jnp.concatenate([y0[r0:r0 + SUBLANES], y1[r0:r0 + SUBLANES]], axis=0)
                    o_ref[half, t, 2 * SUBLANES * jj:2 * SUBLANES * (jj + 1), :] = y.astype(bf16)


def _fft_conv(u, x0, kf, m_data, m_inv, g_fwd, g_inv):
    P, _, _, na, _ = u.shape
    a_steps, b_steps, c_steps = NB // FFT_BA, na // FFT_PB, NB // FFT_BC
    n_tiles = na // (2 * SUBLANES)
    last_a = a_steps - 1
    b_idx = lambda s: jnp.clip(s - a_steps, 0, b_steps - 1)
    c_idx = lambda s: jnp.clip(s - (a_steps + b_steps), 0, c_steps - 1)
    const = lambda c, r, s: (0, 0)
    c_spec = pl.BlockSpec((None, None, FFT_BC, na, FFT_CB), lambda c, r, s: (r, c, c_idx(s), 0, 0))
    return pl.pallas_call(
        functools.partial(_fftconv_kernel, na=na, a_steps=a_steps, b_steps=b_steps),
        grid=(NCB, P, a_steps + b_steps + c_steps),
        in_specs=[pl.BlockSpec((None, None, FFT_BA, na, FFT_CB), lambda c, r, s: (r, c, jnp.minimum(s, last_a), 0, 0)),
                  pl.BlockSpec((FFT_BA, 2 * na, na), lambda c, r, s: (jnp.minimum(s, last_a), 0, 0)),
                  pl.BlockSpec((2, None, FFT_PB, NB, FFT_CB), lambda c, r, s: (0, c, b_idx(s), 0, 0)),
                  pl.BlockSpec(g_fwd.shape, const),
                  pl.BlockSpec(g_inv.shape, const),
                  pl.BlockSpec((FFT_BC, na, 2 * na), lambda c, r, s: (c_idx(s), 0, 0)),
                  c_spec],
        out_specs=pl.BlockSpec((2, n_tiles, None, FFT_BC * SUBLANES, FFT_CB), lambda c, r, s: (r, 0, c, c_idx(s), 0)),
        out_shape=jax.ShapeDtypeStruct((2 * P, n_tiles, NCB, TILE, FFT_CB), bf16),
        scratch_shapes=[pltpu.VMEM((2, FFT_CB // LANES, na * FFT_PITCH, LANES), f32)],
        compiler_params=_params("arbitrary", "arbitrary", "arbitrary"),
        name="fft_conv",
    )(u, m_data, kf, g_fwd, g_inv, m_inv, x0)


def _carry_scan(at, bt, reverse):
    n = at.shape[0]
    row = lax.broadcasted_iota(jnp.int32, at.shape, 0)
    s = 1
    while s < n:
        keep = (row < n - s) if reverse else (row >= s)
        shift = n - s if reverse else s
        ash = jnp.where(keep, pltpu.roll(at, shift, 0), 1.0)
        bsh = jnp.where(keep, pltpu.roll(bt, shift, 0), 0.0)
        bt = at * bsh + bt
        at = at * ash
        s *= 2
    if reverse:
        return jnp.where(row < n - 1, pltpu.roll(bt, n - 1, 0), 0.0)
    return jnp.where(row >= 1, pltpu.roll(bt, 1, 0), 0.0)


def _lru_kernel(xb_ref, wg_ref, bg_ref, lam_ref, o_ref, hf_ref, af_ref, hb_ref, ab_ref, *, ha):
    bt = LRU_BT
    nc = NB // bt
    lam = lam_ref[...]
    half_c = (-0.5 * LRU_C) * (jnp.maximum(-lam, 0.0) + jnp.log1p(jnp.exp(-jnp.abs(lam))))
    rowi = lax.broadcasted_iota(jnp.int32, (bt * ha, 1), 0)

    def gates(d, k, first):
        b0 = pl.multiple_of(k * bt, bt)
        xc = xb_ref[pl.ds(b0, bt)].reshape(bt * ha, LANES)
        t = jnp.tanh(jnp.dot(xc.astype(bf16), wg_ref[d], preferred_element_type=f32) + bg_ref[d:d + 1, :])
        hc = half_c[d:d + 1, :]
        log_a = hc * t[:, :LANES] + hc
        a = jnp.exp(log_a)
        m2 = jnp.tanh(log_a) * (-1.0 - a * a)
        mult = jnp.where(m2 > 0.0, m2 * lax.rsqrt(m2), 0.0)
        if first:
            mult = jnp.where(rowi == (bt * ha - 1 if d else 0), 1.0, mult)
        hx = 0.5 * xc
        bv = mult * (t[:, LANES:] * hx + hx)
        return b0, a.reshape(bt, ha, LANES), bv.reshape(bt, ha, LANES)

    def step(k, carry, first=False):
        hf, af, hb, ab = carry
        b0, a, bv = gates(0, k, first)
        for j in range(bt):
            hf = a[j] * hf + bv[j]
            af = a[j] * af
            hf_ref[b0 + j] = hf
            af_ref[b0 + j] = af
        b0, a, bv = gates(1, nc - 1 - k, first)
        for j in reversed(range(bt)):
            hb = a[j] * hb + bv[j]
            ab = a[j] * ab
            hb_ref[b0 + j] = hb
            ab_ref[b0 + j] = ab
        return hf, af, hb, ab

    zero = jnp.zeros((ha, LANES), f32)
    one = jnp.ones((ha, LANES), f32)
    carry = step(0, (zero, one, zero, one), first=True)
    hf, af, hb, ab = lax.fori_loop(1, nc, step, carry)
    cf = _carry_scan(af, hf, reverse=False)
    cb = _carry_scan(ab, hb, reverse=True)

    fb = SUBLANES

    def finish(k, c):
        sl = pl.ds(pl.multiple_of(k * fb, fb), fb)
        h = (hf_ref[sl] + af_ref[sl] * cf) + (hb_ref[sl] + ab_ref[sl] * cb)
        r0 = pl.multiple_of(k * (fb * SUBLANES), fb * SUBLANES)
        for t in range(ha // SUBLANES):
            tile = h[:, t * SUBLANES:(t + 1) * SUBLANES, :].reshape(fb * SUBLANES, LANES)
            o_ref[t, pl.ds(r0, fb * SUBLANES), :] = tile.astype(bf16)
        return c

    lax.fori_loop(0, NB // fb, finish, 0)


def _lru_gate_weights(wa, wx):
    def blockdiag(w):
        w = w.reshape(2, -1, 2, HEAD, HEAD)
        z = jnp.zeros_like(w[:, :, 0])
        top = jnp.concatenate([w[:, :, 0], z], axis=-1)
        bot = jnp.concatenate([z, w[:, :, 1]], axis=-1)
        return jnp.concatenate([top, bot], axis=-2)
    return jnp.concatenate([blockdiag(wa), blockdiag(wx)], axis=-1).astype(bf16)


def _lru(xb, wa, ba, wx, bx, lam):
    P, _, _, ha, C = xb.shape
    nblk = C // LANES
    wg = _lru_gate_weights(0.5 * wa, 0.5 * wx)
    bg = 0.5 * jnp.concatenate([ba.reshape(2, nblk, 1, LANES), bx.reshape(2, nblk, 1, LANES)], axis=-1)
    ba_spec = pl.BlockSpec((None, NB, None, ha, LANES), lambda b, c: (b // 2, 0, b % 2, 0, c))
    return pl.pallas_call(
        functools.partial(_lru_kernel, ha=ha),
        grid=(2 * P, nblk),
        in_specs=[ba_spec,
                  pl.BlockSpec((2, None, LANES, 2 * LANES), lambda b, c: (0, c, 0, 0)),
                  pl.BlockSpec((2, None, None, 2 * LANES), lambda b, c: (0, c, 0, 0)),
                  pl.BlockSpec((2, LANES), lambda b, c: (0, c))],
        out_specs=pl.BlockSpec((None, ha // SUBLANES, TILE, LANES), lambda b, c: (b, 0, 0, c)),
        out_shape=jax.ShapeDtypeStruct((2 * P, ha // SUBLANES, TILE, C), bf16),
        scratch_shapes=[pltpu.VMEM((NB, ha, LANES), f32)] * 4,
        compiler_params=_params("parallel", "arbitrary"),
        name="lru",
    )(xb, wg, bg, lam)


def _out_kernel(yh_ref, hg_ref, yl_ref, lg_ref, x_ref, hog_ref, log_ref, wo_ref, fg_ref, o_ref, ys_ref):
    yh = jnp.concatenate([yh_ref[cb] for cb in range(NCB)], axis=-1)
    ycat = jnp.concatenate([_rms(yh.astype(f32), hog_ref[...]) * hg_ref[...],
                            _rms(yl_ref[...].astype(f32), log_ref[...]) * lg_ref[...]], axis=-1)
    y = jnp.dot(ycat.astype(bf16), wo_ref[...], preferred_element_type=f32)
    _put_cols(ys_ref, pl.ds(0, TILE), y)
    for a in range(SUBLANES):
        rows = slice(a * NB, (a + 1) * NB)
        ya = _get_cols(ys_ref, pl.ds(a, NB, stride=SUBLANES))
        o_ref[rows, :] = _rms(x_ref[rows, :] + ya, fg_ref[...])


def _out(yh, hg, yl, lg, x, hog, log_g, w_out, fg):
    B, L, D = x.shape
    const = lambda b, i: (0, 0)
    nat = pl.BlockSpec((None, TILE, D), lambda b, i: (b, i, 0))
    return pl.pallas_call(
        _out_kernel,
        grid=(B, L // TILE),
        in_specs=[pl.BlockSpec((None, None, NCB, TILE, FFT_CB), lambda b, i: (b, i, 0, 0, 0)),
                  _gate_spec(D_HY), _gate_spec(D_LRU), _gate_spec(D_LRU), nat,
                  pl.BlockSpec((1, D_HY), const), pl.BlockSpec((1, D_LRU), const),
                  pl.BlockSpec(w_out.shape, const), pl.BlockSpec((1, D), const)],
        out_specs=nat,
        out_shape=jax.ShapeDtypeStruct((B, L, D), f32),
        scratch_shapes=[pltpu.VMEM((D // LANES, TILE, LANES), f32)],
        compiler_params=_params("parallel", "arbitrary"),
        name="out",
    )(yh, hg, yl, lg, x, hog, log_g, w_out, fg)


def kernel(x, norm_g, w_in, hy_conv_w, hy_conv_b, flt_w1, flt_b1, flt_f1, flt_w2, flt_b2, flt_f2,
           flt_w3, flt_b3, flt_f3, flt_w4, hy_skip, lru_conv_w, lru_conv_b, lru_wa, lru_ba, lru_wx,
           lru_bx, lru_lam, hy_out_g, lru_out_g, w_out, final_g):
    B, L, D = x.shape
    assert norm_g.shape[0] == 1, "one layer"
    assert B % 2 == 0 and L % TILE == 0
    ha = L // NB
    na = 2 * ha
    row = lambda v: v.reshape(1, -1)

    u, x0, hg, xb, lg = _inproj(x, row(norm_g[0]), w_in[0].astype(bf16), hy_conv_w[0], row(hy_conv_b[0]),
                                lru_conv_w[0], row(lru_conv_b[0]))
    pair = lambda t: t.reshape(B // 2, NCB, NB, na, FFT_CB)

    m_data, m_real, m_inv, g_fwd, g_inv = (jnp.asarray(m).astype(bf16) for m in _dft_matrices(L))
    kt, l1 = _filter(L, flt_w1[0], flt_b1[0], flt_f1[0], flt_w2[0], flt_b2[0], flt_f2[0],
                     flt_w3[0], flt_b3[0], flt_f3[0], flt_w4[0])
    kf = _fft_k(kt, l1, row(hy_skip[0]), m_real, g_fwd, scale=1.0 / (2 * L))
    yh = _fft_conv(pair(u), pair(x0), kf, m_data, m_inv, g_fwd, g_inv)

    yl = _lru(xb, lru_wa[0], lru_ba[0], lru_wx[0], lru_bx[0], lru_lam[0])
    return _out(yh, hg, yl, lg, x, row(hy_out_g[0]), row(lru_out_g[0]), w_out[0].astype(bf16), row(final_g))
```

```python
import functools
import math

import jax
import jax.numpy as jnp
import numpy as np
from jax import lax
from jax.experimental import pallas as pl
from jax.experimental.pallas import tpu as pltpu

f32 = jnp.float32
bf16 = jnp.bfloat16

D_HY = 768
D_LRU = 768
HEAD = 64
LANES = 128
SUBLANES = 8
NB = 128
TILE = SUBLANES * NB
HALO = SUBLANES
FFT_CB = 256
NCB = D_HY // FFT_CB
FFT_BA = 16
FFT_PB = 32
FFT_BC = 16
FFT_PITCH = NB + SUBLANES
FFTK_BA = 32
FFTK_PB = 32
LRU_BT = 32
FILTER_BANDS = 16
FILTER_EMB = 2 * FILTER_BANDS + 1
MASK_COL = FILTER_EMB
FILTER_TARGET = 1e-2
MIN_DECAY = math.log(FILTER_TARGET) / 0.3
MAX_DECAY = math.log(FILTER_TARGET) / 1.5
LRU_C = 8.0
EPS = 1e-6
VMEM_LIMIT = 60 * 1024 * 1024


def _params(*sem):
    return pltpu.CompilerParams(dimension_semantics=sem, vmem_limit_bytes=VMEM_LIMIT)


def _rms(y, g):
    return y * lax.rsqrt(jnp.mean(y * y, axis=-1, keepdims=True) + EPS) * g


def _put_cols(ref, rows, val):
    for h in range(ref.shape[0]):
        ref[h, rows, :] = val[:, h * LANES:(h + 1) * LANES]


def _get_cols(ref, rows):
    return jnp.concatenate([ref[h, rows, :] for h in range(ref.shape[0])], axis=-1)


def _dot3(a, b):
    ah = a.astype(bf16)
    al = (a - ah.astype(f32)).astype(bf16)
    bh = b.astype(bf16)
    bl = (b - bh.astype(f32)).astype(bf16)
    dot = functools.partial(jnp.dot, preferred_element_type=f32)
    return dot(ah, bh) + (dot(ah, bl) + dot(al, bh))


def _sigmoid(x):
    return 0.5 * jnp.tanh(0.5 * x) + 0.5


def _inproj_kernel(x_ref, xp_ref, xn_ref, g_ref, w_ref, hcw_ref, hcb_ref, lcw_ref, lcb_ref,
                   u_ref, x0_ref, hg_ref, xb_ref, lg_ref, xs_ref, *, n_tiles):
    i = pl.program_id(1)
    g = g_ref[...]
    for a in range(SUBLANES):
        _put_cols(xs_ref, pl.ds(a, NB, stride=SUBLANES), _rms(x_ref[a * NB:(a + 1) * NB, :], g))
    _put_cols(xs_ref, pl.ds(TILE, HALO), jnp.where(i > 0, _rms(xp_ref[...], g), 0.0))
    _put_cols(xs_ref, pl.ds(TILE + HALO, HALO), jnp.where(i < n_tiles - 1, _rms(xn_ref[...], g), 0.0))
    xn = _get_cols(xs_ref, pl.ds(0, TILE + 2 * HALO)).astype(bf16)
    sub = lax.broadcasted_iota(jnp.int32, (SUBLANES, D_HY), 0)

    def proj(c0):
        p = jnp.dot(xn, w_ref[:, c0:c0 + D_HY], preferred_element_type=f32)
        return p.reshape(NB + 2, SUBLANES, D_HY)

    def edge(p3, s):
        if s < 0:
            return jnp.where(sub == 0, p3[NB][SUBLANES + s:SUBLANES + s + 1],
                             pltpu.roll(p3[NB + s], 1, 0))
        return jnp.where(sub == SUBLANES - 1, p3[NB + 1][s - NB:s - NB + 1],
                         pltpu.roll(p3[s - NB], SUBLANES - 1, 0))

    def conv(p3, cw_ref, cb_ref, c0, offsets):
        lo, hi = max(0, -min(offsets)), NB - max(offsets)

        def acc(get):
            y = cb_ref[:, c0:c0 + D_HY]
            for k, o in enumerate(offsets):
                y = y + get(o) * cw_ref[k:k + 1, c0:c0 + D_HY]
            return y

        inner = acc(lambda o: p3[lo + o:hi + o])
        edges = {b: acc(lambda o, b=b: p3[b + o] if 0 <= b + o < NB else edge(p3, b + o))
                 for b in list(range(lo)) + list(range(hi, NB))}
        return lo, hi, inner, edges

    def store(ref, conv_out, other=None):
        lo, hi, inner, edges = conv_out
        if other is not None:
            inner = inner * other[2]
            edges = {b: edges[b] * other[3][b] for b in edges}
        if len(ref.shape) == 3:
            ref[lo:hi] = inner
            for b, y in edges.items():
                ref[b] = y
        else:
            for cb in range(NCB):
                cols = slice(cb * FFT_CB, (cb + 1) * FFT_CB)
                ref[cb, lo:hi] = inner[:, :, cols]
                for b, y in edges.items():
                    ref[cb, b] = y[:, cols]

    hy = (-1, 0, 1)
    store(u_ref, conv(proj(0), hcw_ref, hcb_ref, 0, hy), conv(proj(2 * D_HY), hcw_ref, hcb_ref, 2 * D_HY, hy))
    lo, hi, inner, edges = conv(proj(D_HY), hcw_ref, hcb_ref, D_HY, hy)
    x0 = jnp.concatenate([edges[b] for b in range(lo)] + [inner.reshape((hi - lo) * SUBLANES, D_HY)]
                         + [edges[b] for b in range(hi, NB)], axis=0)
    for cb in range(NCB):
        x0_ref[cb] = x0[:, cb * FFT_CB:(cb + 1) * FFT_CB].astype(bf16)
    store(xb_ref, conv(proj(4 * D_HY), lcw_ref, lcb_ref, 0, (-1, 0, 1, 2)))
    xm = xn[:TILE]
    hg = jnp.dot(xm, w_ref[:, 3 * D_HY:4 * D_HY], preferred_element_type=f32)
    hg_ref[...] = (hg * _sigmoid(hg)).astype(bf16)
    lg = jnp.dot(xm, w_ref[:, 4 * D_HY + D_LRU:], preferred_element_type=f32)
    lg_ref[...] = (lg * _sigmoid(lg)).astype(bf16)


def _ba_spec(c):
    return pl.BlockSpec((None, NB, None, SUBLANES, c), lambda b, i: (b // 2, 0, b % 2, i, 0))


def _gate_spec(c):
    return pl.BlockSpec((None, None, TILE, c), lambda b, i: (b, i, 0, 0))


def _inproj(x, norm_g, w_in, hcw, hcb, lcw, lcb):
    B, L, D = x.shape
    n_tiles = L // TILE
    ha = L // NB
    hb = TILE // HALO
    n_hb = L // HALO
    const = lambda b, i: (0, 0)
    ba_shape = jax.ShapeDtypeStruct((B // 2, NB, 2, ha, D_HY), f32)
    cb_shape = jax.ShapeDtypeStruct((B // 2, NCB, NB, 2, ha, FFT_CB), f32)
    cb_spec = pl.BlockSpec((None, NCB, NB, None, SUBLANES, FFT_CB), lambda b, i: (b // 2, 0, 0, b % 2, i, 0))
    gate_shape = jax.ShapeDtypeStruct((B, n_tiles, TILE, D_HY), bf16)
    return pl.pallas_call(
        functools.partial(_inproj_kernel, n_tiles=n_tiles),
        grid=(B, n_tiles),
        in_specs=[
            pl.BlockSpec((None, TILE, D), lambda b, i: (b, i, 0)),
            pl.BlockSpec((None, HALO, D), lambda b, i: (b, jnp.maximum(i * hb - 1, 0), 0)),
            pl.BlockSpec((None, HALO, D), lambda b, i: (b, jnp.minimum((i + 1) * hb, n_hb - 1), 0)),
            pl.BlockSpec((1, D), const),
            pl.BlockSpec(w_in.shape, const, pipeline_mode=pl.Buffered(1)),
            pl.BlockSpec(hcw.shape, const),
            pl.BlockSpec(hcb.shape, const),
            pl.BlockSpec(lcw.shape, const),
            pl.BlockSpec(lcb.shape, const),
        ],
        out_specs=[cb_spec, pl.BlockSpec((None, None, NCB, TILE, FFT_CB), lambda b, i: (b, i, 0, 0, 0)),
                   _gate_spec(D_HY), _ba_spec(D_LRU), _gate_spec(D_LRU)],
        out_shape=[cb_shape, jax.ShapeDtypeStruct((B, n_tiles, NCB, TILE, FFT_CB), bf16),
                   gate_shape, ba_shape, gate_shape],
        scratch_shapes=[pltpu.VMEM((D // LANES, TILE + 2 * HALO, LANES), f32)],
        compiler_params=_params("parallel", "arbitrary"),
        name="inproj",
    )(x, x, x, norm_g, w_in, hcw, hcb, lcw, lcb)


def _filt_kernel(zf_ref, zb_ref, dl_ref, w1_ref, b1_ref, f1_ref, w2_ref, b2_ref, f2_ref,
                 w3_ref, b3_ref, f3_ref, w4f_ref, w4b_ref, o_ref, s_ref, *, ha):
    dot = _dot3
    dl = dl_ref[...]
    zf = zf_ref[...]
    zb = zb_ref[...]
    h = jnp.concatenate([dot(zf, w1_ref[...]), dot(zb, w1_ref[...])], axis=-1)
    h = jnp.sin(f1_ref[...] * (h + b1_ref[...]))
    h = jnp.sin(f2_ref[...] * (dot(h, w2_ref[...]) + b2_ref[...]))
    h = jnp.sin(f3_ref[...] * (dot(h, w3_ref[...]) + b3_ref[...]))
    nh = h.shape[-1] // 2
    hf = dot(h[:, :nh], w4f_ref[...]) * jnp.exp(-zf[:, 0:1] * dl)
    hb = dot(h[:, nh:], w4b_ref[...]) * jnp.exp(-zb[:, 0:1] * dl) * zb[:, MASK_COL:MASK_COL + 1]
    for cb in range(NCB):
        cols = slice(cb * FFT_CB, (cb + 1) * FFT_CB)
        o_ref[cb, :, :ha, :] = hf[:, cols].reshape(SUBLANES, ha, FFT_CB)
        o_ref[cb, :, ha:, :] = hb[:, cols].reshape(SUBLANES, ha, FFT_CB)

    @pl.when(pl.program_id(0) == 0)
    def _():
        s_ref[...] = jnp.zeros_like(s_ref)

    s_ref[...] += jnp.sum(jnp.abs(hf), axis=0, keepdims=True) + jnp.sum(jnp.abs(hb), axis=0, keepdims=True)


@functools.lru_cache(maxsize=None)
def _filter_tables(L):
    ha = L // NB
    t = np.linspace(0.0, 1.0, L)
    w = (2.0 * math.pi / L) * np.arange(L)
    f = np.linspace(1e-4, FILTER_BANDS - 1, FILTER_BANDS)[None, :]

    def features(lag, mask):
        tl, wl = t[lag][:, None], w[lag][:, None]
        pad = np.zeros((lag.shape[0], LANES - FILTER_EMB - 1))
        z = np.concatenate([tl, np.cos(wl * f), -np.sin(wl * f), mask[:, None], pad], axis=-1)
        return z.astype(np.float32)

    b = np.arange(NB)[:, None]
    a = np.arange(ha)[None, :]
    lag_f = (NB * a + b).reshape(-1)
    lag_b = (L - NB * a - b).reshape(-1)
    valid = lag_b < L
    dl = np.abs(np.linspace(MIN_DECAY, MAX_DECAY, D_HY))[None, :].astype(np.float32)
    return (features(lag_f, np.zeros(lag_f.shape)),
            features(np.where(valid, lag_b, 0), valid.astype(np.float64)), dl)


def _filter(L, w1, b1, f1, w2, b2, f2, w3, b3, f3, w4):
    ha = L // NB
    rc = SUBLANES * ha
    zf, zb, dl = (jnp.asarray(t) for t in _filter_tables(L))
    w1p = jnp.pad(w1, ((0, LANES - w1.shape[0]), (0, 0)))
    row = lambda v: jnp.tile(v.reshape(1, -1), (1, 2))
    zero = jnp.zeros_like(w2)
    diag2 = lambda w: jnp.concatenate([jnp.concatenate([w, zero], axis=1),
                                       jnp.concatenate([zero, w], axis=1)], axis=0)
    const = lambda j: (0, 0)
    full = lambda arr: pl.BlockSpec(arr.shape, const)
    args = [zf, zb, dl, w1p, row(b1), row(f1), diag2(w2), row(b2), row(f2), diag2(w3), row(b3), row(f3), w4, w4]
    specs = [full(a) for a in args]
    specs[0] = specs[1] = pl.BlockSpec((rc, LANES), lambda j: (j, 0))
    specs[12] = pl.BlockSpec((w4.shape[0], D_HY), lambda j: (0, 0))
    specs[13] = pl.BlockSpec((w4.shape[0], D_HY), lambda j: (0, 1))
    return pl.pallas_call(
        functools.partial(_filt_kernel, ha=ha),
        grid=(NB // SUBLANES,),
        in_specs=specs,
        out_specs=[pl.BlockSpec((NCB, SUBLANES, 2 * ha, FFT_CB), lambda j: (0, j, 0, 0)),
                   pl.BlockSpec((1, D_HY), const)],
        out_shape=[jax.ShapeDtypeStruct((NCB, NB, 2 * ha, FFT_CB), f32),
                   jax.ShapeDtypeStruct((1, D_HY), f32)],
        compiler_params=_params("arbitrary"),
        name="filt",
    )(*args)


@functools.lru_cache(maxsize=None)
def _dft_matrices(L):
    n = 2 * L
    na = n // NB
    ha = na // 2
    b = np.arange(NB)[:, None, None]
    p = np.arange(na)[None, :, None]
    a = np.arange(na)[None, None, :]
    ang = (2.0 * math.pi / n) * ((p * (NB * a + b)) % n)
    c, s = np.cos(ang), np.sin(ang)
    ch, sh = c[:, :, :ha], s[:, :, :ha]
    m_data = np.concatenate([np.concatenate([ch, sh], axis=2),
                             np.concatenate([-sh, ch], axis=2)], axis=1)
    m_real = np.concatenate([c, -s], axis=1)
    cht, sht = np.swapaxes(ch, 1, 2), np.swapaxes(sh, 1, 2)
    m_inv = np.concatenate([np.concatenate([cht, -sht], axis=2),
                            np.concatenate([sht, cht], axis=2)], axis=1)
    q = np.arange(NB)
    gang = (2.0 * math.pi / NB) * ((q[:, None] * q[None, :]) % NB)
    gc, gs = np.cos(gang), np.sin(gang)
    g_fwd = np.concatenate([np.concatenate([gc, gs], axis=1),
                            np.concatenate([-gs, gc], axis=1)], axis=0)
    g_inv = np.concatenate([np.concatenate([gc, -gs], axis=1),
                            np.concatenate([gs, gc], axis=1)], axis=0)
    return tuple(m.astype(np.float32) for m in (m_data, m_real, m_inv, g_fwd, g_inv))


def _first_stage(x_ref, m_ref, s_ref, s, na):
    nb = x_ref.shape[0]
    for j in range(nb):
        res = jnp.dot(m_ref[j], x_ref[j].astype(bf16), preferred_element_type=f32)
        rows = pl.ds(s * nb + j, na, stride=FFT_PITCH)
        _put_cols(s_ref.at[0], rows, res[:na])
        _put_cols(s_ref.at[1], rows, res[na:])


def _slab(p):
    return pl.ds(pl.multiple_of(p * FFT_PITCH, SUBLANES), NB)


def _second_stage(s_ref, gf_ref, p):
    rows = _slab(p)
    y = jnp.concatenate([_get_cols(s_ref.at[0], rows), _get_cols(s_ref.at[1], rows)], axis=0)
    return jnp.dot(gf_ref[...], y.astype(bf16), preferred_element_type=f32)


def _fftk_kernel(k_ref, ma_ref, l1_ref, sk_ref, gf_ref, o_ref, s_ref, *, na, a_steps, scale):
    s = pl.program_id(1)

    @pl.when(s < a_steps)
    def _():
        _first_stage(k_ref, ma_ref, s_ref, s, na)

    @pl.when(s >= a_steps)
    def _():
        inv = scale / (l1_ref[...] + EPS)
        tap = scale * sk_ref[...]
        for j in range(FFTK_PB):
            z = _second_stage(s_ref, gf_ref, (s - a_steps) * FFTK_PB + j)
            o_ref[0, j] = (z[:NB] * inv + tap).astype(o_ref.dtype)
            o_ref[1, j] = (z[NB:] * inv).astype(o_ref.dtype)


def _fft_k(kt, l1, skip, m_real, g_fwd, scale):
    _, _, na, _ = kt.shape
    a_steps, b_steps = NB // FFTK_BA, na // FFTK_PB
    last_a = a_steps - 1
    const = lambda c, s: (0, 0)
    return pl.pallas_call(
        functools.partial(_fftk_kernel, na=na, a_steps=a_steps, scale=scale),
        grid=(NCB, a_steps + b_steps),
        in_specs=[pl.BlockSpec((None, FFTK_BA, na, FFT_CB), lambda c, s: (c, jnp.minimum(s, last_a), 0, 0)),
                  pl.BlockSpec((FFTK_BA, 2 * na, na), lambda c, s: (jnp.minimum(s, last_a), 0, 0)),
                  pl.BlockSpec((1, FFT_CB), lambda c, s: (0, c)),
                  pl.BlockSpec((1, FFT_CB), lambda c, s: (0, c)),
                  pl.BlockSpec(g_fwd.shape, const)],
        out_specs=pl.BlockSpec((2, None, FFTK_PB, NB, FFT_CB),
                               lambda c, s: (0, c, jnp.clip(s - a_steps, 0, b_steps - 1), 0, 0)),
        out_shape=jax.ShapeDtypeStruct((2, NCB, na, NB, FFT_CB), bf16),
        scratch_shapes=[pltpu.VMEM((2, FFT_CB // LANES, na * FFT_PITCH, LANES), f32)],
        compiler_params=_params("arbitrary", "arbitrary"),
        name="fft_k",
    )(kt, m_real, l1, skip, g_fwd)


def _fftconv_kernel(ua_ref, ma_ref, k_ref, gf_ref, gi_ref, mc_ref, x0_ref, o_ref, s_ref,
                    *, na, a_steps, b_steps):
    s = pl.program_id(2)

    @pl.when(s < a_steps)
    def _():
        _first_stage(ua_ref, ma_ref, s_ref, s, na)

    @pl.when((s >= a_steps) & (s < a_steps + b_steps))
    def _():
        for j in range(FFT_PB):
            p = (s - a_steps) * FFT_PB + j
            z = _second_stage(s_ref, gf_ref, p)
            zr, zi = z[:NB], z[NB:]
            kr, ki = k_ref[0, j], k_ref[1, j]
            f = jnp.concatenate([zr * kr - zi * ki, zr * ki + zi * kr], axis=0)
            v = jnp.dot(gi_ref[...], f.astype(bf16), preferred_element_type=f32)
            _put_cols(s_ref.at[0], _slab(p), v[:NB])
            _put_cols(s_ref.at[1], _slab(p), v[NB:])

    @pl.when(s >= a_steps + b_steps)
    def _():
        def last_stage(j):
            rows = pl.ds((s - (a_steps + b_steps)) * FFT_BC + j, na, stride=FFT_PITCH)
            v = jnp.concatenate([_get_cols(s_ref.at[0], rows), _get_cols(s_ref.at[1], rows)], axis=0)
            return jnp.dot(mc_ref[j], v.astype(bf16), preferred_element_type=f32)

        ha = na // 2
        for jj in range(FFT_BC // 2):
            y0, y1 = last_stage(2 * jj), last_stage(2 * jj + 1)
            rows = slice(2 * SUBLANES * jj, 2 * SUBLANES * (jj + 1))
            for half in range(2):
                for t in range(ha // SUBLANES):
                    r0 = half * ha + t * SUBLANES
                    y = jnp.concatenate([y0[r0:r0 + SUBLANES], y1[r0:r0 + SUBLANES]], axis=0)
                    o_ref[half, t, rows, :] = (x0_ref[half, t, rows, :] * y).astype(bf16)


def _fft_conv(u, x0, kf, m_data, m_inv, g_fwd, g_inv):
    P, _, _, na, _ = u.shape
    a_steps, b_steps, c_steps = NB // FFT_BA, na // FFT_PB, NB // FFT_BC
    n_tiles = na // (2 * SUBLANES)
    last_a = a_steps - 1
    b_idx = lambda s: jnp.clip(s - a_steps, 0, b_steps - 1)
    c_idx = lambda s: jnp.clip(s - (a_steps + b_steps), 0, c_steps - 1)
    const = lambda c, r, s: (0, 0)
    c_spec = pl.BlockSpec((2, n_tiles, None, FFT_BC * SUBLANES, FFT_CB), lambda c, r, s: (r, 0, c, c_idx(s), 0))
    return pl.pallas_call(
        functools.partial(_fftconv_kernel, na=na, a_steps=a_steps, b_steps=b_steps),
        grid=(NCB, P, a_steps + b_steps + c_steps),
        in_specs=[pl.BlockSpec((None, None, FFT_BA, na, FFT_CB), lambda c, r, s: (r, c, jnp.minimum(s, last_a), 0, 0)),
                  pl.BlockSpec((FFT_BA, 2 * na, na), lambda c, r, s: (jnp.minimum(s, last_a), 0, 0)),
                  pl.BlockSpec((2, None, FFT_PB, NB, FFT_CB), lambda c, r, s: (0, c, b_idx(s), 0, 0)),
                  pl.BlockSpec(g_fwd.shape, const),
                  pl.BlockSpec(g_inv.shape, const),
                  pl.BlockSpec((FFT_BC, na, 2 * na), lambda c, r, s: (c_idx(s), 0, 0)),
                  c_spec],
        out_specs=c_spec,
        out_shape=jax.ShapeDtypeStruct(x0.shape, bf16),
        scratch_shapes=[pltpu.VMEM((2, FFT_CB // LANES, na * FFT_PITCH, LANES), f32)],
        compiler_params=_params("arbitrary", "arbitrary", "arbitrary"),
        name="fft_conv",
    )(u, m_data, kf, g_fwd, g_inv, m_inv, x0)


def _carry_scan(at, bt, reverse):
    n = at.shape[0]
    row = lax.broadcasted_iota(jnp.int32, at.shape, 0)
    s = 1
    while s < n:
        keep = (row < n - s) if reverse else (row >= s)
        shift = n - s if reverse else s
        ash = jnp.where(keep, pltpu.roll(at, shift, 0), 1.0)
        bsh = jnp.where(keep, pltpu.roll(bt, shift, 0), 0.0)
        bt = at * bsh + bt
        at = at * ash
        s *= 2
    if reverse:
        return jnp.where(row < n - 1, pltpu.roll(bt, n - 1, 0), 0.0)
    return jnp.where(row >= 1, pltpu.roll(bt, 1, 0), 0.0)


def _lru_kernel(xb_ref, wg_ref, bg_ref, lam_ref, o_ref, hf_ref, af_ref, hb_ref, ab_ref, *, ha):
    bt = LRU_BT
    nc = NB // bt
    lam = lam_ref[...]
    half_c = (-0.5 * LRU_C) * (jnp.maximum(-lam, 0.0) + jnp.log1p(jnp.exp(-jnp.abs(lam))))
    rowi = lax.broadcasted_iota(jnp.int32, (bt * ha, 1), 0)

    def gates(d, k, first):
        b0 = pl.multiple_of(k * bt, bt)
        xc = xb_ref[pl.ds(b0, bt)].reshape(bt * ha, LANES)
        t = jnp.tanh(jnp.dot(xc.astype(bf16), wg_ref[d], preferred_element_type=f32) + bg_ref[d:d + 1, :])
        hc = half_c[d:d + 1, :]
        log_a = hc * t[:, :LANES] + hc
        a = jnp.exp(log_a)
        m2 = jnp.tanh(log_a) * (-1.0 - a * a)
        mult = jnp.where(m2 > 0.0, m2 * lax.rsqrt(m2), 0.0)
        if first:
            mult = jnp.where(rowi == (bt * ha - 1 if d else 0), 1.0, mult)
        hx = 0.5 * xc
        bv = mult * (t[:, LANES:] * hx + hx)
        return b0, a.reshape(bt, ha, LANES), bv.reshape(bt, ha, LANES)

    def step(k, carry, first=False):
        hf, af, hb, ab = carry
        b0, a, bv = gates(0, k, first)
        for j in range(bt):
            hf = a[j] * hf + bv[j]
            af = a[j] * af
            hf_ref[b0 + j] = hf
            af_ref[b0 + j] = af
        b0, a, bv = gates(1, nc - 1 - k, first)
        for j in reversed(range(bt)):
            hb = a[j] * hb + bv[j]
            ab = a[j] * ab
            hb_ref[b0 + j] = hb
            ab_ref[b0 + j] = ab
        return hf, af, hb, ab

    zero = jnp.zeros((ha, LANES), f32)
    one = jnp.ones((ha, LANES), f32)
    carry = step(0, (zero, one, zero, one), first=True)
    hf, af, hb, ab = lax.fori_loop(1, nc, step, carry)
    cf = _carry_scan(af, hf, reverse=False)
    cb = _carry_scan(ab, hb, reverse=True)

    fb = SUBLANES

    def finish(k, c):
        sl = pl.ds(pl.multiple_of(k * fb, fb), fb)
        h = (hf_ref[sl] + af_ref[sl] * cf) + (hb_ref[sl] + ab_ref[sl] * cb)
        r0 = pl.multiple_of(k * (fb * SUBLANES), fb * SUBLANES)
        for t in range(ha // SUBLANES):
            tile = h[:, t * SUBLANES:(t + 1) * SUBLANES, :].reshape(fb * SUBLANES, LANES)
            o_ref[t, pl.ds(r0, fb * SUBLANES), :] = tile.astype(bf16)
        return c

    lax.fori_loop(0, NB // fb, finish, 0)


def _lru_gate_weights(wa, wx):
    def blockdiag(w):
        w = w.reshape(2, -1, 2, HEAD, HEAD)
        z = jnp.zeros_like(w[:, :, 0])
        top = jnp.concatenate([w[:, :, 0], z], axis=-1)
        bot = jnp.concatenate([z, w[:, :, 1]], axis=-1)
        return jnp.concatenate([top, bot], axis=-2)
    return jnp.concatenate([blockdiag(wa), blockdiag(wx)], axis=-1).astype(bf16)


def _lru(xb, wa, ba, wx, bx, lam):
    P, _, _, ha, C = xb.shape
    nblk = C // LANES
    wg = _lru_gate_weights(0.5 * wa, 0.5 * wx)
    bg = 0.5 * jnp.concatenate([ba.reshape(2, nblk, 1, LANES), bx.reshape(2, nblk, 1, LANES)], axis=-1)
    ba_spec = pl.BlockSpec((None, NB, None, ha, LANES), lambda b, c: (b // 2, 0, b % 2, 0, c))
    return pl.pallas_call(
        functools.partial(_lru_kernel, ha=ha),
        grid=(2 * P, nblk),
        in_specs=[ba_spec,
                  pl.BlockSpec((2, None, LANES, 2 * LANES), lambda b, c: (0, c, 0, 0)),
                  pl.BlockSpec((2, None, None, 2 * LANES), lambda b, c: (0, c, 0, 0)),
                  pl.BlockSpec((2, LANES), lambda b, c: (0, c))],
        out_specs=pl.BlockSpec((None, ha // SUBLANES, TILE, LANES), lambda b, c: (b, 0, 0, c)),
        out_shape=jax.ShapeDtypeStruct((2 * P, ha // SUBLANES, TILE, C), bf16),
        scratch_shapes=[pltpu.VMEM((NB, ha, LANES), f32)] * 4,
        compiler_params=_params("parallel", "arbitrary"),
        name="lru",
    )(xb, wg, bg, lam)


def _out_kernel(yh_ref, hg_ref, yl_ref, lg_ref, x_ref, hog_ref, log_ref, wo_ref, fg_ref, o_ref, ys_ref):
    yh = jnp.concatenate([yh_ref[cb] for cb in range(NCB)], axis=-1)
    ycat = jnp.concatenate([_rms(yh.astype(f32), hog_ref[...]) * hg_ref[...],
                            _rms(yl_ref[...].astype(f32), log_ref[...]) * lg_ref[...]], axis=-1)
    y = jnp.dot(ycat.astype(bf16), wo_ref[...], preferred_element_type=f32)
    _put_cols(ys_ref, pl.ds(0, TILE), y)
    for a in range(SUBLANES):
        rows = slice(a * NB, (a + 1) * NB)
        ya = _get_cols(ys_ref, pl.ds(a, NB, stride=SUBLANES))
        o_ref[rows, :] = _rms(x_ref[rows, :] + ya, fg_ref[...])


def _out(yh, hg, yl, lg, x, hog, log_g, w_out, fg):
    B, L, D = x.shape
    const = lambda b, i: (0, 0)
    nat = pl.BlockSpec((None, TILE, D), lambda b, i: (b, i, 0))
    return pl.pallas_call(
        _out_kernel,
        grid=(B, L // TILE),
        in_specs=[pl.BlockSpec((None, None, NCB, TILE, FFT_CB), lambda b, i: (b, i, 0, 0, 0)),
                  _gate_spec(D_HY), _gate_spec(D_LRU), _gate_spec(D_LRU), nat,
                  pl.BlockSpec((1, D_HY), const), pl.BlockSpec((1, D_LRU), const),
                  pl.BlockSpec(w_out.shape, const), pl.BlockSpec((1, D), const)],
        out_specs=nat,
        out_shape=jax.ShapeDtypeStruct((B, L, D), f32),
        scratch_shapes=[pltpu.VMEM((D // LANES, TILE, LANES), f32)],
        compiler_params=_params("parallel", "arbitrary"),
        name="out",
    )(yh, hg, yl, lg, x, hog, log_g, w_out, fg)


def kernel(x, norm_g, w_in, hy_conv_w, hy_conv_b, flt_w1, flt_b1, flt_f1, flt_w2, flt_b2, flt_f2,
           flt_w3, flt_b3, flt_f3, flt_w4, hy_skip, lru_conv_w, lru_conv_b, lru_wa, lru_ba, lru_wx,
           lru_bx, lru_lam, hy_out_g, lru_out_g, w_out, final_g):
    B, L, D = x.shape
    assert norm_g.shape[0] == 1, "one layer"
    assert B % 2 == 0 and L % TILE == 0
    ha = L // NB
    na = 2 * ha
    row = lambda v: v.reshape(1, -1)

    u, x0, hg, xb, lg = _inproj(x, row(norm_g[0]), w_in[0].astype(bf16), hy_conv_w[0], row(hy_conv_b[0]),
                                lru_conv_w[0], row(lru_conv_b[0]))
    pair = lambda t: t.reshape(B // 2, NCB, NB, na, FFT_CB)

    m_data, m_real, m_inv, g_fwd, g_inv = (jnp.asarray(m).astype(bf16) for m in _dft_matrices(L))
    kt, l1 = _filter(L, flt_w1[0], flt_b1[0], flt_f1[0], flt_w2[0], flt_b2[0], flt_f2[0],
                     flt_w3[0], flt_b3[0], flt_f3[0], flt_w4[0])
    kf = _fft_k(kt, l1, row(hy_skip[0]), m_real, g_fwd, scale=1.0 / (2 * L))
    yh = _fft_conv(pair(u), x0, kf, m_data, m_inv, g_fwd, g_inv)

    yl = _lru(xb, lru_wa[0], lru_ba[0], lru_wx[0], lru_bx[0], lru_lam[0])
    return _out(yh, hg, yl, lg, x, row(hy_out_g[0]), row(lru_out_g[0]), w_out[0].astype(bf16), row(final_g))
```

```python
import functools
import math

import jax
import jax.numpy as jnp
import numpy as np
from jax import lax
from jax.experimental import pallas as pl
from jax.experimental.pallas import tpu as pltpu

f32 = jnp.float32
bf16 = jnp.bfloat16

D_HY = 768
D_LRU = 768
HEAD = 64
LANES = 128
SUBLANES = 8
NB = 128
TILE = SUBLANES * NB
HALO = SUBLANES
FFT_CB = 256
NCB = D_HY // FFT_CB
FFT_BA = 16
FFT_PB = 32
FFT_BC = 16
FFT_PITCH = NB + SUBLANES
FFTK_BA = 32
FFTK_PB = 32
LRU_BT = 32
FILTER_BANDS = 16
FILTER_EMB = 2 * FILTER_BANDS + 1
MASK_COL = FILTER_EMB
FILTER_TARGET = 1e-2
MIN_DECAY = math.log(FILTER_TARGET) / 0.3
MAX_DECAY = math.log(FILTER_TARGET) / 1.5
LRU_C = 8.0
EPS = 1e-6
VMEM_LIMIT = 60 * 1024 * 1024


def _params(*sem):
    return pltpu.CompilerParams(dimension_semantics=sem, vmem_limit_bytes=VMEM_LIMIT)


def _rms(y, g):
    return y * lax.rsqrt(jnp.mean(y * y, axis=-1, keepdims=True) + EPS) * g


def _put_cols(ref, rows, val):
    for h in range(ref.shape[0]):
        ref[h, rows, :] = val[:, h * LANES:(h + 1) * LANES]


def _get_cols(ref, rows):
    return jnp.concatenate([ref[h, rows, :] for h in range(ref.shape[0])], axis=-1)


def _dot3(a, b):
    ah = a.astype(bf16)
    al = (a - ah.astype(f32)).astype(bf16)
    bh = b.astype(bf16)
    bl = (b - bh.astype(f32)).astype(bf16)
    dot = functools.partial(jnp.dot, preferred_element_type=f32)
    return dot(ah, bh) + (dot(ah, bl) + dot(al, bh))


def _sigmoid(x):
    return 0.5 * jnp.tanh(0.5 * x) + 0.5


def _inproj_kernel(x_ref, xp_ref, xn_ref, g_ref, w_ref, hcw_ref, hcb_ref, lcw_ref, lcb_ref,
                   u_ref, x0_ref, hg_ref, xb_ref, lg_ref, xs_ref, *, n_tiles):
    i = pl.program_id(1)
    g = g_ref[...]
    for a in range(SUBLANES):
        _put_cols(xs_ref, pl.ds(a, NB, stride=SUBLANES), _rms(x_ref[a * NB:(a + 1) * NB, :], g))
    _put_cols(xs_ref, pl.ds(TILE, HALO), jnp.where(i > 0, _rms(xp_ref[...], g), 0.0))
    _put_cols(xs_ref, pl.ds(TILE + HALO, HALO), jnp.where(i < n_tiles - 1, _rms(xn_ref[...], g), 0.0))
    xn = _get_cols(xs_ref, pl.ds(0, TILE + 2 * HALO)).astype(bf16)
    sub = lax.broadcasted_iota(jnp.int32, (SUBLANES, D_HY), 0)

    def proj(c0):
        p = jnp.dot(xn, w_ref[:, c0:c0 + D_HY], preferred_element_type=f32)
        return p.reshape(NB + 2, SUBLANES, D_HY)

    def edge(p3, s):
        if s < 0:
            return jnp.where(sub == 0, p3[NB][SUBLANES + s:SUBLANES + s + 1],
                             pltpu.roll(p3[NB + s], 1, 0))
        return jnp.where(sub == SUBLANES - 1, p3[NB + 1][s - NB:s - NB + 1],
                         pltpu.roll(p3[s - NB], SUBLANES - 1, 0))

    def conv(p3, cw_ref, cb_ref, c0, offsets):
        lo, hi = max(0, -min(offsets)), NB - max(offsets)

        def acc(get):
            y = cb_ref[:, c0:c0 + D_HY]
            for k, o in enumerate(offsets):
                y = y + get(o) * cw_ref[k:k + 1, c0:c0 + D_HY]
            return y

        inner = acc(lambda o: p3[lo + o:hi + o])
        edges = {b: acc(lambda o, b=b: p3[b + o] if 0 <= b + o < NB else edge(p3, b + o))
                 for b in list(range(lo)) + list(range(hi, NB))}
        return lo, hi, inner, edges

    def store(ref, conv_out, other=None):
        lo, hi, inner, edges = conv_out
        if other is not None:
            inner = inner * other[2]
            edges = {b: edges[b] * other[3][b] for b in edges}
        if len(ref.shape) == 3:
            ref[lo:hi] = inner
            for b, y in edges.items():
                ref[b] = y
        else:
            for cb in range(NCB):
                cols = slice(cb * FFT_CB, (cb + 1) * FFT_CB)
                ref[cb, lo:hi] = inner[:, :, cols]
                for b, y in edges.items():
                    ref[cb, b] = y[:, cols]

    hy = (-1, 0, 1)
    store(u_ref, conv(proj(0), hcw_ref, hcb_ref, 0, hy), conv(proj(2 * D_HY), hcw_ref, hcb_ref, 2 * D_HY, hy))
    lo, hi, inner, edges = conv(proj(D_HY), hcw_ref, hcb_ref, D_HY, hy)
    x0 = jnp.concatenate([edges[b] for b in range(lo)] + [inner.reshape((hi - lo) * SUBLANES, D_HY)]
                         + [edges[b] for b in range(hi, NB)], axis=0)
    for cb in range(NCB):
        x0_ref[cb] = x0[:, cb * FFT_CB:(cb + 1) * FFT_CB].astype(bf16)
    store(xb_ref, conv(proj(4 * D_HY), lcw_ref, lcb_ref, 0, (-1, 0, 1, 2)))
    xm = xn[:TILE]
    hg = jnp.dot(xm, w_ref[:, 3 * D_HY:4 * D_HY], preferred_element_type=f32)
    hg_ref[...] = (hg * _sigmoid(hg)).astype(bf16)
    lg = jnp.dot(xm, w_ref[:, 4 * D_HY + D_LRU:], preferred_element_type=f32)
    lg_ref[...] = (lg * _sigmoid(lg)).astype(bf16)


def _ba_spec(c):
    return pl.BlockSpec((None, NB, None, SUBLANES, c), lambda b, i: (b // 2, 0, b % 2, i, 0))


def _gate_spec(c):
    return pl.BlockSpec((None, None, TILE, c), lambda b, i: (b, i, 0, 0))


def _inproj(x, norm_g, w_in, hcw, hcb, lcw, lcb):
    B, L, D = x.shape
    n_tiles = L // TILE
    ha = L // NB
    hb = TILE // HALO
    n_hb = L // HALO
    const = lambda b, i: (0, 0)
    ba_shape = jax.ShapeDtypeStruct((B // 2, NB, 2, ha, D_HY), f32)
    cb_shape = jax.ShapeDtypeStruct((B // 2, NCB, NB, 2, ha, FFT_CB), f32)
    cb_spec = pl.BlockSpec((None, NCB, NB, None, SUBLANES, FFT_CB), lambda b, i: (b // 2, 0, 0, b % 2, i, 0))
    gate_shape = jax.ShapeDtypeStruct((B, n_tiles, TILE, D_HY), bf16)
    return pl.pallas_call(
        functools.partial(_inproj_kernel, n_tiles=n_tiles),
        grid=(B, n_tiles),
        in_specs=[
            pl.BlockSpec((None, TILE, D), lambda b, i: (b, i, 0)),
            pl.BlockSpec((None, HALO, D), lambda b, i: (b, jnp.maximum(i * hb - 1, 0), 0)),
            pl.BlockSpec((None, HALO, D), lambda b, i: (b, jnp.minimum((i + 1) * hb, n_hb - 1), 0)),
            pl.BlockSpec((1, D), const),
            pl.BlockSpec(w_in.shape, const, pipeline_mode=pl.Buffered(1)),
            pl.BlockSpec(hcw.shape, const),
            pl.BlockSpec(hcb.shape, const),
            pl.BlockSpec(lcw.shape, const),
            pl.BlockSpec(lcb.shape, const),
        ],
        out_specs=[cb_spec, pl.BlockSpec((None, None, NCB, TILE, FFT_CB), lambda b, i: (b, i, 0, 0, 0)),
                   _gate_spec(D_HY), _ba_spec(D_LRU), _gate_spec(D_LRU)],
        out_shape=[cb_shape, jax.ShapeDtypeStruct((B, n_tiles, NCB, TILE, FFT_CB), bf16),
                   gate_shape, ba_shape, gate_shape],
        scratch_shapes=[pltpu.VMEM((D // LANES, TILE + 2 * HALO, LANES), f32)],
        compiler_params=_params("parallel", "arbitrary"),
        name="inproj",
    )(x, x, x, norm_g, w_in, hcw, hcb, lcw, lcb)


def _filt_kernel(z_ref, dl_ref, w1_ref, b1_ref, f1_ref, w2_ref, b2_ref, f2_ref,
                 w3_ref, b3_ref, f3_ref, w4f_ref, w4b_ref, of_ref, ob_ref, s_ref, *, ha):
    dot = _dot3
    dl = dl_ref[...]
    z = z_ref[...]
    half = z.shape[0] // 2
    h = jnp.concatenate([dot(z[:half], w1_ref[...]), dot(z[half:], w1_ref[...])], axis=-1)
    h = jnp.sin(f1_ref[...] * (h + b1_ref[...]))
    h = jnp.sin(f2_ref[...] * (dot(h, w2_ref[...]) + b2_ref[...]))
    h = jnp.sin(f3_ref[...] * (dot(h, w3_ref[...]) + b3_ref[...]))
    nh = h.shape[-1] // 2
    h = jnp.concatenate([h[:, :nh], h[:, nh:]], axis=0)
    win = jnp.exp(-z[:, 0:1] * dl)
    nf = SUBLANES * ha
    hf = dot(h[:nf], w4f_ref[...]) * win[:nf]
    hb = dot(h[ha:], w4b_ref[...]) * (win[ha:] * z[ha:, MASK_COL:MASK_COL + 1])
    for cb in range(NCB):
        cols = slice(cb * FFT_CB, (cb + 1) * FFT_CB)
        of_ref[cb] = hf[:, cols].reshape(SUBLANES, ha, FFT_CB)
        for i in range(SUBLANES):
            ob_ref[cb, SUBLANES - 1 - i] = hb[i * ha:(i + 1) * ha, cols]

    @pl.when(pl.program_id(0) == 0)
    def _():
        s_ref[...] = jnp.zeros_like(s_ref)

    s_ref[...] += jnp.sum(jnp.abs(hf), axis=0, keepdims=True) + jnp.sum(jnp.abs(hb), axis=0, keepdims=True)


@functools.lru_cache(maxsize=None)
def _filter_tables(L):
    ha = L // NB
    t = np.linspace(0.0, 1.0, L)
    w = (2.0 * math.pi / L) * np.arange(L)
    f = np.linspace(1e-4, FILTER_BANDS - 1, FILTER_BANDS)[None, :]
    j = np.arange(NB // SUBLANES)[:, None, None]
    r = np.arange(SUBLANES + 1)[None, :, None]
    q = np.arange(ha)[None, None, :]
    lag = (NB * q + SUBLANES * j + r).reshape(-1)
    valid = lag < L
    lag = np.where(valid, lag, 0)
    tl, wl = t[lag][:, None], w[lag][:, None]
    pad = np.zeros((lag.shape[0], LANES - FILTER_EMB - 1))
    z = np.concatenate([tl, np.cos(wl * f), -np.sin(wl * f), valid[:, None].astype(np.float64), pad], axis=-1)
    dl = np.abs(np.linspace(MIN_DECAY, MAX_DECAY, D_HY))[None, :].astype(np.float32)
    return z.astype(np.float32), dl


def _filter(L, w1, b1, f1, w2, b2, f2, w3, b3, f3, w4):
    ha = L // NB
    assert ha % (2 * SUBLANES) == 0
    rz = (SUBLANES + 1) * ha
    z, dl = (jnp.asarray(t) for t in _filter_tables(L))
    w1p = jnp.pad(w1, ((0, LANES - w1.shape[0]), (0, 0)))
    row = lambda v: jnp.tile(v.reshape(1, -1), (1, 2))
    zero = jnp.zeros_like(w2)
    diag2 = lambda w: jnp.concatenate([jnp.concatenate([w, zero], axis=1),
                                       jnp.concatenate([zero, w], axis=1)], axis=0)
    const = lambda j: (0, 0)
    full = lambda arr: pl.BlockSpec(arr.shape, const)
    args = [z, dl, w1p, row(b1), row(f1), diag2(w2), row(b2), row(f2), diag2(w3), row(b3), row(f3), w4, w4]
    specs = [full(a) for a in args]
    specs[0] = pl.BlockSpec((rz, LANES), lambda j: (j, 0))
    specs[11] = pl.BlockSpec((w4.shape[0], D_HY), lambda j: (0, 0))
    specs[12] = pl.BlockSpec((w4.shape[0], D_HY), lambda j: (0, 1))
    n_steps = NB // SUBLANES
    half_shape = jax.ShapeDtypeStruct((NCB, NB, ha, FFT_CB), f32)
    return pl.pallas_call(
        functools.partial(_filt_kernel, ha=ha),
        grid=(n_steps,),
        in_specs=specs,
        out_specs=[pl.BlockSpec((NCB, SUBLANES, ha, FFT_CB), lambda j: (0, j, 0, 0)),
                   pl.BlockSpec((NCB, SUBLANES, ha, FFT_CB), lambda j: (0, n_steps - 1 - j, 0, 0)),
                   pl.BlockSpec((1, D_HY), const)],
        out_shape=[half_shape, half_shape, jax.ShapeDtypeStruct((1, D_HY), f32)],
        compiler_params=_params("arbitrary"),
        name="filt",
    )(*args)


@functools.lru_cache(maxsize=None)
def _dft_matrices(L):
    n = 2 * L
    na = n // NB
    ha = na // 2
    b = np.arange(NB)[:, None, None]
    p = np.arange(na)[None, :, None]
    a = np.arange(na)[None, None, :]
    ang = (2.0 * math.pi / n) * ((p * (NB * a + b)) % n)
    c, s = np.cos(ang), np.sin(ang)
    ch, sh = c[:, :, :ha], s[:, :, :ha]
    m_data = np.concatenate([np.concatenate([ch, sh], axis=2),
                             np.concatenate([-sh, ch], axis=2)], axis=1)
    cr = np.concatenate([ch, c[:, :, :ha - 1:-1]], axis=2)
    sr = np.concatenate([sh, s[:, :, :ha - 1:-1]], axis=2)
    m_real = np.concatenate([cr, -sr], axis=1)
    cht, sht = np.swapaxes(ch, 1, 2), np.swapaxes(sh, 1, 2)
    m_inv = np.concatenate([np.concatenate([cht, -sht], axis=2),
                            np.concatenate([sht, cht], axis=2)], axis=1)
    q = np.arange(NB)
    gang = (2.0 * math.pi / NB) * ((q[:, None] * q[None, :]) % NB)
    gc, gs = np.cos(gang), np.sin(gang)
    g_fwd = np.concatenate([np.concatenate([gc, gs], axis=1),
                            np.concatenate([-gs, gc], axis=1)], axis=0)
    g_inv = np.concatenate([np.concatenate([gc, -gs], axis=1),
                            np.concatenate([gs, gc], axis=1)], axis=0)
    return tuple(m.astype(np.float32) for m in (m_data, m_real, m_inv, g_fwd, g_inv))


def _first_stage(x_ref, m_ref, s_ref, s, na, x2_ref=None):
    nb = x_ref.shape[0]
    for j in range(nb):
        x = x_ref[j] if x2_ref is None else jnp.concatenate([x_ref[j], x2_ref[j]], axis=0)
        res = jnp.dot(m_ref[j], x.astype(bf16), preferred_element_type=f32)
        rows = pl.ds(s * nb + j, na, stride=FFT_PITCH)
        _put_cols(s_ref.at[0], rows, res[:na])
        _put_cols(s_ref.at[1], rows, res[na:])


def _slab(p):
    return pl.ds(pl.multiple_of(p * FFT_PITCH, SUBLANES), NB)


def _second_stage(s_ref, gf_ref, p):
    rows = _slab(p)
    y = jnp.concatenate([_get_cols(s_ref.at[0], rows), _get_cols(s_ref.at[1], rows)], axis=0)
    return jnp.dot(gf_ref[...], y.astype(bf16), preferred_element_type=f32)


def _fftk_kernel(kf_ref, kb_ref, ma_ref, l1_ref, sk_ref, gf_ref, o_ref, s_ref, *, na, a_steps, scale):
    s = pl.program_id(1)

    @pl.when(s < a_steps)
    def _():
        _first_stage(kf_ref, ma_ref, s_ref, s, na, kb_ref)

    @pl.when(s >= a_steps)
    def _():
        inv = scale / (l1_ref[...] + EPS)
        tap = scale * sk_ref[...]
        for j in range(FFTK_PB):
            z = _second_stage(s_ref, gf_ref, (s - a_steps) * FFTK_PB + j)
            o_ref[0, j] = (z[:NB] * inv + tap).astype(o_ref.dtype)
            o_ref[1, j] = (z[NB:] * inv).astype(o_ref.dtype)


def _fft_k(kt_f, kt_b, l1, skip, m_real, g_fwd, scale):
    na = 2 * kt_f.shape[2]
    a_steps, b_steps = NB // FFTK_BA, na // FFTK_PB
    last_a = a_steps - 1
    const = lambda c, s: (0, 0)
    return pl.pallas_call(
        functools.partial(_fftk_kernel, na=na, a_steps=a_steps, scale=scale),
        grid=(NCB, a_steps + b_steps),
        in_specs=[pl.BlockSpec((None, FFTK_BA, na // 2, FFT_CB), lambda c, s: (c, jnp.minimum(s, last_a), 0, 0)),
                  pl.BlockSpec((None, FFTK_BA, na // 2, FFT_CB), lambda c, s: (c, jnp.minimum(s, last_a), 0, 0)),
                  pl.BlockSpec((FFTK_BA, 2 * na, na), lambda c, s: (jnp.minimum(s, last_a), 0, 0)),
                  pl.BlockSpec((1, FFT_CB), lambda c, s: (0, c)),
                  pl.BlockSpec((1, FFT_CB), lambda c, s: (0, c)),
                  pl.BlockSpec(g_fwd.shape, const)],
        out_specs=pl.BlockSpec((2, None, FFTK_PB, NB, FFT_CB),
                               lambda c, s: (0, c, jnp.clip(s - a_steps, 0, b_steps - 1), 0, 0)),
        out_shape=jax.ShapeDtypeStruct((2, NCB, na, NB, FFT_CB), bf16),
        scratch_shapes=[pltpu.VMEM((2, FFT_CB // LANES, na * FFT_PITCH, LANES), f32)],
        compiler_params=_params("arbitrary", "arbitrary"),
        name="fft_k",
    )(kt_f, kt_b, m_real, l1, skip, g_fwd)


def _fftconv_kernel(ua_ref, ma_ref, k_ref, gf_ref, gi_ref, mc_ref, x0_ref, o_ref, s_ref,
                    *, na, a_steps, b_steps):
    s = pl.program_id(2)

    @pl.when(s < a_steps)
    def _():
        _first_stage(ua_ref, ma_ref, s_ref, s, na)

    @pl.when((s >= a_steps) & (s < a_steps + b_steps))
    def _():
        for j in range(FFT_PB):
            p = (s - a_steps) * FFT_PB + j
            z = _second_stage(s_ref, gf_ref, p)
            zr, zi = z[:NB], z[NB:]
            kr, ki = k_ref[0, j], k_ref[1, j]
            f = jnp.concatenate([zr * kr - zi * ki, zr * ki + zi * kr], axis=0)
            v = jnp.dot(gi_ref[...], f.astype(bf16), preferred_element_type=f32)
            _put_cols(s_ref.at[0], _slab(p), v[:NB])
            _put_cols(s_ref.at[1], _slab(p), v[NB:])

    @pl.when(s >= a_steps + b_steps)
    def _():
        def last_stage(j):
            rows = pl.ds((s - (a_steps + b_steps)) * FFT_BC + j, na, stride=FFT_PITCH)
            v = jnp.concatenate([_get_cols(s_ref.at[0], rows), _get_cols(s_ref.at[1], rows)], axis=0)
            return jnp.dot(mc_ref[j], v.astype(bf16), preferred_element_type=f32)

        ha = na // 2
        for jj in range(FFT_BC // 2):
            y0, y1 = last_stage(2 * jj), last_stage(2 * jj + 1)
            rows = slice(2 * SUBLANES * jj, 2 * SUBLANES * (jj + 1))
            for half in range(2):
                for t in range(ha // SUBLANES):
                    r0 = half * ha + t * SUBLANES
                    y = jnp.concatenate([y0[r0:r0 + SUBLANES], y1[r0:r0 + SUBLANES]], axis=0)
                    o_ref[half, t, rows, :] = (x0_ref[half, t, rows, :] * y).astype(bf16)


def _fft_conv(u, x0, kf, m_data, m_inv, g_fwd, g_inv):
    P, _, _, na, _ = u.shape
    a_steps, b_steps, c_steps = NB // FFT_BA, na // FFT_PB, NB // FFT_BC
    n_tiles = na // (2 * SUBLANES)
    last_a = a_steps - 1
    b_idx = lambda s: jnp.clip(s - a_steps, 0, b_steps - 1)
    c_idx = lambda s: jnp.clip(s - (a_steps + b_steps), 0, c_steps - 1)
    const = lambda c, r, s: (0, 0)
    c_spec = pl.BlockSpec((2, n_tiles, None, FFT_BC * SUBLANES, FFT_CB), lambda c, r, s: (r, 0, c, c_idx(s), 0))
    return pl.pallas_call(
        functools.partial(_fftconv_kernel, na=na, a_steps=a_steps, b_steps=b_steps),
        grid=(NCB, P, a_steps + b_steps + c_steps),
        in_specs=[pl.BlockSpec((None, None, FFT_BA, na, FFT_CB), lambda c, r, s: (r, c, jnp.minimum(s, last_a), 0, 0)),
                  pl.BlockSpec((FFT_BA, 2 * na, na), lambda c, r, s: (jnp.minimum(s, last_a), 0, 0)),
                  pl.BlockSpec((2, None, FFT_PB, NB, FFT_CB), lambda c, r, s: (0, c, b_idx(s), 0, 0)),
                  pl.BlockSpec(g_fwd.shape, const),
                  pl.BlockSpec(g_inv.shape, const),
                  pl.BlockSpec((FFT_BC, na, 2 * na), lambda c, r, s: (c_idx(s), 0, 0)),
                  c_spec],
        out_specs=c_spec,
        out_shape=jax.ShapeDtypeStruct(x0.shape, bf16),
        scratch_shapes=[pltpu.VMEM((2, FFT_CB // LANES, na * FFT_PITCH, LANES), f32)],
        compiler_params=_params("arbitrary", "arbitrary", "arbitrary"),
        name="fft_conv",
    )(u, m_data, kf, g_fwd, g_inv, m_inv, x0)


def _carry_scan(at, bt, reverse):
    n = at.shape[0]
    row = lax.broadcasted_iota(jnp.int32, at.shape, 0)
    s = 1
    while s < n:
        keep = (row < n - s) if reverse else (row >= s)
        shift = n - s if reverse else s
        ash = jnp.where(keep, pltpu.roll(at, shift, 0), 1.0)
        bsh = jnp.where(keep, pltpu.roll(bt, shift, 0), 0.0)
        bt = at * bsh + bt
        at = at * ash
        s *= 2
    if reverse:
        return jnp.where(row < n - 1, pltpu.roll(bt, n - 1, 0), 0.0)
    return jnp.where(row >= 1, pltpu.roll(bt, 1, 0), 0.0)


def _lru_kernel(xb_ref, wg_ref, bg_ref, lam_ref, o_ref, hf_ref, af_ref, hb_ref, ab_ref, *, ha):
    bt = LRU_BT
    nc = NB // bt
    lam = lam_ref[...]
    half_c = (-0.5 * LRU_C) * (jnp.maximum(-lam, 0.0) + jnp.log1p(jnp.exp(-jnp.abs(lam))))
    rowi = lax.broadcasted_iota(jnp.int32, (bt * ha, 1), 0)

    def gates(d, k, first):
        b0 = pl.multiple_of(k * bt, bt)
        xc = xb_ref[pl.ds(b0, bt)].reshape(bt * ha, LANES)
        t = jnp.tanh(jnp.dot(xc.astype(bf16), wg_ref[d], preferred_element_type=f32) + bg_ref[d:d + 1, :])
        hc = half_c[d:d + 1, :]
        log_a = hc * t[:, :LANES] + hc
        a = jnp.exp(log_a)
        m2 = jnp.tanh(log_a) * (-1.0 - a * a)
        mult = jnp.where(m2 > 0.0, m2 * lax.rsqrt(m2), 0.0)
        if first:
            mult = jnp.where(rowi == (bt * ha - 1 if d else 0), 1.0, mult)
        hx = 0.5 * xc
        bv = mult * (t[:, LANES:] * hx + hx)
        return b0, a.reshape(bt, ha, LANES), bv.reshape(bt, ha, LANES)

    def step(k, carry, first=False):
        hf, af, hb, ab = carry
        b0, a, bv = gates(0, k, first)
        for j in range(bt):
            hf = a[j] * hf + bv[j]
            af = a[j] * af
            hf_ref[b0 + j] = hf
            af_ref[b0 + j] = af
        b0, a, bv = gates(1, nc - 1 - k, first)
        for j in reversed(range(bt)):
            hb = a[j] * hb + bv[j]
            ab = a[j] * ab
            hb_ref[b0 + j] = hb
            ab_ref[b0 + j] = ab
        return hf, af, hb, ab

    zero = jnp.zeros((ha, LANES), f32)
    one = jnp.ones((ha, LANES), f32)
    carry = step(0, (zero, one, zero, one), first=True)
    hf, af, hb, ab = lax.fori_loop(1, nc, step, carry)
    cf = _carry_scan(af, hf, reverse=False)
    cb = _carry_scan(ab, hb, reverse=True)

    fb = SUBLANES

    def finish(k, c):
        sl = pl.ds(pl.multiple_of(k * fb, fb), fb)
        h = (hf_ref[sl] + af_ref[sl] * cf) + (hb_ref[sl] + ab_ref[sl] * cb)
        r0 = pl.multiple_of(k * (fb * SUBLANES), fb * SUBLANES)
        for t in range(ha // SUBLANES):
            tile = h[:, t * SUBLANES:(t + 1) * SUBLANES, :].reshape(fb * SUBLANES, LANES)
            o_ref[t, pl.ds(r0, fb * SUBLANES), :] = tile.astype(bf16)
        return c

    lax.fori_loop(0, NB // fb, finish, 0)


def _lru_gate_weights(wa, wx):
    def blockdiag(w):
        w = w.reshape(2, -1, 2, HEAD, HEAD)
        z = jnp.zeros_like(w[:, :, 0])
        top = jnp.concatenate([w[:, :, 0], z], axis=-1)
        bot = jnp.concatenate([z, w[:, :, 1]], axis=-1)
        return jnp.concatenate([top, bot], axis=-2)
    return jnp.concatenate([blockdiag(wa), blockdiag(wx)], axis=-1).astype(bf16)


def _lru(xb, wa, ba, wx, bx, lam):
    P, _, _, ha, C = xb.shape
    nblk = C // LANES
    wg = _lru_gate_weights(0.5 * wa, 0.5 * wx)
    bg = 0.5 * jnp.concatenate([ba.reshape(2, nblk, 1, LANES), bx.reshape(2, nblk, 1, LANES)], axis=-1)
    ba_spec = pl.BlockSpec((None, NB, None, ha, LANES), lambda b, c: (b // 2, 0, b % 2, 0, c))
    return pl.pallas_call(
        functools.partial(_lru_kernel, ha=ha),
        grid=(2 * P, nblk),
        in_specs=[ba_spec,
                  pl.BlockSpec((2, None, LANES, 2 * LANES), lambda b, c: (0, c, 0, 0)),
                  pl.BlockSpec((2, None, None, 2 * LANES), lambda b, c: (0, c, 0, 0)),
                  pl.BlockSpec((2, LANES), lambda b, c: (0, c))],
        out_specs=pl.BlockSpec((None, ha // SUBLANES, TILE, LANES), lambda b, c: (b, 0, 0, c)),
        out_shape=jax.ShapeDtypeStruct((2 * P, ha // SUBLANES, TILE, C), bf16),
        scratch_shapes=[pltpu.VMEM((NB, ha, LANES), f32)] * 4,
        compiler_params=_params("parallel", "arbitrary"),
        name="lru",
    )(xb, wg, bg, lam)


def _out_kernel(yh_ref, hg_ref, yl_ref, lg_ref, x_ref, hog_ref, log_ref, wo_ref, fg_ref, o_ref, ys_ref):
    yh = jnp.concatenate([yh_ref[cb] for cb in range(NCB)], axis=-1)
    ycat = jnp.concatenate([_rms(yh.astype(f32), hog_ref[...]) * hg_ref[...],
                            _rms(yl_ref[...].astype(f32), log_ref[...]) * lg_ref[...]], axis=-1)
    y = jnp.dot(ycat.astype(bf16), wo_ref[...], preferred_element_type=f32)
    _put_cols(ys_ref, pl.ds(0, TILE), y)
    for a in range(SUBLANES):
        rows = slice(a * NB, (a + 1) * NB)
        ya = _get_cols(ys_ref, pl.ds(a, NB, stride=SUBLANES))
        o_ref[rows, :] = _rms(x_ref[rows, :] + ya, fg_ref[...])


def _out(yh, hg, yl, lg, x, hog, log_g, w_out, fg):
    B, L, D = x.shape
    const = lambda b, i: (0, 0)
    nat = pl.BlockSpec((None, TILE, D), lambda b, i: (b, i, 0))
    return pl.pallas_call(
        _out_kernel,
        grid=(B, L // TILE),
        in_specs=[pl.BlockSpec((None, None, NCB, TILE, FFT_CB), lambda b, i: (b, i, 0, 0, 0)),
                  _gate_spec(D_HY), _gate_spec(D_LRU), _gate_spec(D_LRU), nat,
                  pl.BlockSpec((1, D_HY), const), pl.BlockSpec((1, D_LRU), const),
                  pl.BlockSpec(w_out.shape, const), pl.BlockSpec((1, D), const)],
        out_specs=nat,
        out_shape=jax.ShapeDtypeStruct((B, L, D), f32),
        scratch_shapes=[pltpu.VMEM((D // LANES, TILE, LANES), f32)],
        compiler_params=_params("parallel", "arbitrary"),
        name="out",
    )(yh, hg, yl, lg, x, hog, log_g, w_out, fg)


def kernel(x, norm_g, w_in, hy_conv_w, hy_conv_b, flt_w1, flt_b1, flt_f1, flt_w2, flt_b2, flt_f2,
           flt_w3, flt_b3, flt_f3, flt_w4, hy_skip, lru_conv_w, lru_conv_b, lru_wa, lru_ba, lru_wx,
           lru_bx, lru_lam, hy_out_g, lru_out_g, w_out, final_g):
    B, L, D = x.shape
    assert norm_g.shape[0] == 1, "one layer"
    assert B % 2 == 0 and L % TILE == 0
    ha = L // NB
    na = 2 * ha
    row = lambda v: v.reshape(1, -1)

    u, x0, hg, xb, lg = _inproj(x, row(norm_g[0]), w_in[0].astype(bf16), hy_conv_w[0], row(hy_conv_b[0]),
                                lru_conv_w[0], row(lru_conv_b[0]))
    pair = lambda t: t.reshape(B // 2, NCB, NB, na, FFT_CB)

    m_data, m_real, m_inv, g_fwd, g_inv = (jnp.asarray(m).astype(bf16) for m in _dft_matrices(L))
    kt_f, kt_b, l1 = _filter(L, flt_w1[0], flt_b1[0], flt_f1[0], flt_w2[0], flt_b2[0], flt_f2[0],
                             flt_w3[0], flt_b3[0], flt_f3[0], flt_w4[0])
    kf = _fft_k(kt_f, kt_b, l1, row(hy_skip[0]), m_real, g_fwd, scale=1.0 / (2 * L))
    yh = _fft_conv(pair(u), x0, kf, m_data, m_inv, g_fwd, g_inv)

    yl = _lru(xb, lru_wa[0], lru_ba[0], lru_wx[0], lru_bx[0], lru_lam[0])
    return _out(yh, hg, yl, lg, x, row(hy_out_g[0]), row(lru_out_g[0]), w_out[0].astype(bf16), row(final_g))
```

```python
import functools
import math

import jax
import jax.numpy as jnp
import numpy as np
from jax import lax
from jax.experimental import pallas as pl
from jax.experimental.pallas import tpu as pltpu

f32 = jnp.float32
bf16 = jnp.bfloat16

D_HY = 768
D_LRU = 768
HEAD = 64
LANES = 128
SUBLANES = 8
NB = 128
TILE = SUBLANES * NB
HALO = SUBLANES
FFT_CB = 256
NCB = D_HY // FFT_CB
FFT_BA = 16
FFT_PB = 32
FFT_BC = 16
FFT_PITCH = NB + SUBLANES
FFTK_BA = 32
FFTK_PB = 32
LRU_BT = 32
FILTER_BANDS = 16
FILTER_EMB = 2 * FILTER_BANDS + 1
MASK_COL = FILTER_EMB
FILTER_TARGET = 1e-2
MIN_DECAY = math.log(FILTER_TARGET) / 0.3
MAX_DECAY = math.log(FILTER_TARGET) / 1.5
LRU_C = 8.0
EPS = 1e-6
VMEM_LIMIT = 60 * 1024 * 1024


def _params(*sem):
    return pltpu.CompilerParams(dimension_semantics=sem, vmem_limit_bytes=VMEM_LIMIT)


def _rms(y, g):
    return y * lax.rsqrt(jnp.mean(y * y, axis=-1, keepdims=True) + EPS) * g


def _put_cols(ref, rows, val):
    for h in range(ref.shape[0]):
        ref[h, rows, :] = val[:, h * LANES:(h + 1) * LANES]


def _get_cols(ref, rows):
    return jnp.concatenate([ref[h, rows, :] for h in range(ref.shape[0])], axis=-1)


def _dot3(a, b):
    ah = a.astype(bf16)
    al = (a - ah.astype(f32)).astype(bf16)
    bh = b.astype(bf16)
    bl = (b - bh.astype(f32)).astype(bf16)
    dot = functools.partial(jnp.dot, preferred_element_type=f32)
    return dot(ah, bh) + (dot(ah, bl) + dot(al, bh))


def _sigmoid(x):
    return 0.5 * jnp.tanh(0.5 * x) + 0.5


def _inproj_kernel(x_ref, xp_ref, xn_ref, g_ref, w_ref, hcw_ref, hcb_ref, lcw_ref, lcb_ref,
                   u_ref, x0_ref, hg_ref, xb_ref, lg_ref, xs_ref, *, n_tiles):
    i = pl.program_id(1)
    g = g_ref[...]
    for a in range(SUBLANES):
        _put_cols(xs_ref, pl.ds(a, NB, stride=SUBLANES), _rms(x_ref[a * NB:(a + 1) * NB, :], g))
    _put_cols(xs_ref, pl.ds(TILE, HALO), jnp.where(i > 0, _rms(xp_ref[...], g), 0.0))
    _put_cols(xs_ref, pl.ds(TILE + HALO, HALO), jnp.where(i < n_tiles - 1, _rms(xn_ref[...], g), 0.0))
    xn = _get_cols(xs_ref, pl.ds(0, TILE + 2 * HALO)).astype(bf16)
    sub = lax.broadcasted_iota(jnp.int32, (SUBLANES, D_HY), 0)

    def proj(c0):
        p = jnp.dot(xn, w_ref[:, c0:c0 + D_HY], preferred_element_type=f32)
        return p.reshape(NB + 2, SUBLANES, D_HY)

    def edge(p3, s):
        if s < 0:
            return jnp.where(sub == 0, p3[NB][SUBLANES + s:SUBLANES + s + 1],
                             pltpu.roll(p3[NB + s], 1, 0))
        return jnp.where(sub == SUBLANES - 1, p3[NB + 1][s - NB:s - NB + 1],
                         pltpu.roll(p3[s - NB], SUBLANES - 1, 0))

    def conv(p3, cw_ref, cb_ref, c0, offsets):
        lo, hi = max(0, -min(offsets)), NB - max(offsets)

        def acc(get):
            y = cb_ref[:, c0:c0 + D_HY]
            for k, o in enumerate(offsets):
                y = y + get(o) * cw_ref[k:k + 1, c0:c0 + D_HY]
            return y

        inner = acc(lambda o: p3[lo + o:hi + o])
        edges = {b: acc(lambda o, b=b: p3[b + o] if 0 <= b + o < NB else edge(p3, b + o))
                 for b in list(range(lo)) + list(range(hi, NB))}
        return lo, hi, inner, edges

    def store(ref, conv_out, other=None):
        lo, hi, inner, edges = conv_out
        if other is not None:
            inner = inner * other[2]
            edges = {b: edges[b] * other[3][b] for b in edges}
        if len(ref.shape) == 3:
            ref[lo:hi] = inner
            for b, y in edges.items():
                ref[b] = y
        else:
            for cb in range(NCB):
                cols = slice(cb * FFT_CB, (cb + 1) * FFT_CB)
                ref[cb, lo:hi] = inner[:, :, cols]
                for b, y in edges.items():
                    ref[cb, b] = y[:, cols]

    hy = (-1, 0, 1)
    store(u_ref, conv(proj(0), hcw_ref, hcb_ref, 0, hy), conv(proj(2 * D_HY), hcw_ref, hcb_ref, 2 * D_HY, hy))
    lo, hi, inner, edges = conv(proj(D_HY), hcw_ref, hcb_ref, D_HY, hy)
    x0 = jnp.concatenate([edges[b] for b in range(lo)] + [inner.reshape((hi - lo) * SUBLANES, D_HY)]
                         + [edges[b] for b in range(hi, NB)], axis=0)
    for cb in range(NCB):
        x0_ref[cb] = x0[:, cb * FFT_CB:(cb + 1) * FFT_CB].astype(bf16)
    store(xb_ref, conv(proj(4 * D_HY), lcw_ref, lcb_ref, 0, (-1, 0, 1, 2)))
    xm = xn[:TILE]
    hg = jnp.dot(xm, w_ref[:, 3 * D_HY:4 * D_HY], preferred_element_type=f32)
    hg_ref[...] = (hg * _sigmoid(hg)).astype(bf16)
    lg = jnp.dot(xm, w_ref[:, 4 * D_HY + D_LRU:], preferred_element_type=f32)
    lg_ref[...] = (lg * _sigmoid(lg)).astype(bf16)


def _ba_spec(c):
    return pl.BlockSpec((None, NB, None, SUBLANES, c), lambda b, i: (b // 2, 0, b % 2, i, 0))


def _gate_spec(c):
    return pl.BlockSpec((None, None, TILE, c), lambda b, i: (b, i, 0, 0))


def _inproj(x, norm_g, w_in, hcw, hcb, lcw, lcb):
    B, L, D = x.shape
    n_tiles = L // TILE
    ha = L // NB
    hb = TILE // HALO
    n_hb = L // HALO
    const = lambda b, i: (0, 0)
    ba_shape = jax.ShapeDtypeStruct((B // 2, NB, 2, ha, D_HY), f32)
    cb_shape = jax.ShapeDtypeStruct((B // 2, NCB, NB, 2, ha, FFT_CB), f32)
    cb_spec = pl.BlockSpec((None, NCB, NB, None, SUBLANES, FFT_CB), lambda b, i: (b // 2, 0, 0, b % 2, i, 0))
    gate_shape = jax.ShapeDtypeStruct((B, n_tiles, TILE, D_HY), bf16)
    return pl.pallas_call(
        functools.partial(_inproj_kernel, n_tiles=n_tiles),
        grid=(B, n_tiles),
        in_specs=[
            pl.BlockSpec((None, TILE, D), lambda b, i: (b, i, 0)),
            pl.BlockSpec((None, HALO, D), lambda b, i: (b, jnp.maximum(i * hb - 1, 0), 0)),
            pl.BlockSpec((None, HALO, D), lambda b, i: (b, jnp.minimum((i + 1) * hb, n_hb - 1), 0)),
            pl.BlockSpec((1, D), const),
            pl.BlockSpec(w_in.shape, const, pipeline_mode=pl.Buffered(1)),
            pl.BlockSpec(hcw.shape, const),
            pl.BlockSpec(hcb.shape, const),
            pl.BlockSpec(lcw.shape, const),
            pl.BlockSpec(lcb.shape, const),
        ],
        out_specs=[cb_spec, pl.BlockSpec((None, None, NCB, TILE, FFT_CB), lambda b, i: (b, i, 0, 0, 0)),
                   _gate_spec(D_HY), _ba_spec(D_LRU), _gate_spec(D_LRU)],
        out_shape=[cb_shape, jax.ShapeDtypeStruct((B, n_tiles, NCB, TILE, FFT_CB), bf16),
                   gate_shape, ba_shape, gate_shape],
        scratch_shapes=[pltpu.VMEM((D // LANES, TILE + 2 * HALO, LANES), f32)],
        compiler_params=_params("parallel", "arbitrary"),
        name="inproj",
    )(x, x, x, norm_g, w_in, hcw, hcb, lcw, lcb)


def _filt_kernel(z_ref, dl_ref, w1_ref, b1_ref, f1_ref, w2_ref, b2_ref, f2_ref,
                 w3_ref, b3_ref, f3_ref, w4f_ref, w4b_ref, of_ref, ob_ref, s_ref, *, ha):
    dot = _dot3
    dl = dl_ref[...]
    z = z_ref[...]
    half = z.shape[0] // 2
    h = jnp.concatenate([dot(z[:half], w1_ref[...]), dot(z[half:], w1_ref[...])], axis=-1)
    h = jnp.sin(f1_ref[...] * (h + b1_ref[...]))
    h = jnp.sin(f2_ref[...] * (dot(h, w2_ref[...]) + b2_ref[...]))
    h = jnp.sin(f3_ref[...] * (dot(h, w3_ref[...]) + b3_ref[...]))
    nh = h.shape[-1] // 2
    h = jnp.concatenate([h[:, :nh], h[:, nh:]], axis=0)
    win = jnp.exp(-z[:, 0:1] * dl)
    nf = SUBLANES * ha
    hf = dot(h[:nf], w4f_ref[...]) * win[:nf]
    hb = dot(h[ha:], w4b_ref[...]) * (win[ha:] * z[ha:, MASK_COL:MASK_COL + 1])
    for cb in range(NCB):
        cols = slice(cb * FFT_CB, (cb + 1) * FFT_CB)
        of_ref[cb] = hf[:, cols].reshape(SUBLANES, ha, FFT_CB)
        for i in range(SUBLANES):
            ob_ref[cb, SUBLANES - 1 - i] = hb[i * ha:(i + 1) * ha, cols]

    @pl.when(pl.program_id(0) == 0)
    def _():
        s_ref[...] = jnp.zeros_like(s_ref)

    s_ref[...] += jnp.sum(jnp.abs(hf), axis=0, keepdims=True) + jnp.sum(jnp.abs(hb), axis=0, keepdims=True)


@functools.lru_cache(maxsize=None)
def _filter_tables(L):
    ha = L // NB
    t = np.linspace(0.0, 1.0, L)
    w = (2.0 * math.pi / L) * np.arange(L)
    f = np.linspace(1e-4, FILTER_BANDS - 1, FILTER_BANDS)[None, :]
    j = np.arange(NB // SUBLANES)[:, None, None]
    r = np.arange(SUBLANES + 1)[None, :, None]
    q = np.arange(ha)[None, None, :]
    lag = (NB * q + SUBLANES * j + r).reshape(-1)
    valid = lag < L
    lag = np.where(valid, lag, 0)
    tl, wl = t[lag][:, None], w[lag][:, None]
    pad = np.zeros((lag.shape[0], LANES - FILTER_EMB - 1))
    z = np.concatenate([tl, np.cos(wl * f), -np.sin(wl * f), valid[:, None].astype(np.float64), pad], axis=-1)
    dl = np.abs(np.linspace(MIN_DECAY, MAX_DECAY, D_HY))[None, :].astype(np.float32)
    return z.astype(np.float32), dl


def _filter(L, w1, b1, f1, w2, b2, f2, w3, b3, f3, w4):
    ha = L // NB
    assert ha % (2 * SUBLANES) == 0
    rz = (SUBLANES + 1) * ha
    z, dl = (jnp.asarray(t) for t in _filter_tables(L))
    w1p = jnp.pad(w1, ((0, LANES - w1.shape[0]), (0, 0)))
    row = lambda v: jnp.tile(v.reshape(1, -1), (1, 2))
    zero = jnp.zeros_like(w2)
    diag2 = lambda w: jnp.concatenate([jnp.concatenate([w, zero], axis=1),
                                       jnp.concatenate([zero, w], axis=1)], axis=0)
    const = lambda j: (0, 0)
    full = lambda arr: pl.BlockSpec(arr.shape, const)
    args = [z, dl, w1p, row(b1), row(f1), diag2(w2), row(b2), row(f2), diag2(w3), row(b3), row(f3), w4, w4]
    specs = [full(a) for a in args]
    specs[0] = pl.BlockSpec((rz, LANES), lambda j: (j, 0))
    specs[11] = pl.BlockSpec((w4.shape[0], D_HY), lambda j: (0, 0))
    specs[12] = pl.BlockSpec((w4.shape[0], D_HY), lambda j: (0, 1))
    n_steps = NB // SUBLANES
    half_shape = jax.ShapeDtypeStruct((NCB, NB, ha, FFT_CB), f32)
    return pl.pallas_call(
        functools.partial(_filt_kernel, ha=ha),
        grid=(n_steps,),
        in_specs=specs,
        out_specs=[pl.BlockSpec((NCB, SUBLANES, ha, FFT_CB), lambda j: (0, j, 0, 0)),
                   pl.BlockSpec((NCB, SUBLANES, ha, FFT_CB), lambda j: (0, n_steps - 1 - j, 0, 0)),
                   pl.BlockSpec((1, D_HY), const)],
        out_shape=[half_shape, half_shape, jax.ShapeDtypeStruct((1, D_HY), f32)],
        compiler_params=_params("arbitrary"),
        name="filt",
    )(*args)


@functools.lru_cache(maxsize=None)
def _dft_matrices(L):
    n = 2 * L
    na = n // NB
    ha = na // 2
    b = np.arange(NB)[:, None, None]
    p = np.arange(na)[None, :, None]
    a = np.arange(na)[None, None, :]
    ang = (2.0 * math.pi / n) * ((p * (NB * a + b)) % n)
    c, s = np.cos(ang), np.sin(ang)
    ch, sh = c[:, :, :ha], s[:, :, :ha]
    m_data = np.concatenate([np.concatenate([ch, sh], axis=2),
                             np.concatenate([-sh, ch], axis=2)], axis=1)
    cr = np.concatenate([ch, c[:, :, :ha - 1:-1]], axis=2)
    sr = np.concatenate([sh, s[:, :, :ha - 1:-1]], axis=2)
    m_real = np.concatenate([cr, -sr], axis=1)
    cht, sht = np.swapaxes(ch, 1, 2), np.swapaxes(sh, 1, 2)
    m_inv = np.concatenate([np.concatenate([cht, -sht], axis=2),
                            np.concatenate([sht, cht], axis=2)], axis=1)
    q = np.arange(NB)
    gang = (2.0 * math.pi / NB) * ((q[:, None] * q[None, :]) % NB)
    gc, gs = np.cos(gang), np.sin(gang)
    g_fwd = np.concatenate([np.concatenate([gc, gs], axis=1),
                            np.concatenate([-gs, gc], axis=1)], axis=0)
    g_inv = np.concatenate([np.concatenate([gc, -gs], axis=1),
                            np.concatenate([gs, gc], axis=1)], axis=0)
    return tuple(m.astype(np.float32) for m in (m_data, m_real, m_inv, g_fwd, g_inv))


def _first_stage(x_ref, m_ref, s_ref, s, na, x2_ref=None):
    nb = x_ref.shape[0]
    for j in range(nb):
        x = x_ref[j] if x2_ref is None else jnp.concatenate([x_ref[j], x2_ref[j]], axis=0)
        res = jnp.dot(m_ref[j], x.astype(bf16), preferred_element_type=f32)
        rows = pl.ds(s * nb + j, na, stride=FFT_PITCH)
        _put_cols(s_ref.at[0], rows, res[:na])
        _put_cols(s_ref.at[1], rows, res[na:])


def _slab(p):
    return pl.ds(pl.multiple_of(p * FFT_PITCH, SUBLANES), NB)


def _second_stage(s_ref, gf_ref, p):
    rows = _slab(p)
    y = jnp.concatenate([_get_cols(s_ref.at[0], rows), _get_cols(s_ref.at[1], rows)], axis=0)
    return jnp.dot(gf_ref[...], y.astype(bf16), preferred_element_type=f32)


def _fftk_kernel(kf_ref, kb_ref, ma_ref, l1_ref, sk_ref, gf_ref, o_ref, s_ref, *, na, a_steps, scale):
    s = pl.program_id(1)

    @pl.when(s < a_steps)
    def _():
        _first_stage(kf_ref, ma_ref, s_ref, s, na, kb_ref)

    @pl.when(s >= a_steps)
    def _():
        inv = scale / (l1_ref[...] + EPS)
        tap = scale * sk_ref[...]
        for j in range(FFTK_PB):
            z = _second_stage(s_ref, gf_ref, (s - a_steps) * FFTK_PB + j)
            o_ref[0, j] = (z[:NB] * inv + tap).astype(o_ref.dtype)
            o_ref[1, j] = (z[NB:] * inv).astype(o_ref.dtype)


def _fft_k(kt_f, kt_b, l1, skip, m_real, g_fwd, scale):
    na = 2 * kt_f.shape[2]
    a_steps, b_steps = NB // FFTK_BA, na // FFTK_PB
    last_a = a_steps - 1
    const = lambda c, s: (0, 0)
    return pl.pallas_call(
        functools.partial(_fftk_kernel, na=na, a_steps=a_steps, scale=scale),
        grid=(NCB, a_steps + b_steps),
        in_specs=[pl.BlockSpec((None, FFTK_BA, na // 2, FFT_CB), lambda c, s: (c, jnp.minimum(s, last_a), 0, 0)),
                  pl.BlockSpec((None, FFTK_BA, na // 2, FFT_CB), lambda c, s: (c, jnp.minimum(s, last_a), 0, 0)),
                  pl.BlockSpec((FFTK_BA, 2 * na, na), lambda c, s: (jnp.minimum(s, last_a), 0, 0)),
                  pl.BlockSpec((1, FFT_CB), lambda c, s: (0, c)),
                  pl.BlockSpec((1, FFT_CB), lambda c, s: (0, c)),
                  pl.BlockSpec(g_fwd.shape, const)],
        out_specs=pl.BlockSpec((2, None, FFTK_PB, NB, FFT_CB),
                               lambda c, s: (0, c, jnp.clip(s - a_steps, 0, b_steps - 1), 0, 0)),
        out_shape=jax.ShapeDtypeStruct((2, NCB, na, NB, FFT_CB), bf16),
        scratch_shapes=[pltpu.VMEM((2, FFT_CB // LANES, na * FFT_PITCH, LANES), f32)],
        compiler_params=_params("arbitrary", "arbitrary"),
        name="fft_k",
    )(kt_f, kt_b, m_real, l1, skip, g_fwd)


def _fftconv_kernel(ua_ref, ma_ref, k_ref, gf_ref, gi_ref, mc_ref, x0_ref, o_ref, s_ref,
                    *, na, a_steps, b_steps):
    s = pl.program_id(2)

    @pl.when(s < a_steps)
    def _():
        _first_stage(ua_ref, ma_ref, s_ref, s, na)

    @pl.when((s >= a_steps) & (s < a_steps + b_steps))
    def _():
        for j in range(FFT_PB):
            p = (s - a_steps) * FFT_PB + j
            z = _second_stage(s_ref, gf_ref, p)
            zr, zi = z[:NB], z[NB:]
            kr, ki = k_ref[0, j], k_ref[1, j]
            f = jnp.concatenate([zr * kr - zi * ki, zr * ki + zi * kr], axis=0)
            v = jnp.dot(gi_ref[...], f.astype(bf16), preferred_element_type=f32)
            _put_cols(s_ref.at[0], _slab(p), v[:NB])
            _put_cols(s_ref.at[1], _slab(p), v[NB:])

    @pl.when(s >= a_steps + b_steps)
    def _():
        def last_stage(j):
            rows = pl.ds((s - (a_steps + b_steps)) * FFT_BC + j, na, stride=FFT_PITCH)
            v = jnp.concatenate([_get_cols(s_ref.at[0], rows), _get_cols(s_ref.at[1], rows)], axis=0)
            return jnp.dot(mc_ref[j], v.astype(bf16), preferred_element_type=f32)

        ha = na // 2
        for jj in range(FFT_BC // 2):
            y0, y1 = last_stage(2 * jj), last_stage(2 * jj + 1)
            rows = slice(2 * SUBLANES * jj, 2 * SUBLANES * (jj + 1))
            for half in range(2):
                for t in range(ha // SUBLANES):
                    r0 = half * ha + t * SUBLANES
                    y = jnp.concatenate([y0[r0:r0 + SUBLANES], y1[r0:r0 + SUBLANES]], axis=0)
                    o_ref[half, t, rows, :] = (x0_ref[half, t, rows, :] * y).astype(bf16)


def _fft_conv(u, x0, kf, m_data, m_inv, g_fwd, g_inv):
    P, _, _, na, _ = u.shape
    a_steps, b_steps, c_steps = NB // FFT_BA, na // FFT_PB, NB // FFT_BC
    n_tiles = na // (2 * SUBLANES)
    last_a = a_steps - 1
    b_idx = lambda s: jnp.clip(s - a_steps, 0, b_steps - 1)
    c_idx = lambda s: jnp.clip(s - (a_steps + b_steps), 0, c_steps - 1)
    const = lambda c, r, s: (0, 0)
    c_spec = pl.BlockSpec((2, n_tiles, None, FFT_BC * SUBLANES, FFT_CB), lambda c, r, s: (r, 0, c, c_idx(s), 0))
    return pl.pallas_call(
        functools.partial(_fftconv_kernel, na=na, a_steps=a_steps, b_steps=b_steps),
        grid=(NCB, P, a_steps + b_steps + c_steps),
        in_specs=[pl.BlockSpec((None, None, FFT_BA, na, FFT_CB), lambda c, r, s: (r, c, jnp.minimum(s, last_a), 0, 0)),
                  pl.BlockSpec((FFT_BA, 2 * na, na), lambda c, r, s: (jnp.minimum(s, last_a), 0, 0)),
                  pl.BlockSpec((2, None, FFT_PB, NB, FFT_CB), lambda c, r, s: (0, c, b_idx(s), 0, 0)),
                  pl.BlockSpec(g_fwd.shape, const),
                  pl.BlockSpec(g_inv.shape, const),
                  pl.BlockSpec((FFT_BC, na, 2 * na), lambda c, r, s: (c_idx(s), 0, 0)),
                  c_spec],
        out_specs=c_spec,
        out_shape=jax.ShapeDtypeStruct(x0.shape, bf16),
        scratch_shapes=[pltpu.VMEM((2, FFT_CB // LANES, na * FFT_PITCH, LANES), f32)],
        compiler_params=_params("arbitrary", "arbitrary", "arbitrary"),
        name="fft_conv",
    )(u, m_data, kf, g_fwd, g_inv, m_inv, x0)


def _carry_scan(at, bt, reverse):
    n = at.shape[0]
    row = lax.broadcasted_iota(jnp.int32, at.shape, 0)
    s = 1
    while s < n:
        keep = (row < n - s) if reverse else (row >= s)
        shift = n - s if reverse else s
        ash = jnp.where(keep, pltpu.roll(at, shift, 0), 1.0)
        bsh = jnp.where(keep, pltpu.roll(bt, shift, 0), 0.0)
        bt = at * bsh + bt
        at = at * ash
        s *= 2
    if reverse:
        return jnp.where(row < n - 1, pltpu.roll(bt, n - 1, 0), 0.0)
    return jnp.where(row >= 1, pltpu.roll(bt, 1, 0), 0.0)


def _lru_kernel(xb_ref, wg_ref, bg_ref, lam_ref, o_ref, hf_ref, af_ref, hb_ref, ab_ref, *, ha):
    bt = LRU_BT
    nc = NB // bt
    lam = lam_ref[...]
    half_c = (-0.5 * LRU_C) * (jnp.maximum(-lam, 0.0) + jnp.log1p(jnp.exp(-jnp.abs(lam))))
    rowi = lax.broadcasted_iota(jnp.int32, (bt * ha, 1), 0)

    def gates(d, k, first):
        b0 = pl.multiple_of(k * bt, bt)
        xc = xb_ref[pl.ds(b0, bt)].reshape(bt * ha, LANES)
        t = jnp.tanh(jnp.dot(xc.astype(bf16), wg_ref[d], preferred_element_type=f32) + bg_ref[d:d + 1, :])
        hc = half_c[d:d + 1, :]
        log_a = hc * t[:, :LANES] + hc
        a = jnp.exp(log_a)
        m2 = jnp.tanh(log_a) * (-1.0 - a * a)
        mult = jnp.where(m2 > 0.0, m2 * lax.rsqrt(m2), 0.0)
        if first:
            mult = jnp.where(rowi == (bt * ha - 1 if d else 0), 1.0, mult)
        hx = 0.5 * xc
        bv = mult * (t[:, LANES:] * hx + hx)
        return b0, a.reshape(bt, ha, LANES), bv.reshape(bt, ha, LANES)

    def step(k, carry, first=False):
        hf, af, hb, ab = carry
        b0, a, bv = gates(0, k, first)
        for j in range(bt):
            hf = a[j] * hf + bv[j]
            af = a[j] * af
            hf_ref[b0 + j] = hf
            af_ref[b0 + j] = af
        b0, a, bv = gates(1, nc - 1 - k, first)
        for j in reversed(range(bt)):
            hb = a[j] * hb + bv[j]
            ab = a[j] * ab
            hb_ref[b0 + j] = hb
            ab_ref[b0 + j] = ab
        return hf, af, hb, ab

    zero = jnp.zeros((ha, LANES), f32)
    one = jnp.ones((ha, LANES), f32)
    carry = step(0, (zero, one, zero, one), first=True)
    hf, af, hb, ab = lax.fori_loop(1, nc, step, carry)
    cf = _carry_scan(af, hf, reverse=False)
    cb = _carry_scan(ab, hb, reverse=True)

    fb = SUBLANES

    def finish(k, c):
        sl = pl.ds(pl.multiple_of(k * fb, fb), fb)
        h = (hf_ref[sl] + af_ref[sl] * cf) + (hb_ref[sl] + ab_ref[sl] * cb)
        r0 = pl.multiple_of(k * (fb * SUBLANES), fb * SUBLANES)
        for t in range(ha // SUBLANES):
            tile = h[:, t * SUBLANES:(t + 1) * SUBLANES, :].reshape(fb * SUBLANES, LANES)
            o_ref[t, pl.ds(r0, fb * SUBLANES), :] = tile.astype(bf16)
        return c

    lax.fori_loop(0, NB // fb, finish, 0, unroll=4)


def _lru_gate_weights(wa, wx):
    def blockdiag(w):
        w = w.reshape(2, -1, 2, HEAD, HEAD)
        z = jnp.zeros_like(w[:, :, 0])
        top = jnp.concatenate([w[:, :, 0], z], axis=-1)
        bot = jnp.concatenate([z, w[:, :, 1]], axis=-1)
        return jnp.concatenate([top, bot], axis=-2)
    return jnp.concatenate([blockdiag(wa), blockdiag(wx)], axis=-1).astype(bf16)


def _lru(xb, wa, ba, wx, bx, lam):
    P, _, _, ha, C = xb.shape
    nblk = C // LANES
    wg = _lru_gate_weights(0.5 * wa, 0.5 * wx)
    bg = 0.5 * jnp.concatenate([ba.reshape(2, nblk, 1, LANES), bx.reshape(2, nblk, 1, LANES)], axis=-1)
    ba_spec = pl.BlockSpec((None, NB, None, ha, LANES), lambda b, c: (b // 2, 0, b % 2, 0, c))
    return pl.pallas_call(
        functools.partial(_lru_kernel, ha=ha),
        grid=(2 * P, nblk),
        in_specs=[ba_spec,
                  pl.BlockSpec((2, None, LANES, 2 * LANES), lambda b, c: (0, c, 0, 0)),
                  pl.BlockSpec((2, None, None, 2 * LANES), lambda b, c: (0, c, 0, 0)),
                  pl.BlockSpec((2, LANES), lambda b, c: (0, c))],
        out_specs=pl.BlockSpec((None, ha // SUBLANES, TILE, LANES), lambda b, c: (b, 0, 0, c)),
        out_shape=jax.ShapeDtypeStruct((2 * P, ha // SUBLANES, TILE, C), bf16),
        scratch_shapes=[pltpu.VMEM((NB, ha, LANES), f32)] * 4,
        compiler_params=_params("parallel", "arbitrary"),
        name="lru",
    )(xb, wg, bg, lam)


def _out_kernel(yh_ref, hg_ref, yl_ref, lg_ref, x_ref, hog_ref, log_ref, wo_ref, fg_ref, o_ref, ys_ref):
    yh = jnp.concatenate([yh_ref[cb] for cb in range(NCB)], axis=-1)
    ycat = jnp.concatenate([_rms(yh.astype(f32), hog_ref[...]) * hg_ref[...],
                            _rms(yl_ref[...].astype(f32), log_ref[...]) * lg_ref[...]], axis=-1)
    y = jnp.dot(ycat.astype(bf16), wo_ref[...], preferred_element_type=f32)
    _put_cols(ys_ref, pl.ds(0, TILE), y)
    for a in range(SUBLANES):
        rows = slice(a * NB, (a + 1) * NB)
        ya = _get_cols(ys_ref, pl.ds(a, NB, stride=SUBLANES))
        o_ref[rows, :] = _rms(x_ref[rows, :] + ya, fg_ref[...])


def _out(yh, hg, yl, lg, x, hog, log_g, w_out, fg):
    B, L, D = x.shape
    const = lambda b, i: (0, 0)
    nat = pl.BlockSpec((None, TILE, D), lambda b, i: (b, i, 0))
    return pl.pallas_call(
        _out_kernel,
        grid=(B, L // TILE),
        in_specs=[pl.BlockSpec((None, None, NCB, TILE, FFT_CB), lambda b, i: (b, i, 0, 0, 0)),
                  _gate_spec(D_HY), _gate_spec(D_LRU), _gate_spec(D_LRU), nat,
                  pl.BlockSpec((1, D_HY), const), pl.BlockSpec((1, D_LRU), const),
                  pl.BlockSpec(w_out.shape, const), pl.BlockSpec((1, D), const)],
        out_specs=nat,
        out_shape=jax.ShapeDtypeStruct((B, L, D), f32),
        scratch_shapes=[pltpu.VMEM((D // LANES, TILE, LANES), f32)],
        compiler_params=_params("parallel", "arbitrary"),
        name="out",
    )(yh, hg, yl, lg, x, hog, log_g, w_out, fg)


def kernel(x, norm_g, w_in, hy_conv_w, hy_conv_b, flt_w1, flt_b1, flt_f1, flt_w2, flt_b2, flt_f2,
           flt_w3, flt_b3, flt_f3, flt_w4, hy_skip, lru_conv_w, lru_conv_b, lru_wa, lru_ba, lru_wx,
           lru_bx, lru_lam, hy_out_g, lru_out_g, w_out, final_g):
    B, L, D = x.shape
    assert norm_g.shape[0] == 1, "one layer"
    assert B % 2 == 0 and L % TILE == 0
    ha = L // NB
    na = 2 * ha
    row = lambda v: v.reshape(1, -1)

    u, x0, hg, xb, lg = _inproj(x, row(norm_g[0]), w_in[0].astype(bf16), hy_conv_w[0], row(hy_conv_b[0]),
                                lru_conv_w[0], row(lru_conv_b[0]))
    pair = lambda t: t.reshape(B // 2, NCB, NB, na, FFT_CB)

    m_data, m_real, m_inv, g_fwd, g_inv = (jnp.asarray(m).astype(bf16) for m in _dft_matrices(L))
    kt_f, kt_b, l1 = _filter(L, flt_w1[0], flt_b1[0], flt_f1[0], flt_w2[0], flt_b2[0], flt_f2[0],
                             flt_w3[0], flt_b3[0], flt_f3[0], flt_w4[0])
    kf = _fft_k(kt_f, kt_b, l1, row(hy_skip[0]), m_real, g_fwd, scale=1.0 / (2 * L))
    yh = _fft_conv(pair(u), x0, kf, m_data, m_inv, g_fwd, g_inv)

    yl = _lru(xb, lru_wa[0], lru_ba[0], lru_wx[0], lru_bx[0], lru_lam[0])
    return _out(yh, hg, yl, lg, x, row(hy_out_g[0]), row(lru_out_g[0]), w_out[0].astype(bf16), row(final_g))
```
